```python
import jax
import jax.numpy as jnp
from jax import lax
import numpy as np

D_MODEL = 1024
BATCH = 8
SEQ = 2048
DEPTH = 4
DEC_BATCH = 32
DEC_SEQ = 1
PAST_LEN = 8192
PAGE_SIZE = 128

N_BRANCH = 4
HEADS = 4
BRANCH_W = D_MODEL // N_BRANCH
HEAD_DIM = BRANCH_W // HEADS
FFN_DIM = ((8 * D_MODEL // 3 + 127) // 128) * 128
DECAY_LORA = 64
AAA_LORA = 64
GATE_LORA = 128
RWKV_IN = 3 * BRANCH_W + DECAY_LORA + AAA_LORA + GATE_LORA
Q_BLOCK = 128
RET_CHUNK = 128
ROPE_BASE = 10000.0
RMS_EPS = 1e-6
GN_EPS = 1e-5
RWKV_GN_EPS = 64e-5
FORGET_BIAS = 3.0
IN_SPLITS = [BRANCH_W] * 3 + [HEADS] + [BRANCH_W] * 4 + [RWKV_IN] + [BRANCH_W] * 3 + [N_BRANCH * D_MODEL]
IN_COLS = sum(IN_SPLITS)

kernel_name = 'hybrid_fox_retnet_rwkv7_stickbreak_step'


def _split(h, sizes):
    offs = np.cumsum([0] + list(sizes))
    return [h[..., int(offs[i]):int(offs[i + 1])] for i in range(len(sizes))]


def _heads(x):
    return x.reshape(x.shape[:-1] + (HEADS, HEAD_DIM))


def _rmsnorm(x, g):
    xf = x.astype(jnp.float32)
    y = xf * lax.rsqrt(jnp.mean(xf * xf, axis=-1, keepdims=True) + RMS_EPS)
    return (y * g.astype(jnp.float32)).astype(x.dtype)


def _group_norm(y, eps):
    yf = y.astype(jnp.float32)
    mu = jnp.mean(yf, axis=-1, keepdims=True)
    var = jnp.mean(jnp.square(yf - mu), axis=-1, keepdims=True)
    out = (yf - mu) * lax.rsqrt(var + eps)
    return out.reshape(out.shape[:-2] + (HEADS * HEAD_DIM,))


def _rotary(x, pos):
    half = HEAD_DIM // 2
    inv = ROPE_BASE ** (-jnp.arange(half, dtype=jnp.float32) / half)
    ang = pos.astype(jnp.float32)[:, None] * inv[None, :]
    cos = jnp.cos(ang)[None, :, None, :]
    sin = jnp.sin(ang)[None, :, None, :]
    xf = x.astype(jnp.float32)
    x1, x2 = xf[..., :half], xf[..., half:]
    return jnp.concatenate([x1 * cos - x2 * sin, x1 * sin + x2 * cos], axis=-1)


def _sweep_queries(block_fn, per_query, qpos):
    t = qpos.shape[0]
    if t > Q_BLOCK and t % Q_BLOCK == 0:
        nb = t // Q_BLOCK
        blocks = tuple(a.reshape((a.shape[0], nb, Q_BLOCK) + a.shape[2:]).swapaxes(0, 1) for a in per_query)
        out = lax.map(lambda xs: block_fn(xs[0], xs[1]), (blocks, qpos.reshape(nb, Q_BLOCK)))
        out = out.swapaxes(0, 1)
        return out.reshape((out.shape[0], t) + out.shape[3:])
    return block_fn(per_query, qpos)


def _fox_block(qs, qpos, k, v, cum_k, kpos):
    q, cum_q = qs
    s = jnp.einsum('bqhd,bkhd->bhqk', q, k).astype(jnp.float32) * (HEAD_DIM ** -0.5)
    s = s + jnp.swapaxes(cum_q, 1, 2)[..., :, None] - jnp.swapaxes(cum_k, 1, 2)[..., None, :]
    mask = kpos[None, :] <= qpos[:, None]
    p = jax.nn.softmax(jnp.where(mask, s, -jnp.inf), axis=-1)
    return jnp.einsum('bhqk,bkhd->bqhd', p.astype(v.dtype), v)


def _sb_block(qs, qpos, k, v, kpos):
    q = qs[0]
    z = jnp.einsum('bqhd,bkhd->bhqk', q, k).astype(jnp.float32) * (HEAD_DIM ** -0.5)
    mask = kpos[None, :] < qpos[:, None]
    log1m = jnp.where(mask, jax.nn.log_sigmoid(-z), 0.0)
    excl = lax.cumsum(log1m, axis=3, reverse=True) - log1m
    a = jnp.where(mask, jnp.exp(jax.nn.log_sigmoid(z) + excl), 0.0)
    return jnp.einsum('bhqk,bkhd->bqhd', a.astype(v.dtype), v)


def _retention(q, k, v, s0):
    lg = jnp.log(1.0 - 2.0 ** (-5.0 - jnp.arange(HEADS, dtype=jnp.float32)))
    t = q.shape[1]
    L = RET_CHUNK if t % RET_CHUNK == 0 else t
    nc = t // L
    n = jnp.arange(L, dtype=jnp.float32)
    diff = n[:, None] - n[None, :]
    dmask = jnp.where(diff[None] >= 0, jnp.exp(jnp.maximum(diff, 0.0)[None] * lg[:, None, None]), 0.0)
    xi = jnp.exp((n[:, None] + 1.0) * lg[None, :])
    zeta = jnp.exp((L - 1.0 - n)[:, None] * lg[None, :])
    chunk_decay = jnp.exp(L * lg)

    def step(s, xs):
        qc, kc, vc = xs
        att = jnp.einsum('bnhd,bmhd->bhnm', qc, kc) * dmask[None]
        o = jnp.einsum('bhnm,bmhe->bnhe', att, vc) + jnp.einsum('bnhd,bhde->bnhe', qc, s) * xi[None, :, :, None]
        s = chunk_decay[None, :, None, None] * s + jnp.einsum('bmhd,bmhe,mh->bhde', kc, vc, zeta)
        return s, o

    def to_chunks(a):
        return a.reshape((a.shape[0], nc, L) + a.shape[2:]).swapaxes(0, 1)

    s_t, o = lax.scan(step, s0.astype(jnp.float32), (to_chunks(q), to_chunks(k), to_chunks(v)))
    o = o.swapaxes(0, 1).reshape(q.shape[:2] + (HEADS, HEAD_DIM))
    return o, s_t


def _rwkv7(hr, shift0, s0, mu, w0, w_up, a0, a_up, g_up, k_k, k_a, r_k, ln_w, ln_b):
    f32 = jnp.float32
    prev = jnp.concatenate([shift0[:, None, :].astype(hr.dtype), hr[:, :-1]], axis=1)
    xm = hr + (prev - hr) * mu
    r, k, v, wd, ad, gd = _split(xm, [BRANCH_W] * 3 + [DECAY_LORA, AAA_LORA, GATE_LORA])
    w = -jax.nn.softplus(-(w0 + jnp.tanh(wd) @ w_up).astype(f32)) - 0.5
    decay = jnp.exp(-jnp.exp(w))
    a = jax.nn.sigmoid((a0 + ad @ a_up).astype(f32))
    g = (jax.nn.sigmoid(gd) @ g_up).astype(f32)
    kk = _heads(k.astype(f32) * k_k)
    kk = kk * lax.rsqrt(jnp.sum(kk * kk, axis=-1, keepdims=True) + 1e-12)
    kmod = k.astype(f32) * (1.0 + (a - 1.0) * k_a)
    rh, kh, vh = _heads(r.astype(f32)), _heads(kmod), _heads(v.astype(f32))

    def step(s, xs):
        r_t, k_t, v_t, w_t, kk_t, b_t = xs
        sa = jnp.einsum('bhij,bhj->bhi', s, -kk_t)
        s = s * w_t[:, :, None, :] + sa[..., None] * b_t[:, :, None, :] + v_t[..., None] * k_t[:, :, None, :]
        return s, jnp.einsum('bhij,bhj->bhi', s, r_t)

    def tm(z):
        return jnp.swapaxes(z, 0, 1)

    s_t, y = lax.scan(step, s0.astype(f32),
                      (tm(rh), tm(kh), tm(vh), tm(_heads(decay)), tm(kk), tm(kk * _heads(a))))
    y = _group_norm(tm(y), RWKV_GN_EPS) * ln_w + ln_b
    bonus = jnp.sum(rh * kh * _heads(r_k), axis=-1, keepdims=True) * vh
    out = (y + bonus.reshape(y.shape)) * g
    return out, s_t, hr[:, -1]


def setup_inputs(seed: int = 0) -> dict:
    key = jax.random.key(seed)
    keys = iter(jax.random.split(key, 64))

    def nrm(shape, scale=1.0):
        return scale * jax.random.normal(next(keys), shape, jnp.float32)

    n_pages = PAST_LEN // PAGE_SIZE
    n_used = DEC_BATCH * n_pages
    n_pool = n_used + max(1, n_used // 4)
    page_table = jax.random.permutation(next(keys), n_pool)[:n_used].reshape(DEC_BATCH, n_pages).astype(jnp.int32)
    L, D, BW = DEPTH, D_MODEL, BRANCH_W
    kv_shape = (L, n_pool, PAGE_SIZE, HEADS, HEAD_DIM)
    return {
        'x_prompt': nrm((BATCH, SEQ, D)),
        'x_sample': nrm((DEC_BATCH, DEC_SEQ, D)),
        'cache_fox_k': nrm(kv_shape),
        'cache_fox_v': nrm(kv_shape),
        'cache_fox_logf': jax.nn.log_sigmoid(FORGET_BIAS + nrm((L, n_pool, PAGE_SIZE, HEADS))),
        'cache_sb_k': nrm(kv_shape),
        'cache_sb_v': nrm(kv_shape),
        'state_ret': nrm((L, DEC_BATCH, HEADS, HEAD_DIM, HEAD_DIM), 0.2),
        'state_rwkv': nrm((L, DEC_BATCH, HEADS, HEAD_DIM, HEAD_DIM), 0.2),
        'state_rwkv_shift': nrm((L, DEC_BATCH, RWKV_IN)),
        'page_table': page_table,
        'w_in': nrm((L, D, IN_COLS), D ** -0.5),
        'fox_f_bias': FORGET_BIAS + nrm((L, HEADS), 0.5),
        'ret_gn_w': 1.0 + nrm((L, BW), 0.1),
        'rwkv_mu': jax.random.uniform(next(keys), (L, RWKV_IN), jnp.float32),
        'rwkv_w0': -1.0 + nrm((L, BW), 0.5),
        'rwkv_w_up': nrm((L, DECAY_LORA, BW), 0.1),
        'rwkv_a0': nrm((L, BW), 0.1),
        'rwkv_a_up': nrm((L, AAA_LORA, BW), 0.1),
        'rwkv_g_up': nrm((L, GATE_LORA, BW), GATE_LORA ** -0.5),
        'rwkv_k_k': 0.85 + nrm((L, BW), 0.1),
        'rwkv_k_a': 1.0 + nrm((L, BW), 0.1),
        'rwkv_r_k': nrm((L, BW), 0.1),
        'rwkv_ln_w': 1.0 + nrm((L, BW), 0.1),
        'rwkv_ln_b': nrm((L, BW), 0.01),
        'w_branch': nrm((L, N_BRANCH, BW, D), BW ** -0.5),
        'w_out': nrm((L, D, D), D ** -0.5),
        'norm_ffn1': 1.0 + nrm((L, D), 0.1),
        'ffn1_w_in': nrm((L, D, 2 * FFN_DIM), D ** -0.5),
        'ffn1_w_out': nrm((L, FFN_DIM, D), FFN_DIM ** -0.5),
        'norm_mix': 1.0 + nrm((L, D), 0.1),
        'norm_ffn2': 1.0 + nrm((L, D), 0.1),
        'ffn2_w_in': nrm((L, D, 2 * FFN_DIM), D ** -0.5),
        'ffn2_w_out': nrm((L, FFN_DIM, D), FFN_DIM ** -0.5),
        'norm_final': 1.0 + nrm((D,), 0.1),
    }


def reference(x_prompt, x_sample, cache_fox_k, cache_fox_v, cache_fox_logf, cache_sb_k, cache_sb_v,
              state_ret, state_rwkv, state_rwkv_shift, page_table,
              w_in, fox_f_bias, ret_gn_w, rwkv_mu, rwkv_w0, rwkv_w_up, rwkv_a0, rwkv_a_up, rwkv_g_up,
              rwkv_k_k, rwkv_k_a, rwkv_r_k, rwkv_ln_w, rwkv_ln_b, w_branch, w_out,
              norm_ffn1, ffn1_w_in, ffn1_w_out, norm_mix, norm_ffn2, ffn2_w_in, ffn2_w_out, norm_final):
    f32 = jnp.float32
    n_pages = page_table.shape[1]

    def gather_pages(c):
        g = jnp.take(c, page_table, axis=0)
        return g.reshape((g.shape[0], n_pages * g.shape[2]) + g.shape[3:])

    def ffn(x, g, wi, wo):
        a, b = jnp.split(_rmsnorm(x, g) @ wi, 2, axis=-1)
        return (jax.nn.silu(a) * b) @ wo

    def mix(x, l, fk_p, fv_p, flf_p, sk_p, sv_p, ret_s0, rw_s0, rw_sh0):
        bsz, t, _ = x.shape
        p = fk_p.shape[1]
        h = _rmsnorm(x, norm_mix[l]) @ w_in[l]
        (fq, fk, fv, ff, rq, rk, rv, rg, hr, sq, sk, sv, gl) = _split(h, IN_SPLITS)
        qpos = p + jnp.arange(t)
        kpos = jnp.arange(p + t)

        fk_h, fv_h = _heads(fk), _heads(fv)
        lf = jax.nn.log_sigmoid((ff + fox_f_bias[l]).astype(f32))
        fk_all = jnp.concatenate([fk_p.astype(fk_h.dtype), fk_h], axis=1)
        fv_all = jnp.concatenate([fv_p.astype(fv_h.dtype), fv_h], axis=1)
        cum = jnp.cumsum(jnp.concatenate([flf_p.astype(f32), lf], axis=1), axis=1)
        o_a = _sweep_queries(lambda qs, qp: _fox_block(qs, qp, fk_all, fv_all, cum, kpos),
                             (_heads(fq), cum[:, p:]), qpos)

        q_r = _rotary(_heads(rq), qpos)
        k_r = _rotary(_heads(rk), qpos) * (HEAD_DIM ** -0.5)
        o_b, ret_s = _retention(q_r, k_r, _heads(rv).astype(f32), ret_s0)
        o_b = jax.nn.silu(rg.astype(f32)) * (_group_norm(o_b, GN_EPS) * ret_gn_w[l])

        o_c, rw_s, rw_sh = _rwkv7(hr, rw_sh0, rw_s0, rwkv_mu[l], rwkv_w0[l], rwkv_w_up[l], rwkv_a0[l],
                                  rwkv_a_up[l], rwkv_g_up[l], rwkv_k_k[l], rwkv_k_a[l], rwkv_r_k[l],
                                  rwkv_ln_w[l], rwkv_ln_b[l])

        sk_h, sv_h = _heads(sk), _heads(sv)
        sk_all = jnp.concatenate([sk_p.astype(sk_h.dtype), sk_h], axis=1)
        sv_all = jnp.concatenate([sv_p.astype(sv_h.dtype), sv_h], axis=1)
        o_d = _sweep_queries(lambda qs, qp: _sb_block(qs, qp, sk_all, sv_all, kpos), (_heads(sq),), qpos)

        gates = jax.nn.sigmoid(gl.reshape(bsz, t, N_BRANCH, D_MODEL).astype(f32)).astype(x.dtype)
        branches = (o_a, o_b, o_c, o_d)
        merged = sum(gates[:, :, i] * (branches[i].reshape(bsz, t, BRANCH_W).astype(x.dtype) @ w_branch[l, i])
                     for i in range(N_BRANCH))
        new = (fk_h, fv_h, lf, sk_h, sv_h, ret_s, rw_s, rw_sh)
        return merged @ w_out[l], new

    def layer(x, l, past):
        x = x + 0.5 * ffn(x, norm_ffn1[l], ffn1_w_in[l], ffn1_w_out[l])
        m, new = mix(x, l, *past)
        x = x + m
        x = x + 0.5 * ffn(x, norm_ffn2[l], ffn2_w_in[l], ffn2_w_out[l])
        return x, new

    bp = x_prompt.shape[0]
    xp, xs = x_prompt, x_sample
    new_p = [[] for _ in range(8)]
    new_s = [[] for _ in range(8)]
    for l in range(DEPTH):
        past_p = (jnp.zeros((bp, 0, HEADS, HEAD_DIM), xp.dtype), jnp.zeros((bp, 0, HEADS, HEAD_DIM), xp.dtype),
                  jnp.zeros((bp, 0, HEADS), f32),
                  jnp.zeros((bp, 0, HEADS, HEAD_DIM), xp.dtype), jnp.zeros((bp, 0, HEADS, HEAD_DIM), xp.dtype),
                  jnp.zeros((bp, HEADS, HEAD_DIM, HEAD_DIM), f32), jnp.zeros((bp, HEADS, HEAD_DIM, HEAD_DIM), f32),
                  jnp.zeros((bp, RWKV_IN), xp.dtype))
        xp, np_l = layer(xp, l, past_p)
        past_s = (gather_pages(cache_fox_k[l]), gather_pages(cache_fox_v[l]), gather_pages(cache_fox_logf[l]),
                  gather_pages(cache_sb_k[l]), gather_pages(cache_sb_v[l]),
                  state_ret[l], state_rwkv[l], state_rwkv_shift[l])
        xs, ns_l = layer(xs, l, past_s)
        for i in range(8):
            new_p[i].append(np_l[i])
            new_s[i].append(ns_l[i])

    y_prompt = _rmsnorm(xp, norm_final)
    y_sample = _rmsnorm(xs, norm_final)
    sp = [jnp.stack(v) for v in new_p]
    ss = [jnp.stack(v) for v in new_s]
    return (y_prompt, y_sample, sp[0], sp[1], sp[2], sp[3], sp[4], sp[5], sp[6], sp[7],
            ss[0], ss[1], ss[2], ss[3], ss[4], ss[5], ss[6], ss[7])
```

```python
import functools
import math

import jax
import jax.numpy as jnp
import numpy as np
from jax import lax
from jax.experimental import pallas as pl
from jax.experimental.pallas import tpu as pltpu

F32 = jnp.float32
BF16 = jnp.bfloat16
HIGHEST = lax.Precision.HIGHEST

D_MODEL = 1024
HEADS = 4
HEAD_DIM = 64
BRANCH_W = HEADS * HEAD_DIM
N_BRANCH = 4
DECAY_LORA = 64
AAA_LORA = 64
GATE_LORA = 128
RWKV_IN = 3 * BRANCH_W + DECAY_LORA + AAA_LORA + GATE_LORA
PAGE_SIZE = 128
RET_CHUNK = 128
ROPE_BASE = 10000.0
RMS_EPS = 1e-6
GN_EPS = 1e-5
RWKV_GN_EPS = 64e-5
QK_SCALE = HEAD_DIM ** -0.5
NEG_BIG = -1e30

COL_GATE = 0
COL_FOX = N_BRANCH * D_MODEL
COL_RET = COL_FOX + 3 * BRANCH_W
COL_RWKV = COL_RET + 4 * BRANCH_W
COL_SB = COL_RWKV + RWKV_IN
COL_FORGET = COL_SB + 3 * BRANCH_W
PROJ_COLS = COL_FORGET + BRANCH_W
BLK = BRANCH_W

VMEM_LIMIT = 56 * 1024 * 1024
RWKV_CHUNK = 64


def _cparams(sem):
    return pltpu.CompilerParams(dimension_semantics=sem, vmem_limit_bytes=VMEM_LIMIT)


def _dot(a, b, precision=None):
    return jnp.dot(a, b, preferred_element_type=F32, precision=precision)


def _dot_nt(a, b, precision=None):
    return lax.dot_general(a, b, (((1,), (1,)), ((), ())), preferred_element_type=F32, precision=precision)


def _dot_tn(a, b, precision=None):
    return lax.dot_general(a, b, (((0,), (0,)), ((), ())), preferred_element_type=F32, precision=precision)


def _rms(x, g):
    return x * lax.rsqrt(jnp.mean(x * x, axis=-1, keepdims=True) + RMS_EPS) * g


def _softplus_neg_abs(z):
    return jnp.log1p(jnp.exp(-jnp.abs(z)))


def _head_slices():
    return [slice(h * HEAD_DIM, (h + 1) * HEAD_DIM) for h in range(HEADS)]


def _ffn_kernel(x_ref, g_ref, wa_ref, wb_ref, wo_ref, gf_ref, o_ref, xn_ref, acc_ref, *, final_norm):
    j = pl.program_id(1)

    @pl.when(j == 0)
    def _():
        xn_ref[...] = _rms(x_ref[...], g_ref[...]).astype(BF16)
        acc_ref[...] = jnp.zeros_like(acc_ref)

    xn = xn_ref[...]
    a = _dot(xn, wa_ref[...])
    b = _dot(xn, wb_ref[...])
    hmid = (a * jax.nn.sigmoid(a)) * b
    acc_ref[...] += _dot(hmid.astype(BF16), wo_ref[...])

    @pl.when(j == pl.num_programs(1) - 1)
    def _():
        y = x_ref[...] + 0.5 * acc_ref[...]
        if final_norm:
            o_ref[...] = _rms(y, gf_ref[...])
        else:
            o_ref[...] = y


def _ffn(x, g, wi, wo, gf, final_norm, tm, tf=256):
    m, d = x.shape
    f = wo.shape[0]
    nf = f // tf
    kern = functools.partial(_ffn_kernel, final_norm=final_norm)
    outs = pl.pallas_call(
        kern,
        grid=(m // tm, nf),
        in_specs=[
            pl.BlockSpec((tm, d), lambda i, j: (i, 0)),
            pl.BlockSpec((1, d), lambda i, j: (0, 0)),
            pl.BlockSpec((d, tf), lambda i, j: (0, j)),
            pl.BlockSpec((d, tf), lambda i, j: (0, j + nf)),
            pl.BlockSpec((tf, d), lambda i, j: (j, 0)),
            pl.BlockSpec((1, d), lambda i, j: (0, 0)),
        ],
        out_specs=pl.BlockSpec((tm, d), lambda i, j: (i, 0)),
        out_shape=jax.ShapeDtypeStruct((m, d), F32),
        scratch_shapes=[pltpu.VMEM((tm, d), BF16), pltpu.VMEM((tm, d), F32)],
        compiler_params=_cparams(("parallel", "arbitrary")),
        name="ffn",
    )(x, g.reshape(1, d), wi, wi, wo, gf.reshape(1, d))
    return outs


def _proj_kernel(x_ref, g_ref, w_ref, o_ref, xn_ref):
    @pl.when(pl.program_id(1) == 0)
    def _():
        xn_ref[...] = _rms(x_ref[...], g_ref[...]).astype(BF16)

    o_ref[...] = _dot(xn_ref[...], w_ref[...])


def _proj(x, g, w, tm, tn=256):
    m, d = x.shape
    n = w.shape[1]
    return pl.pallas_call(
        _proj_kernel,
        grid=(m // tm, n // tn),
        in_specs=[
            pl.BlockSpec((tm, d), lambda i, j: (i, 0)),
            pl.BlockSpec((1, d), lambda i, j: (0, 0)),
            pl.BlockSpec((d, tn), lambda i, j: (0, j)),
        ],
        out_specs=pl.BlockSpec((tm, tn), lambda i, j: (i, j)),
        out_shape=jax.ShapeDtypeStruct((m, n), F32),
        scratch_shapes=[pltpu.VMEM((tm, d), BF16)],
        compiler_params=_cparams(("parallel", "arbitrary")),
        name="proj",
    )(x, g.reshape(1, d), w)


def _merge_kernel(x_ref, oa_ref, ob_ref, oc_ref, od_ref, g0_ref, g1_ref, g2_ref, g3_ref, wb_ref, wo_ref, o_ref):
    merged = None
    for i, (o_r, g_r) in enumerate(((oa_ref, g0_ref), (ob_ref, g1_ref), (oc_ref, g2_ref), (od_ref, g3_ref))):
        t = jax.nn.sigmoid(g_r[...]) * _dot(o_r[...].astype(BF16), wb_ref[i])
        merged = t if merged is None else merged + t
    o_ref[...] = x_ref[...] + _dot(merged.astype(BF16), wo_ref[...])


def _merge(x, branches, h, wb, wo, tm):
    m, d = x.shape
    gate_specs = [pl.BlockSpec((tm, d), functools.partial(lambda i, k: (i, k), k=COL_GATE // d + k)) for k in range(N_BRANCH)]
    return pl.pallas_call(
        _merge_kernel,
        grid=(m // tm,),
        in_specs=[pl.BlockSpec((tm, d), lambda i: (i, 0))]
        + [pl.BlockSpec((tm, BRANCH_W), lambda i: (i, 0))] * N_BRANCH
        + gate_specs
        + [pl.BlockSpec((N_BRANCH, BRANCH_W, d), lambda i: (0, 0, 0)), pl.BlockSpec((d, d), lambda i: (0, 0))],
        out_specs=pl.BlockSpec((tm, d), lambda i: (i, 0)),
        out_shape=jax.ShapeDtypeStruct((m, d), F32),
        compiler_params=_cparams(("parallel",)),
        name="merge",
    )(x, *branches, h, h, h, h, wb, wo)


def _forget_kernel(ff_ref, bias_ref, lf_ref, cum_ref, *, t):
    x = ff_ref[...] + bias_ref[...]
    lf = jnp.minimum(x, 0.0) - _softplus_neg_abs(x)
    lf_ref[...] = lf
    w = 128
    tri = (lax.broadcasted_iota(jnp.int32, (w, w), 0) <= lax.broadcasted_iota(jnp.int32, (w, w), 1)).astype(F32)
    carry = jnp.zeros((x.shape[0], 1), F32)
    for c in range(t // w):
        cs = _dot(lf[:, c * w:(c + 1) * w], tri, HIGHEST) + carry
        cum_ref[:, c * w:(c + 1) * w] = cs
        carry = cs[:, w - 1:w]


def _forget(ff_t, bias_rows):
    r, t = ff_t.shape
    return pl.pallas_call(
        functools.partial(_forget_kernel, t=t),
        out_shape=(jax.ShapeDtypeStruct((r, t), F32), jax.ShapeDtypeStruct((r, t), F32)),
        compiler_params=pltpu.CompilerParams(vmem_limit_bytes=VMEM_LIMIT),
        name="forget",
    )(ff_t, bias_rows)


def _fox_kernel(q_ref, k_ref, v_ref, cq_ref, ck_ref, o_ref, m_scr, l_scr, acc_scr, *, tq):
    qi = pl.program_id(1)
    ki = pl.program_id(2)

    @pl.when(ki == 0)
    def _():
        m_scr[...] = jnp.full_like(m_scr, NEG_BIG)
        l_scr[...] = jnp.zeros_like(l_scr)
        acc_scr[...] = jnp.zeros_like(acc_scr)

    @pl.when(ki <= qi)
    def _():
        q = q_ref[0].astype(BF16)
        k = k_ref[0].astype(BF16)
        v = v_ref[0].astype(BF16)
        cq = cq_ref[0]
        ck = ck_ref[0]
        row = qi * tq + lax.broadcasted_iota(jnp.int32, (tq, tq), 0)
        col = ki * tq + lax.broadcasted_iota(jnp.int32, (tq, tq), 1)
        mask = col <= row
        for h, sl in enumerate(_head_slices()):
            s = _dot_nt(q[:, sl], k[:, sl]) * QK_SCALE + cq[:, h:h + 1] - ck[h:h + 1, :]
            s = jnp.where(mask, s, NEG_BIG)
            m_old = m_scr[h]
            m_new = jnp.maximum(m_old, jnp.max(s, axis=-1, keepdims=True))
            p = jnp.exp(s - m_new)
            alpha = jnp.exp(m_old - m_new)
            l_scr[h] = alpha * l_scr[h] + jnp.sum(p, axis=-1, keepdims=True)
            acc_scr[h] = alpha * acc_scr[h] + _dot(p.astype(BF16), v[:, sl])
            m_scr[h] = m_new

    @pl.when(ki == pl.num_programs(2) - 1)
    def _():
        for h, sl in enumerate(_head_slices()):
            o_ref[0, :, sl] = acc_scr[h] / l_scr[h]


def _fox_prompt(h3, cum_q, cum_k, tq=256):
    b, t, _ = h3.shape
    nq = t // tq
    cb = COL_FOX // BLK
    return pl.pallas_call(
        functools.partial(_fox_kernel, tq=tq),
        grid=(b, nq, nq),
        in_specs=[
            pl.BlockSpec((1, tq, BLK), lambda bi, qi, ki: (bi, qi, cb)),
            pl.BlockSpec((1, tq, BLK), lambda bi, qi, ki: (bi, jnp.minimum(ki, qi), cb + 1)),
            pl.BlockSpec((1, tq, BLK), lambda bi, qi, ki: (bi, jnp.minimum(ki, qi), cb + 2)),
            pl.BlockSpec((1, tq, HEADS), lambda bi, qi, ki: (bi, qi, 0)),
            pl.BlockSpec((1, HEADS, tq), lambda bi, qi, ki: (bi, 0, jnp.minimum(ki, qi))),
        ],
        out_specs=pl.BlockSpec((1, tq, BLK), lambda bi, qi, ki: (bi, qi, 0)),
        out_shape=jax.ShapeDtypeStruct((b, t, BLK), F32),
        scratch_shapes=[pltpu.VMEM((HEADS, tq, 1), F32), pltpu.VMEM((HEADS, tq, 1), F32),
                        pltpu.VMEM((HEADS, tq, HEAD_DIM), F32)],
        compiler_params=_cparams(("parallel", "parallel", "arbitrary")),
        name="fox_prompt",
    )(h3, h3, h3, cum_q, cum_k)


def _strict_upper(n):
    return lax.broadcasted_iota(jnp.int32, (n, n), 0) > lax.broadcasted_iota(jnp.int32, (n, n), 1)


def _suffix_excl(x, u_bf16):
    hi = x.astype(BF16)
    lo = (x - hi.astype(F32)).astype(BF16)
    return _dot(hi, u_bf16) + _dot(lo, u_bf16)


def _sb_kernel(q_ref, k_ref, v_ref, o_ref, r_scr, acc_scr, *, tq):
    qi = pl.program_id(1)
    j = pl.program_id(2)

    @pl.when(j == 0)
    def _():
        r_scr[...] = jnp.zeros_like(r_scr)
        acc_scr[...] = jnp.zeros_like(acc_scr)

    @pl.when(j <= qi)
    def _():
        kb = qi - j
        q = q_ref[0].astype(BF16)
        k = k_ref[0].astype(BF16)
        v = v_ref[0].astype(BF16)
        row = qi * tq + lax.broadcasted_iota(jnp.int32, (tq, tq), 0)
        col = kb * tq + lax.broadcasted_iota(jnp.int32, (tq, tq), 1)
        mask = col < row
        u = _strict_upper(tq).astype(BF16)
        for h, sl in enumerate(_head_slices()):
            z = _dot_nt(q[:, sl], k[:, sl]) * QK_SCALE
            sp = _softplus_neg_abs(z)
            ls = jnp.minimum(z, 0.0) - sp
            l1m = jnp.where(mask, -jnp.maximum(z, 0.0) - sp, 0.0)
            excl = _suffix_excl(l1m, u) + r_scr[h]
            a = jnp.where(mask, jnp.exp(ls + excl), 0.0)
            acc_scr[h] += _dot(a.astype(BF16), v[:, sl])
            r_scr[h] += jnp.sum(l1m, axis=-1, keepdims=True)

    @pl.when(j == pl.num_programs(2) - 1)
    def _():
        for h, sl in enumerate(_head_slices()):
            o_ref[0, :, sl] = acc_scr[h]


def _sb_prompt(h3, tq=256):
    b, t, _ = h3.shape
    nq = t // tq
    cb = COL_SB // BLK
    return pl.pallas_call(
        functools.partial(_sb_kernel, tq=tq),
        grid=(b, nq, nq),
        in_specs=[
            pl.BlockSpec((1, tq, BLK), lambda bi, qi, j: (bi, qi, cb)),
            pl.BlockSpec((1, tq, BLK), lambda bi, qi, j: (bi, jnp.maximum(qi - j, 0), cb + 1)),
            pl.BlockSpec((1, tq, BLK), lambda bi, qi, j: (bi, jnp.maximum(qi - j, 0), cb + 2)),
        ],
        out_specs=pl.BlockSpec((1, tq, BLK), lambda bi, qi, j: (bi, qi, 0)),
        out_shape=jax.ShapeDtypeStruct((b, t, BLK), F32),
        scratch_shapes=[pltpu.VMEM((HEADS, tq, 1), F32), pltpu.VMEM((HEADS, tq, HEAD_DIM), F32)],
        compiler_params=_cparams(("parallel", "parallel", "arbitrary")),
        name="sb_prompt",
    )(h3, h3, h3)


def _head_block_mask():
    row = lax.broadcasted_iota(jnp.int32, (8, BLK), 0)
    lane_head = lax.broadcasted_iota(jnp.int32, (8, BLK), 1) // HEAD_DIM
    return (row == lane_head).astype(F32)


def _fox_sample_kernel(pt_ref, qb_ref, kn_ref, vn_ref, lfn_ref, k_ref, v_ref, lf_ref, o_ref, m_scr, l_scr, r_scr, acc_scr):
    p = pl.program_id(1)
    qb = qb_ref[0]

    @pl.when(p == 0)
    def _():
        m_scr[...] = jnp.sum(qb * kn_ref[0], axis=-1, keepdims=True) * QK_SCALE
        l_scr[...] = jnp.ones_like(l_scr)
        r_scr[...] = lfn_ref[0]
        acc_scr[...] = jnp.broadcast_to(vn_ref[0], acc_scr.shape)

    lf = lf_ref[0]
    u = _strict_upper(PAGE_SIZE).astype(F32)
    s = (_dot_nt(qb.astype(BF16), k_ref[0].astype(BF16)) * QK_SCALE
         + _dot(lf, u, HIGHEST) + r_scr[...])
    m_old = m_scr[...]
    m_new = jnp.maximum(m_old, jnp.max(s, axis=-1, keepdims=True))
    pw = jnp.exp(s - m_new)
    alpha = jnp.exp(m_old - m_new)
    l_scr[...] = alpha * l_scr[...] + jnp.sum(pw, axis=-1, keepdims=True)
    acc_scr[...] = alpha * acc_scr[...] + _dot(pw.astype(BF16), v_ref[0].astype(BF16))
    m_scr[...] = m_new
    r_scr[...] += jnp.sum(lf, axis=-1, keepdims=True)

    @pl.when(p == pl.num_programs(1) - 1)
    def _():
        o_ref[0] = jnp.sum(acc_scr[...] / l_scr[...] * _head_block_mask(), axis=0, keepdims=True)


def _sb_sample_kernel(pt_ref, qb_ref, k_ref, v_ref, o_ref, r_scr, acc_scr):
    p = pl.program_id(1)

    @pl.when(p == 0)
    def _():
        r_scr[...] = jnp.zeros_like(r_scr)
        acc_scr[...] = jnp.zeros_like(acc_scr)

    z = _dot_nt(qb_ref[0].astype(BF16), k_ref[0].astype(BF16)) * QK_SCALE
    sp = _softplus_neg_abs(z)
    ls = jnp.minimum(z, 0.0) - sp
    l1m = -jnp.maximum(z, 0.0) - sp
    excl = _suffix_excl(l1m, _strict_upper(PAGE_SIZE).astype(BF16)) + r_scr[...]
    a = jnp.exp(ls + excl)
    acc_scr[...] += _dot(a.astype(BF16), v_ref[0].astype(BF16))
    r_scr[...] += jnp.sum(l1m, axis=-1, keepdims=True)

    @pl.when(p == pl.num_programs(1) - 1)
    def _():
        o_ref[0] = jnp.sum(acc_scr[...] * _head_block_mask(), axis=0, keepdims=True)


def _query_rows(q):
    b = q.shape[0]
    lane_head = np.arange(BLK) // HEAD_DIM
    mask = jnp.asarray((np.arange(8)[:, None] == lane_head[None, :]).astype(np.float32))
    return q[:, None, :] * mask[None]


def _fox_sample(q, k_new, v_new, lf_new, cache_k, cache_v, cache_lf_t, page_table, base):
    b, n_pages = page_table.shape

    def page(bi, p, pt):
        return (base + pt[bi, n_pages - 1 - p], 0, 0)

    def per_b(bi, p, pt):
        return (bi, 0, 0)

    lfn = jnp.concatenate([lf_new, jnp.zeros_like(lf_new)], axis=1)[:, :, None]
    grid_spec = pltpu.PrefetchScalarGridSpec(
        num_scalar_prefetch=1,
        grid=(b, n_pages),
        in_specs=[
            pl.BlockSpec((1, 8, BLK), per_b),
            pl.BlockSpec((1, 1, BLK), per_b),
            pl.BlockSpec((1, 1, BLK), per_b),
            pl.BlockSpec((1, 8, 1), per_b),
            pl.BlockSpec((1, PAGE_SIZE, BLK), page),
            pl.BlockSpec((1, PAGE_SIZE, BLK), page),
            pl.BlockSpec((1, 8, PAGE_SIZE), page),
        ],
        out_specs=pl.BlockSpec((1, 1, BLK), per_b),
        scratch_shapes=[pltpu.VMEM((8, 1), F32), pltpu.VMEM((8, 1), F32), pltpu.VMEM((8, 1), F32),
                        pltpu.VMEM((8, BLK), F32)],
    )
    out = pl.pallas_call(
        _fox_sample_kernel,
        grid_spec=grid_spec,
        out_shape=jax.ShapeDtypeStruct((b, 1, BLK), F32),
        compiler_params=_cparams(("parallel", "arbitrary")),
        name="fox_sample",
    )(page_table, _query_rows(q), k_new[:, None, :], v_new[:, None, :], lfn, cache_k, cache_v, cache_lf_t)
    return out[:, 0]


def _sb_sample(q, cache_k, cache_v, page_table, base):
    b, n_pages = page_table.shape

    def page(bi, p, pt):
        return (base + pt[bi, n_pages - 1 - p], 0, 0)

    def per_b(bi, p, pt):
        return (bi, 0, 0)

    grid_spec = pltpu.PrefetchScalarGridSpec(
        num_scalar_prefetch=1,
        grid=(b, n_pages),
        in_specs=[
            pl.BlockSpec((1, 8, BLK), per_b),
            pl.BlockSpec((1, PAGE_SIZE, BLK), page),
            pl.BlockSpec((1, PAGE_SIZE, BLK), page),
        ],
        out_specs=pl.BlockSpec((1, 1, BLK), per_b),
        scratch_shapes=[pltpu.VMEM((8, 1), F32), pltpu.VMEM((8, BLK), F32)],
    )
    out = pl.pallas_call(
        _sb_sample_kernel,
        grid_spec=grid_spec,
        out_shape=jax.ShapeDtypeStruct((b, 1, BLK), F32),
        compiler_params=_cparams(("parallel", "arbitrary")),
        name="sb_sample",
    )(page_table, _query_rows(q), cache_k, cache_v)
    return out[:, 0]


def _rot_half(x):
    half = HEAD_DIM // 2
    lane = lax.broadcasted_iota(jnp.int32, x.shape, 1) % HEAD_DIM
    n = x.shape[1]
    return jnp.where(lane < half, -pltpu.roll(x, n - half, 1), pltpu.roll(x, half, 1))


def _ret_kernel(q_ref, k_ref, v_ref, g_ref, cos_ref, sin_ref, dmask_ref, xi_ref, zeta_ref, cd_ref, gnw_ref, s0_ref,
                o_ref, st_ref, s_scr):
    c = pl.program_id(1)

    @pl.when(c == 0)
    def _():
        s_scr[...] = s0_ref[0]

    cos = cos_ref[...]
    sin = sin_ref[...]
    q = q_ref[0]
    k = k_ref[0]
    qr = (q * cos + _rot_half(q) * sin).astype(BF16)
    kr = (k * cos + _rot_half(k) * sin) * QK_SCALE
    v = v_ref[0].astype(BF16)
    g = g_ref[0]
    gate = g * jax.nn.sigmoid(g)
    gnw = gnw_ref[...]
    for h, sl in enumerate(_head_slices()):
        qh = qr[:, sl]
        kh = kr[:, sl]
        vh = v[:, sl]
        s = s_scr[h]
        att = _dot_nt(qh, kh.astype(BF16)) * dmask_ref[h]
        o = _dot(att.astype(BF16), vh) + _dot(qh, s.astype(BF16)) * xi_ref[h]
        s_scr[h] = cd_ref[h] * s + _dot_tn((kh * zeta_ref[h]).astype(BF16), vh)
        mu = jnp.mean(o, axis=-1, keepdims=True)
        d = o - mu
        var = jnp.mean(d * d, axis=-1, keepdims=True)
        o_ref[0, :, sl] = gate[:, sl] * (d * lax.rsqrt(var + GN_EPS) * gnw[:, sl])

    @pl.when(c == pl.num_programs(1) - 1)
    def _():
        st_ref[0] = s_scr[...]


def _ret_tables(chunk, n_valid):
    lg = np.log(1.0 - 2.0 ** (-5.0 - np.arange(HEADS, dtype=np.float64)))
    n = np.arange(chunk, dtype=np.float64)
    diff = n[:, None] - n[None, :]
    valid = (n < n_valid).astype(np.float64)
    dmask = np.where(diff[None] >= 0, np.exp(np.maximum(diff, 0.0)[None] * lg[:, None, None]), 0.0)
    dmask = dmask * valid[None, :, None] * valid[None, None, :]
    xi = np.exp((n[None, :] + 1.0) * lg[:, None]) * valid[None, :]
    zeta = np.exp((n_valid - 1.0 - n)[None, :] * lg[:, None]) * valid[None, :]
    cd = np.exp(n_valid * lg)
    f = lambda a: jnp.asarray(a.astype(np.float32))
    return f(dmask), f(xi[:, :, None]), f(zeta[:, :, None]), f(cd[:, None, None])


def _rope_tables(pos):
    half = HEAD_DIM // 2
    inv = ROPE_BASE ** (-jnp.arange(half, dtype=F32) / half)
    ang = pos.astype(F32)[:, None] * inv[None, :]
    cos = jnp.tile(jnp.cos(ang), (1, 2 * HEADS))
    sin = jnp.tile(jnp.sin(ang), (1, 2 * HEADS))
    return cos, sin


def _retention(h3, col0, cos, sin, tables, gnw, s0, chunk):
    b, t, _ = h3.shape
    nc = t // chunk
    cb = col0 // BLK
    dmask, xi, zeta, cd = tables

    def tok(k):
        return pl.BlockSpec((1, chunk, BLK), functools.partial(lambda bi, c, k: (bi, c, k), k=cb + k))

    const3 = lambda bi, c: (0, 0, 0)
    return pl.pallas_call(
        _ret_kernel,
        grid=(b, nc),
        in_specs=[tok(0), tok(1), tok(2), tok(3),
                  pl.BlockSpec((chunk, BLK), lambda bi, c: (c, 0)),
                  pl.BlockSpec((chunk, BLK), lambda bi, c: (c, 0)),
                  pl.BlockSpec((HEADS, chunk, chunk), const3),
                  pl.BlockSpec((HEADS, chunk, 1), const3),
                  pl.BlockSpec((HEADS, chunk, 1), const3),
                  pl.BlockSpec((HEADS, 1, 1), const3),
                  pl.BlockSpec((1, BLK), lambda bi, c: (0, 0)),
                  pl.BlockSpec((1, HEADS, HEAD_DIM, HEAD_DIM), lambda bi, c: (bi, 0, 0, 0))],
        out_specs=[pl.BlockSpec((1, chunk, BLK), lambda bi, c: (bi, c, 0)),
                   pl.BlockSpec((1, HEADS, HEAD_DIM, HEAD_DIM), lambda bi, c: (bi, 0, 0, 0))],
        out_shape=[jax.ShapeDtypeStruct((b, t, BLK), F32),
                   jax.ShapeDtypeStruct((b, HEADS, HEAD_DIM, HEAD_DIM), F32)],
        scratch_shapes=[pltpu.VMEM((HEADS, HEAD_DIM, HEAD_DIM), F32)],
        compiler_params=_cparams(("parallel", "arbitrary")),
        name="retention",
    )(h3, h3, h3, h3, cos, sin, dmask, xi, zeta, cd, gnw.reshape(1, BLK), s0)


def _head_sum_matrix():
    i = lax.broadcasted_iota(jnp.int32, (BLK, BLK), 0) // HEAD_DIM
    j = lax.broadcasted_iota(jnp.int32, (BLK, BLK), 1) // HEAD_DIM
    return (i == j).astype(F32)


def _rwkv_prep_kernel(hr_ref, prev_ref, valid_ref, mu_ref, w0_ref, a0_ref, kk_w_ref, ka_ref, rk_ref,
                      wup_ref, aup_ref, gup_ref,
                      r_ref, k_ref, v_ref, logd_ref, kk_ref, b_ref, g_ref, bonus_ref):
    hr = hr_ref[...]
    xm = hr + (prev_ref[...] - hr) * mu_ref[...]
    valid = valid_ref[...]
    r = xm[:, 0:BLK]
    k = xm[:, BLK:2 * BLK]
    v = xm[:, 2 * BLK:3 * BLK]
    o = 3 * BLK
    wd = xm[:, o:o + DECAY_LORA]
    ad = xm[:, o + DECAY_LORA:o + DECAY_LORA + AAA_LORA]
    gd = xm[:, o + DECAY_LORA + AAA_LORA:]
    wpre = w0_ref[...] + _dot(jnp.tanh(wd).astype(BF16), wup_ref[...])
    w = jnp.minimum(wpre, 0.0) - _softplus_neg_abs(wpre) - 0.5
    a = jax.nn.sigmoid(a0_ref[...] + _dot(ad.astype(BF16), aup_ref[...]))
    g = _dot(jax.nn.sigmoid(gd).astype(BF16), gup_ref[...])
    hs = _head_sum_matrix()
    kk = k * kk_w_ref[...]
    kk = kk * lax.rsqrt(_dot(kk * kk, hs, HIGHEST) + 1e-12)
    kmod = k * (1.0 + (a - 1.0) * ka_ref[...])
    bonus = _dot(r * kmod * rk_ref[...], hs, HIGHEST) * v
    r_ref[...] = r * valid
    k_ref[...] = kmod * valid
    v_ref[...] = v * valid
    logd_ref[...] = -jnp.exp(w) * valid
    kk_ref[...] = kk * valid
    b_ref[...] = kk * a * valid
    g_ref[...] = g
    bonus_ref[...] = bonus


def _rwkv_prep(hr, prev, valid, mu, w0, a0, k_k, k_a, r_k, w_up, a_up, g_up, tm):
    m = hr.shape[0]
    row = lambda n: pl.BlockSpec((1, n), lambda i: (0, 0))
    full = lambda a: pl.BlockSpec(a.shape, lambda i: (0, 0))
    tokb = pl.BlockSpec((tm, BLK), lambda i: (i, 0))
    return pl.pallas_call(
        _rwkv_prep_kernel,
        grid=(m // tm,),
        in_specs=[pl.BlockSpec((tm, RWKV_IN), lambda i: (i, 0)), pl.BlockSpec((tm, RWKV_IN), lambda i: (i, 0)),
                  pl.BlockSpec((tm, 1), lambda i: (i, 0)),
                  row(RWKV_IN), row(BLK), row(BLK), row(BLK), row(BLK), row(BLK),
                  full(w_up), full(a_up), full(g_up)],
        out_specs=[tokb] * 8,
        out_shape=[jax.ShapeDtypeStruct((m, BLK), F32)] * 8,
        compiler_params=_cparams(("parallel",)),
        name="rwkv_prep",
    )(hr, prev, valid, mu.reshape(1, -1), w0.reshape(1, -1), a0.reshape(1, -1), k_k.reshape(1, -1),
      k_a.reshape(1, -1), r_k.reshape(1, -1), w_up, a_up, g_up)


def _rwkv_chunk_kernel(r_ref, k_ref, v_ref, logd_ref, kk_ref, b_ref, r2_ref, y0_ref, g_ref, h_ref, *, chunk):
    c = chunk
    ri = lax.broadcasted_iota(jnp.int32, (c, c), 0)
    ci = lax.broadcasted_iota(jnp.int32, (c, c), 1)
    incl = (ci <= ri).astype(F32)
    strict = ci < ri
    eye = (ri == ci).astype(F32)
    eye_d = (lax.broadcasted_iota(jnp.int32, (HEAD_DIM, HEAD_DIM), 0)
             == lax.broadcasted_iota(jnp.int32, (HEAD_DIM, HEAD_DIM), 1)).astype(F32)

    logd = logd_ref[0]
    logw = _dot(incl, logd, HIGHEST)
    logw_c = logw[c - 1:c, :]
    e_neg = jnp.exp(-logw)
    e_end = jnp.exp(logw_c - logw)
    kk_t = kk_ref[0] * jnp.exp(logw - logd)
    r_t = r_ref[0] * jnp.exp(logw)
    b = b_ref[0]
    k = k_ref[0]
    b_t = b * e_neg
    k_t = k * e_neg
    b_e = b * e_end
    k_e = k * e_end
    w_c = jnp.exp(logw_c)
    v = v_ref[0]

    for h, sl in enumerate(_head_slices()):
        kkh, rh, bh, kh, vh = kk_t[:, sl], r_t[:, sl], b_t[:, sl], k_t[:, sl], v[:, sl]
        a_b = jnp.where(strict, _dot_nt(kkh, bh, HIGHEST), 0.0)
        a_k = jnp.where(strict, _dot_nt(kkh, kh, HIGHEST), 0.0)
        m_b = _dot_nt(rh, bh, HIGHEST) * incl
        m_k = _dot_nt(rh, kh, HIGHEST) * incl
        npow = -a_b
        tinv = eye + npow
        for _ in range(int(math.log2(c)) - 1):
            npow = _dot(npow, npow, HIGHEST)
            tinv = tinv + _dot(tinv, npow, HIGHEST)
        p = _dot(tinv, kkh, HIGHEST)
        qv = _dot(tinv, _dot(a_k, vh, HIGHEST), HIGHEST)
        r2_ref[0, :, sl] = rh - _dot(m_b, p, HIGHEST)
        y0_ref[0, :, sl] = _dot(m_k, vh, HIGHEST) - _dot(m_b, qv, HIGHEST)
        g_ref[0, 0, h] = eye_d * w_c[:, sl] - _dot_tn(p, b_e[:, sl], HIGHEST)
        h_ref[0, 0, h] = _dot_tn(vh, k_e[:, sl], HIGHEST) - _dot_tn(qv, b_e[:, sl], HIGHEST)


def _rwkv_chunks(r, k, v, logd, kk, b, chunk):
    bsz, t, _ = r.shape
    nc = t // chunk
    tok = pl.BlockSpec((1, chunk, BLK), lambda bi, c: (bi, c, 0))
    mat = pl.BlockSpec((1, 1, HEADS, HEAD_DIM, HEAD_DIM), lambda bi, c: (bi, c, 0, 0, 0))
    mat_shape = jax.ShapeDtypeStruct((bsz, nc, HEADS, HEAD_DIM, HEAD_DIM), F32)
    return pl.pallas_call(
        functools.partial(_rwkv_chunk_kernel, chunk=chunk),
        grid=(bsz, nc),
        in_specs=[tok] * 6,
        out_specs=[tok, tok, mat, mat],
        out_shape=[jax.ShapeDtypeStruct((bsz, t, BLK), F32)] * 2 + [mat_shape] * 2,
        compiler_params=_cparams(("parallel", "parallel")),
        name="rwkv_chunks",
    )(r, k, v, logd, kk, b)


def _rwkv_scan_kernel(r2_ref, y0_ref, g_ref, h_ref, gate_ref, bonus_ref, lnw_ref, lnb_ref, s0_ref,
                      o_ref, st_ref, s_scr):
    c = pl.program_id(1)

    @pl.when(c == 0)
    def _():
        s_scr[...] = s0_ref[0]

    r2 = r2_ref[0]
    y0 = y0_ref[0]
    gate = gate_ref[0]
    bonus = bonus_ref[0]
    lnw = lnw_ref[...]
    lnb = lnb_ref[...]
    for h, sl in enumerate(_head_slices()):
        s = s_scr[h]
        y = _dot_nt(r2[:, sl], s, HIGHEST) + y0[:, sl]
        s_scr[h] = _dot(s, g_ref[0, 0, h], HIGHEST) + h_ref[0, 0, h]
        mu = jnp.mean(y, axis=-1, keepdims=True)
        d = y - mu
        var = jnp.mean(d * d, axis=-1, keepdims=True)
        yn = d * lax.rsqrt(var + RWKV_GN_EPS) * lnw[:, sl] + lnb[:, sl]
        o_ref[0, :, sl] = (yn + bonus[:, sl]) * gate[:, sl]

    @pl.when(c == pl.num_programs(1) - 1)
    def _():
        st_ref[0] = s_scr[...]


def _rwkv_scan(r2, y0, g, hmat, gate, bonus, ln_w, ln_b, s0, chunk):
    bsz, t, _ = r2.shape
    nc = t // chunk
    tok = pl.BlockSpec((1, chunk, BLK), lambda bi, c: (bi, c, 0))
    mat = pl.BlockSpec((1, 1, HEADS, HEAD_DIM, HEAD_DIM), lambda bi, c: (bi, c, 0, 0, 0))
    state = pl.BlockSpec((1, HEADS, HEAD_DIM, HEAD_DIM), lambda bi, c: (bi, 0, 0, 0))
    row = pl.BlockSpec((1, BLK), lambda bi, c: (0, 0))
    return pl.pallas_call(
        _rwkv_scan_kernel,
        grid=(bsz, nc),
        in_specs=[tok, tok, mat, mat, tok, tok, row, row, state],
        out_specs=[tok, state],
        out_shape=[jax.ShapeDtypeStruct((bsz, t, BLK), F32),
                   jax.ShapeDtypeStruct((bsz, HEADS, HEAD_DIM, HEAD_DIM), F32)],
        scratch_shapes=[pltpu.VMEM((HEADS, HEAD_DIM, HEAD_DIM), F32)],
        compiler_params=_cparams(("parallel", "arbitrary")),
        name="rwkv_scan",
    )(r2, y0, g, hmat, gate, bonus, ln_w.reshape(1, BLK), ln_b.reshape(1, BLK), s0)


def _rwkv(hr3, shift0, s0, n_valid, p, chunk, tm):
    bsz, t, _ = hr3.shape
    prev = jnp.concatenate([shift0[:, None, :], hr3[:, :-1]], axis=1)
    valid = jnp.broadcast_to((jnp.arange(t) < n_valid).astype(F32)[None, :, None], (bsz, t, 1))
    m = bsz * t
    outs = _rwkv_prep(hr3.reshape(m, RWKV_IN), prev.reshape(m, RWKV_IN), valid.reshape(m, 1),
                      p["mu"], p["w0"], p["a0"], p["k_k"], p["k_a"], p["r_k"], p["w_up"], p["a_up"], p["g_up"], tm)
    r, k, v, logd, kk, b, gate, bonus = [o.reshape(bsz, t, BLK) for o in outs]
    r2, y0, g, hmat = _rwkv_chunks(r, k, v, logd, kk, b, chunk)
    return _rwkv_scan(r2, y0, g, hmat, gate, bonus, p["ln_w"], p["ln_b"], s0, chunk)


def _reorder_w_in(w):
    fox_end = 3 * BRANCH_W
    ff_end = fox_end + HEADS
    gate_start = w.shape[-1] - N_BRANCH * D_MODEL
    pad = jnp.zeros(w.shape[:-1] + (BRANCH_W - HEADS,), w.dtype)
    out = jnp.concatenate([w[..., gate_start:], w[..., :fox_end], w[..., ff_end:gate_start],
                           w[..., fox_end:ff_end], pad], axis=-1)
    assert out.shape[-1] == PROJ_COLS
    return out.astype(BF16)


def _heads4(x):
    return x.reshape(x.shape[:-1] + (HEADS, HEAD_DIM))


def kernel(x_prompt, x_sample, cache_fox_k, cache_fox_v, cache_fox_logf, cache_sb_k, cache_sb_v, state_ret, state_rwkv, state_rwkv_shift, page_table, w_in, fox_f_bias, ret_gn_w, rwkv_mu, rwkv_w0, rwkv_w_up, rwkv_a0, rwkv_a_up, rwkv_g_up, rwkv_k_k, rwkv_k_a, rwkv_r_k, rwkv_ln_w, rwkv_ln_b, w_branch, w_out, norm_ffn1, ffn1_w_in, ffn1_w_out, norm_mix, norm_ffn2, ffn2_w_in, ffn2_w_out, norm_final):
    depth = w_in.shape[0]
    bp, t, d = x_prompt.shape
    bs = x_sample.shape[0]
    n_pool = cache_fox_k.shape[1]
    past_len = page_table.shape[1] * PAGE_SIZE

    w_in_r = _reorder_w_in(w_in)
    f1i, f1o, f2i, f2o = (a.astype(BF16) for a in (ffn1_w_in, ffn1_w_out, ffn2_w_in, ffn2_w_out))
    wbr = w_branch.astype(BF16)
    wo = w_out.astype(BF16)
    w_up, a_up, g_up = rwkv_w_up.astype(BF16), rwkv_a_up.astype(BF16), rwkv_g_up.astype(BF16)

    pool = lambda c: c.reshape(depth * n_pool, PAGE_SIZE, BLK)
    ck_fox, cv_fox, ck_sb, cv_sb = pool(cache_fox_k), pool(cache_fox_v), pool(cache_sb_k), pool(cache_sb_v)
    lf_t = jnp.swapaxes(cache_fox_logf, 2, 3).reshape(depth * n_pool, HEADS, PAGE_SIZE)
    lf_t = jnp.concatenate([lf_t, jnp.zeros_like(lf_t)], axis=1)

    cos_p, sin_p = _rope_tables(jnp.arange(t))
    s_pad = 8
    cos_s, sin_s = _rope_tables(past_len + jnp.arange(s_pad))
    ret_tab_p = _ret_tables(RET_CHUNK, RET_CHUNK)
    ret_tab_s = _ret_tables(s_pad, 1)
    zeros_state = jnp.zeros((bp, HEADS, HEAD_DIM, HEAD_DIM), F32)

    xp = x_prompt.reshape(bp * t, d)
    xs = x_sample.reshape(bs, d)
    new_p = [[] for _ in range(8)]
    new_s = [[] for _ in range(8)]
    tm_p = 1024

    for l in range(depth):
        last = l == depth - 1
        rw = dict(mu=rwkv_mu[l], w0=rwkv_w0[l], a0=rwkv_a0[l], k_k=rwkv_k_k[l], k_a=rwkv_k_a[l], r_k=rwkv_r_k[l],
                  w_up=w_up[l], a_up=a_up[l], g_up=g_up[l], ln_w=rwkv_ln_w[l], ln_b=rwkv_ln_b[l])

        xp = _ffn(xp, norm_ffn1[l], f1i[l], f1o[l], norm_final, False, tm_p)
        h = _proj(xp, norm_mix[l], w_in_r[l], tm_p)
        h3 = h.reshape(bp, t, PROJ_COLS)
        ff_t = jnp.swapaxes(h3[:, :, COL_FORGET:COL_FORGET + HEADS], 1, 2).reshape(bp * HEADS, t)
        lf_rows, cum_rows = _forget(ff_t, jnp.tile(fox_f_bias[l], bp)[:, None])
        cum_k = cum_rows.reshape(bp, HEADS, t)
        lf = jnp.swapaxes(lf_rows.reshape(bp, HEADS, t), 1, 2)
        o_a = _fox_prompt(h3, jnp.swapaxes(cum_k, 1, 2), cum_k)
        o_b, ret_p = _retention(h3, COL_RET, cos_p, sin_p, ret_tab_p, ret_gn_w[l], zeros_state, RET_CHUNK)
        hr = h3[:, :, COL_RWKV:COL_RWKV + RWKV_IN]
        o_c, rw_p = _rwkv(hr, jnp.zeros((bp, RWKV_IN), F32), zeros_state, t, rw, RWKV_CHUNK, tm_p)
        o_d = _sb_prompt(h3)
        flat = lambda o: o.reshape(bp * t, BLK)
        xp = _merge(xp, [flat(o_a), flat(o_b), flat(o_c), flat(o_d)], h, wbr[l], wo[l], 512)
        xp = _ffn(xp, norm_ffn2[l], f2i[l], f2o[l], norm_final, last, tm_p)
        for i, a in enumerate((_heads4(h3[:, :, COL_FOX + BLK:COL_FOX + 2 * BLK]),
                               _heads4(h3[:, :, COL_FOX + 2 * BLK:COL_FOX + 3 * BLK]), lf,
                               _heads4(h3[:, :, COL_SB + BLK:COL_SB + 2 * BLK]),
                               _heads4(h3[:, :, COL_SB + 2 * BLK:COL_SB + 3 * BLK]), ret_p, rw_p, hr[:, -1])):
            new_p[i].append(a)

        xs = _ffn(xs, norm_ffn1[l], f1i[l], f1o[l], norm_final, False, bs)
        hs = _proj(xs, norm_mix[l], w_in_r[l], bs)
        col = lambda c0, n=BLK: hs[:, c0:c0 + n]
        ff_s = jnp.swapaxes(col(COL_FORGET, HEADS), 0, 1)
        ff_s = jnp.concatenate([ff_s, jnp.zeros((8 - HEADS, bs), F32)], axis=0)
        bias_s = jnp.concatenate([fox_f_bias[l], jnp.zeros((8 - HEADS,), F32)])[:, None]
        lf_s_rows, _ = _forget(jnp.pad(ff_s, ((0, 0), (0, 128 - bs))), bias_s)
        lf_s = jnp.swapaxes(lf_s_rows[:HEADS, :bs], 0, 1)
        o_a = _fox_sample(col(COL_FOX), col(COL_FOX + BLK), col(COL_FOX + 2 * BLK), lf_s,
                          ck_fox, cv_fox, lf_t, page_table, l * n_pool)
        hs_pad = jnp.pad(hs[:, None, :], ((0, 0), (0, s_pad - 1), (0, 0)))
        o_b, ret_s = _retention(hs_pad, COL_RET, cos_s, sin_s, ret_tab_s, ret_gn_w[l], state_ret[l], s_pad)
        hr_s = hs_pad[:, :, COL_RWKV:COL_RWKV + RWKV_IN]
        o_c, rw_s = _rwkv(hr_s, state_rwkv_shift[l], state_rwkv[l], 1, rw, s_pad, bs * s_pad)
        o_d = _sb_sample(col(COL_SB), ck_sb, cv_sb, page_table, l * n_pool)
        xs = _merge(xs, [o_a, o_b[:, 0], o_c[:, 0], o_d], hs, wbr[l], wo[l], bs)
        xs = _ffn(xs, norm_ffn2[l], f2i[l], f2o[l], norm_final, last, bs)
        for i, a in enumerate((_heads4(col(COL_FOX + BLK))[:, None], _heads4(col(COL_FOX + 2 * BLK))[:, None],
                               lf_s[:, None], _heads4(col(COL_SB + BLK))[:, None],
                               _heads4(col(COL_SB + 2 * BLK))[:, None], ret_s, rw_s, col(COL_RWKV, RWKV_IN))):
            new_s[i].append(a)

    sp = [jnp.stack(v) for v in new_p]
    ss = [jnp.stack(v) for v in new_s]
    return (xp.reshape(bp, t, d), xs.reshape(bs, 1, d), sp[0], sp[1], sp[2], sp[3], sp[4], sp[5], sp[6], sp[7],
            ss[0], ss[1], ss[2], ss[3], ss[4], ss[5], ss[6], ss[7])
```

```python
import functools
import math

import jax
import jax.numpy as jnp
import numpy as np
from jax import lax
from jax.experimental import pallas as pl
from jax.experimental.pallas import tpu as pltpu

F32 = jnp.float32
BF16 = jnp.bfloat16
HIGHEST = lax.Precision.HIGHEST

D_MODEL = 1024
HEADS = 4
HEAD_DIM = 64
BRANCH_W = HEADS * HEAD_DIM
N_BRANCH = 4
DECAY_LORA = 64
AAA_LORA = 64
GATE_LORA = 128
RWKV_IN = 3 * BRANCH_W + DECAY_LORA + AAA_LORA + GATE_LORA
PAGE_SIZE = 128
RET_CHUNK = 128
ROPE_BASE = 10000.0
RMS_EPS = 1e-6
GN_EPS = 1e-5
RWKV_GN_EPS = 64e-5
QK_SCALE = HEAD_DIM ** -0.5
NEG_BIG = -1e30

COL_GATE = 0
COL_FOX = N_BRANCH * D_MODEL
COL_RET = COL_FOX + 3 * BRANCH_W
COL_RWKV = COL_RET + 4 * BRANCH_W
COL_SB = COL_RWKV + RWKV_IN
COL_FORGET = COL_SB + 3 * BRANCH_W
PROJ_COLS = COL_FORGET + BRANCH_W
BLK = BRANCH_W

VMEM_LIMIT = 56 * 1024 * 1024
RWKV_CHUNK = 64
RWKV_CHUNKS_PER_STEP = 4


def _cparams(sem):
    return pltpu.CompilerParams(dimension_semantics=sem, vmem_limit_bytes=VMEM_LIMIT)


def _dot(a, b, precision=None):
    return jnp.dot(a, b, preferred_element_type=F32, precision=precision)


def _dot_nt(a, b, precision=None):
    return lax.dot_general(a, b, (((1,), (1,)), ((), ())), preferred_element_type=F32, precision=precision)


def _dot_tn(a, b, precision=None):
    return lax.dot_general(a, b, (((0,), (0,)), ((), ())), preferred_element_type=F32, precision=precision)


_NN = (((1,), (0,)), ((), ()))
_NT = (((1,), (1,)), ((), ()))
_TN = (((0,), (0,)), ((), ()))


def _mm(a, b, dims, passes):
    dg = lambda x, y: lax.dot_general(x, y, dims, preferred_element_type=F32)
    ah = a.astype(BF16)
    bh = b.astype(BF16)
    if passes == 1:
        return dg(ah, bh)
    al = (a - ah.astype(F32)).astype(BF16)
    bl = (b - bh.astype(F32)).astype(BF16)
    return dg(ah, bh) + (dg(ah, bl) + dg(al, bh))


PREC_PAIR = 1
PREC_INV = 1
PREC_APPLY = 1
PREC_SCAN = 1


def _rms(x, g):
    return x * lax.rsqrt(jnp.mean(x * x, axis=-1, keepdims=True) + RMS_EPS) * g


def _softplus_neg_abs(z):
    return jnp.log1p(jnp.exp(-jnp.abs(z)))


def _head_slices():
    return [slice(h * HEAD_DIM, (h + 1) * HEAD_DIM) for h in range(HEADS)]


def _ffn_kernel(x_ref, g_ref, wa_ref, wb_ref, wo_ref, gf_ref, o_ref, xn_ref, acc_ref, *, final_norm):
    j = pl.program_id(1)

    @pl.when(j == 0)
    def _():
        xn_ref[...] = _rms(x_ref[...], g_ref[...]).astype(BF16)
        acc_ref[...] = jnp.zeros_like(acc_ref)

    xn = xn_ref[...]
    a = _dot(xn, wa_ref[...])
    b = _dot(xn, wb_ref[...])
    hmid = (a * jax.nn.sigmoid(a)) * b
    acc_ref[...] += _dot(hmid.astype(BF16), wo_ref[...])

    @pl.when(j == pl.num_programs(1) - 1)
    def _():
        y = x_ref[...] + 0.5 * acc_ref[...]
        if final_norm:
            o_ref[...] = _rms(y, gf_ref[...])
        else:
            o_ref[...] = y


def _ffn(x, g, wi, wo, gf, final_norm, tm, tf=256):
    m, d = x.shape
    f = wo.shape[0]
    nf = f // tf
    kern = functools.partial(_ffn_kernel, final_norm=final_norm)
    outs = pl.pallas_call(
        kern,
        grid=(m // tm, nf),
        in_specs=[
            pl.BlockSpec((tm, d), lambda i, j: (i, 0)),
            pl.BlockSpec((1, d), lambda i, j: (0, 0)),
            pl.BlockSpec((d, tf), lambda i, j: (0, j)),
            pl.BlockSpec((d, tf), lambda i, j: (0, j + nf)),
            pl.BlockSpec((tf, d), lambda i, j: (j, 0)),
            pl.BlockSpec((1, d), lambda i, j: (0, 0)),
        ],
        out_specs=pl.BlockSpec((tm, d), lambda i, j: (i, 0)),
        out_shape=jax.ShapeDtypeStruct((m, d), F32),
        scratch_shapes=[pltpu.VMEM((tm, d), BF16), pltpu.VMEM((tm, d), F32)],
        compiler_params=_cparams(("parallel", "arbitrary")),
        name="ffn",
    )(x, g.reshape(1, d), wi, wi, wo, gf.reshape(1, d))
    return outs


def _proj_kernel(x_ref, g_ref, w_ref, o_ref, xn_ref):
    @pl.when(pl.program_id(1) == 0)
    def _():
        xn_ref[...] = _rms(x_ref[...], g_ref[...]).astype(BF16)

    o_ref[...] = _dot(xn_ref[...], w_ref[...])


def _proj(x, g, w, tm, tn=256):
    m, d = x.shape
    n = w.shape[1]
    return pl.pallas_call(
        _proj_kernel,
        grid=(m // tm, n // tn),
        in_specs=[
            pl.BlockSpec((tm, d), lambda i, j: (i, 0)),
            pl.BlockSpec((1, d), lambda i, j: (0, 0)),
            pl.BlockSpec((d, tn), lambda i, j: (0, j)),
        ],
        out_specs=pl.BlockSpec((tm, tn), lambda i, j: (i, j)),
        out_shape=jax.ShapeDtypeStruct((m, n), F32),
        scratch_shapes=[pltpu.VMEM((tm, d), BF16)],
        compiler_params=_cparams(("parallel", "arbitrary")),
        name="proj",
    )(x, g.reshape(1, d), w)


def _merge_kernel(x_ref, oa_ref, ob_ref, oc_ref, od_ref, g0_ref, g1_ref, g2_ref, g3_ref, wb_ref, wo_ref, o_ref):
    merged = None
    for i, (o_r, g_r) in enumerate(((oa_ref, g0_ref), (ob_ref, g1_ref), (oc_ref, g2_ref), (od_ref, g3_ref))):
        t = jax.nn.sigmoid(g_r[...]) * _dot(o_r[...].astype(BF16), wb_ref[i])
        merged = t if merged is None else merged + t
    o_ref[...] = x_ref[...] + _dot(merged.astype(BF16), wo_ref[...])


def _merge(x, branches, h, wb, wo, tm):
    m, d = x.shape
    gate_specs = [pl.BlockSpec((tm, d), functools.partial(lambda i, k: (i, k), k=COL_GATE // d + k)) for k in range(N_BRANCH)]
    return pl.pallas_call(
        _merge_kernel,
        grid=(m // tm,),
        in_specs=[pl.BlockSpec((tm, d), lambda i: (i, 0))]
        + [pl.BlockSpec((tm, BRANCH_W), lambda i: (i, 0))] * N_BRANCH
        + gate_specs
        + [pl.BlockSpec((N_BRANCH, BRANCH_W, d), lambda i: (0, 0, 0)), pl.BlockSpec((d, d), lambda i: (0, 0))],
        out_specs=pl.BlockSpec((tm, d), lambda i: (i, 0)),
        out_shape=jax.ShapeDtypeStruct((m, d), F32),
        compiler_params=_cparams(("parallel",)),
        name="merge",
    )(x, *branches, h, h, h, h, wb, wo)


def _forget_kernel(ff_ref, bias_ref, lf_ref, cum_ref, *, t):
    x = ff_ref[...] + bias_ref[...]
    lf = jnp.minimum(x, 0.0) - _softplus_neg_abs(x)
    lf_ref[...] = lf
    w = 128
    tri = (lax.broadcasted_iota(jnp.int32, (w, w), 0) <= lax.broadcasted_iota(jnp.int32, (w, w), 1)).astype(F32)
    carry = jnp.zeros((x.shape[0], 1), F32)
    for c in range(t // w):
        cs = _dot(lf[:, c * w:(c + 1) * w], tri, HIGHEST) + carry
        cum_ref[:, c * w:(c + 1) * w] = cs
        carry = cs[:, w - 1:w]


def _forget(ff_t, bias_rows):
    r, t = ff_t.shape
    return pl.pallas_call(
        functools.partial(_forget_kernel, t=t),
        out_shape=(jax.ShapeDtypeStruct((r, t), F32), jax.ShapeDtypeStruct((r, t), F32)),
        compiler_params=pltpu.CompilerParams(vmem_limit_bytes=VMEM_LIMIT),
        name="forget",
    )(ff_t, bias_rows)


def _fox_kernel(q_ref, k_ref, v_ref, cq_ref, ck_ref, o_ref, m_scr, l_scr, acc_scr, *, tq):
    qi = pl.program_id(1)
    ki = pl.program_id(2)

    @pl.when(ki == 0)
    def _():
        m_scr[...] = jnp.full_like(m_scr, NEG_BIG)
        l_scr[...] = jnp.zeros_like(l_scr)
        acc_scr[...] = jnp.zeros_like(acc_scr)

    @pl.when(ki <= qi)
    def _():
        q = q_ref[0].astype(BF16)
        k = k_ref[0].astype(BF16)
        v = v_ref[0].astype(BF16)
        cq = cq_ref[0]
        ck = ck_ref[0]
        row = qi * tq + lax.broadcasted_iota(jnp.int32, (tq, tq), 0)
        col = ki * tq + lax.broadcasted_iota(jnp.int32, (tq, tq), 1)
        mask = col <= row
        for h, sl in enumerate(_head_slices()):
            s = _dot_nt(q[:, sl], k[:, sl]) * QK_SCALE + cq[:, h:h + 1] - ck[h:h + 1, :]
            s = jnp.where(mask, s, NEG_BIG)
            m_old = m_scr[h]
            m_new = jnp.maximum(m_old, jnp.max(s, axis=-1, keepdims=True))
            p = jnp.exp(s - m_new)
            alpha = jnp.exp(m_old - m_new)
            l_scr[h] = alpha * l_scr[h] + jnp.sum(p, axis=-1, keepdims=True)
            acc_scr[h] = alpha * acc_scr[h] + _dot(p.astype(BF16), v[:, sl])
            m_scr[h] = m_new

    @pl.when(ki == pl.num_programs(2) - 1)
    def _():
        for h, sl in enumerate(_head_slices()):
            o_ref[0, :, sl] = acc_scr[h] / l_scr[h]


def _fox_prompt(h3, cum_q, cum_k, tq=512):
    b, t, _ = h3.shape
    nq = t // tq
    cb = COL_FOX // BLK
    return pl.pallas_call(
        functools.partial(_fox_kernel, tq=tq),
        grid=(b, nq, nq),
        in_specs=[
            pl.BlockSpec((1, tq, BLK), lambda bi, qi, ki: (bi, qi, cb)),
            pl.BlockSpec((1, tq, BLK), lambda bi, qi, ki: (bi, jnp.minimum(ki, qi), cb + 1)),
            pl.BlockSpec((1, tq, BLK), lambda bi, qi, ki: (bi, jnp.minimum(ki, qi), cb + 2)),
            pl.BlockSpec((1, tq, HEADS), lambda bi, qi, ki: (bi, qi, 0)),
            pl.BlockSpec((1, HEADS, tq), lambda bi, qi, ki: (bi, 0, jnp.minimum(ki, qi))),
        ],
        out_specs=pl.BlockSpec((1, tq, BLK), lambda bi, qi, ki: (bi, qi, 0)),
        out_shape=jax.ShapeDtypeStruct((b, t, BLK), F32),
        scratch_shapes=[pltpu.VMEM((HEADS, tq, 1), F32), pltpu.VMEM((HEADS, tq, 1), F32),
                        pltpu.VMEM((HEADS, tq, HEAD_DIM), F32)],
        compiler_params=_cparams(("parallel", "parallel", "arbitrary")),
        name="fox_prompt",
    )(h3, h3, h3, cum_q, cum_k)


def _strict_upper(n):
    return lax.broadcasted_iota(jnp.int32, (n, n), 0) > lax.broadcasted_iota(jnp.int32, (n, n), 1)


def _suffix_excl(x, u_bf16):
    hi = x.astype(BF16)
    lo = (x - hi.astype(F32)).astype(BF16)
    return _dot(hi, u_bf16) + _dot(lo, u_bf16)


def _sb_kernel(q_ref, k_ref, v_ref, o_ref, r_scr, acc_scr, *, tq):
    qi = pl.program_id(1)
    j = pl.program_id(2)

    @pl.when(j == 0)
    def _():
        r_scr[...] = jnp.zeros_like(r_scr)
        acc_scr[...] = jnp.zeros_like(acc_scr)

    @pl.when(j <= qi)
    def _():
        kb = qi - j
        q = q_ref[0].astype(BF16)
        k = k_ref[0].astype(BF16)
        v = v_ref[0].astype(BF16)
        row = qi * tq + lax.broadcasted_iota(jnp.int32, (tq, tq), 0)
        col = kb * tq + lax.broadcasted_iota(jnp.int32, (tq, tq), 1)
        mask = col < row
        u = _strict_upper(tq).astype(BF16)
        for h, sl in enumerate(_head_slices()):
            z = _dot_nt(q[:, sl], k[:, sl]) * QK_SCALE
            sp = _softplus_neg_abs(z)
            ls = jnp.minimum(z, 0.0) - sp
            l1m = jnp.where(mask, -jnp.maximum(z, 0.0) - sp, 0.0)
            excl = _suffix_excl(l1m, u) + r_scr[h]
            a = jnp.where(mask, jnp.exp(ls + excl), 0.0)
            acc_scr[h] += _dot(a.astype(BF16), v[:, sl])
            r_scr[h] += jnp.sum(l1m, axis=-1, keepdims=True)

    @pl.when(j == pl.num_programs(2) - 1)
    def _():
        for h, sl in enumerate(_head_slices()):
            o_ref[0, :, sl] = acc_scr[h]


def _sb_prompt(h3, tq=256):
    b, t, _ = h3.shape
    nq = t // tq
    cb = COL_SB // BLK
    return pl.pallas_call(
        functools.partial(_sb_kernel, tq=tq),
        grid=(b, nq, nq),
        in_specs=[
            pl.BlockSpec((1, tq, BLK), lambda bi, qi, j: (bi, qi, cb)),
            pl.BlockSpec((1, tq, BLK), lambda bi, qi, j: (bi, jnp.maximum(qi - j, 0), cb + 1)),
            pl.BlockSpec((1, tq, BLK), lambda bi, qi, j: (bi, jnp.maximum(qi - j, 0), cb + 2)),
        ],
        out_specs=pl.BlockSpec((1, tq, BLK), lambda bi, qi, j: (bi, qi, 0)),
        out_shape=jax.ShapeDtypeStruct((b, t, BLK), F32),
        scratch_shapes=[pltpu.VMEM((HEADS, tq, 1), F32), pltpu.VMEM((HEADS, tq, HEAD_DIM), F32)],
        compiler_params=_cparams(("parallel", "parallel", "arbitrary")),
        name="sb_prompt",
    )(h3, h3, h3)


GROUP = 8
MAX_SAMPLE_PAGES_PER_STEP = 16


def _page_scores(k_refs, qx_scr, s_scr):
    for j, k_ref in enumerate(k_refs):
        for h, sl in enumerate(_head_slices()):
            s_scr[j * GROUP + h:j * GROUP + h + 1, :] = jnp.sum(k_ref[0, sl, :] * qx_scr[sl, :], axis=0, keepdims=True)


def _page_carries(r, tot, pp):
    carries = []
    for j in range(pp):
        carries.append(r)
        r = r + tot[j * GROUP:(j + 1) * GROUP]
    return jnp.concatenate(carries, axis=0), r


def _weighted_values(w_scr, v_refs, acc_scr, scale):
    for h, sl in enumerate(_head_slices()):
        acc = acc_scr[sl, :] if scale is None else acc_scr[sl, :] * scale[h:h + 1, :]
        for j, v_ref in enumerate(v_refs):
            acc = acc + w_scr[j * GROUP + h:j * GROUP + h + 1, :] * v_ref[0, sl, :]
        acc_scr[sl, :] = acc


def _fox_sample_kernel(pt_ref, q_ref, kn_ref, vn_ref, lfn_ref, *refs, pp):
    k_refs, v_refs, lf_refs = refs[:pp], refs[pp:2 * pp], refs[2 * pp:3 * pp]
    o_ref, qx_scr, s_scr, lf_scr, p_scr, m_scr, l_scr, r_scr, acc_scr = refs[3 * pp:]
    step = pl.program_id(1)

    @pl.when(step == 0)
    def _():
        qcol = q_ref[0] * QK_SCALE
        qx_scr[...] = jnp.broadcast_to(qcol, qx_scr.shape)
        s_scr[...] = jnp.zeros_like(s_scr)
        lf_scr[...] = jnp.zeros_like(lf_scr)
        m_scr[...] = jnp.zeros_like(m_scr)
        own = qcol * kn_ref[0]
        for h, sl in enumerate(_head_slices()):
            m_scr[h:h + 1, :] = jnp.sum(own[sl], axis=0, keepdims=True)
        l_scr[...] = jnp.ones_like(l_scr)
        r_scr[...] = lfn_ref[0]
        lane = lax.broadcasted_iota(jnp.int32, acc_scr.shape, 1)
        acc_scr[...] = jnp.where(lane == 0, vn_ref[0], 0.0)

    for j, lf_ref in enumerate(lf_refs):
        lf_scr[j * GROUP:j * GROUP + HEADS, :] = lf_ref[0]
    _page_scores(k_refs, qx_scr, s_scr)
    lf_all = lf_scr[...]
    excl = _dot(lf_all, _strict_upper(PAGE_SIZE).astype(F32), HIGHEST)
    carries, r_out = _page_carries(r_scr[...], jnp.sum(lf_all, axis=-1, keepdims=True), pp)
    r_scr[...] = r_out
    s_all = s_scr[...] + excl + carries
    row_max = jnp.max(s_all, axis=-1, keepdims=True)
    m_old = m_scr[...]
    m_new = m_old
    for j in range(pp):
        m_new = jnp.maximum(m_new, row_max[j * GROUP:(j + 1) * GROUP])
    alpha = jnp.exp(m_old - m_new)
    p_all = jnp.exp(s_all - jnp.concatenate([m_new] * pp, axis=0))
    p_scr[...] = p_all
    row_sum = jnp.sum(p_all, axis=-1, keepdims=True)
    l_new = alpha * l_scr[...]
    for j in range(pp):
        l_new = l_new + row_sum[j * GROUP:(j + 1) * GROUP]
    l_scr[...] = l_new
    m_scr[...] = m_new
    _weighted_values(p_scr, v_refs, acc_scr, alpha)

    @pl.when(step == pl.num_programs(1) - 1)
    def _():
        l = l_scr[...]
        for h, sl in enumerate(_head_slices()):
            o_ref[0, sl, :] = jnp.sum(acc_scr[sl, :], axis=-1, keepdims=True) / l[h:h + 1, :]


def _sb_sample_kernel(pt_ref, q_ref, *refs, pp):
    k_refs, v_refs = refs[:pp], refs[pp:2 * pp]
    o_ref, qx_scr, z_scr, a_scr, r_scr, acc_scr = refs[2 * pp:]
    step = pl.program_id(1)

    @pl.when(step == 0)
    def _():
        qx_scr[...] = jnp.broadcast_to(q_ref[0] * QK_SCALE, qx_scr.shape)
        z_scr[...] = jnp.zeros_like(z_scr)
        r_scr[...] = jnp.zeros_like(r_scr)
        acc_scr[...] = jnp.zeros_like(acc_scr)

    _page_scores(k_refs, qx_scr, z_scr)
    z = z_scr[...]
    sp = _softplus_neg_abs(z)
    ls = jnp.minimum(z, 0.0) - sp
    l1m = -jnp.maximum(z, 0.0) - sp
    excl = _suffix_excl(l1m, _strict_upper(PAGE_SIZE).astype(BF16))
    carries, r_out = _page_carries(r_scr[...], jnp.sum(l1m, axis=-1, keepdims=True), pp)
    r_scr[...] = r_out
    a_scr[...] = jnp.exp(ls + excl + carries)
    _weighted_values(a_scr, v_refs, acc_scr, None)

    @pl.when(step == pl.num_programs(1) - 1)
    def _():
        o_ref[...] = jnp.sum(acc_scr[...], axis=-1, keepdims=True)[None]


def _pages_per_step(n_pages):
    pp = math.gcd(n_pages, MAX_SAMPLE_PAGES_PER_STEP)
    return pp


def _page_specs(page_table, base, pp, block):
    n_pages = page_table.shape[1]

    def spec(j):
        return pl.BlockSpec(block, lambda bi, st, pt: (base + pt[bi, n_pages - 1 - (st * pp + j)], 0, 0))

    return [spec(j) for j in range(pp)]


def _per_sample(bi, st, pt):
    return (bi, 0, 0)


def _fox_sample(q, k_new, v_new, lf_new, cache_kt, cache_vt, cache_lft, page_table, base):
    b, n_pages = page_table.shape
    pp = _pages_per_step(n_pages)
    col = lambda x: x[:, :, None]
    lfn = col(jnp.concatenate([lf_new, jnp.zeros((b, GROUP - HEADS), F32)], axis=1))
    kv_block = (1, BLK, PAGE_SIZE)
    grid_spec = pltpu.PrefetchScalarGridSpec(
        num_scalar_prefetch=1,
        grid=(b, n_pages // pp),
        in_specs=[pl.BlockSpec((1, BLK, 1), _per_sample)] * 3 + [pl.BlockSpec((1, GROUP, 1), _per_sample)]
        + _page_specs(page_table, base, pp, kv_block) + _page_specs(page_table, base, pp, kv_block)
        + _page_specs(page_table, base, pp, (1, HEADS, PAGE_SIZE)),
        out_specs=pl.BlockSpec((1, BLK, 1), _per_sample),
        scratch_shapes=[pltpu.VMEM((BLK, PAGE_SIZE), F32)] + [pltpu.VMEM((pp * GROUP, PAGE_SIZE), F32)] * 3
        + [pltpu.VMEM((GROUP, 1), F32)] * 3 + [pltpu.VMEM((BLK, PAGE_SIZE), F32)],
    )
    out = pl.pallas_call(
        functools.partial(_fox_sample_kernel, pp=pp),
        grid_spec=grid_spec,
        out_shape=jax.ShapeDtypeStruct((b, BLK, 1), F32),
        compiler_params=_cparams(("parallel", "arbitrary")),
        name="fox_sample",
    )(page_table, col(q), col(k_new), col(v_new), lfn, *([cache_kt] * pp), *([cache_vt] * pp), *([cache_lft] * pp))
    return out[:, :, 0]


def _sb_sample(q, cache_kt, cache_vt, page_table, base):
    b, n_pages = page_table.shape
    pp = _pages_per_step(n_pages)
    kv_block = (1, BLK, PAGE_SIZE)
    grid_spec = pltpu.PrefetchScalarGridSpec(
        num_scalar_prefetch=1,
        grid=(b, n_pages // pp),
        in_specs=[pl.BlockSpec((1, BLK, 1), _per_sample)]
        + _page_specs(page_table, base, pp, kv_block) + _page_specs(page_table, base, pp, kv_block),
        out_specs=pl.BlockSpec((1, BLK, 1), _per_sample),
        scratch_shapes=[pltpu.VMEM((BLK, PAGE_SIZE), F32)] + [pltpu.VMEM((pp * GROUP, PAGE_SIZE), F32)] * 2
        + [pltpu.VMEM((GROUP, 1), F32), pltpu.VMEM((BLK, PAGE_SIZE), F32)],
    )
    out = pl.pallas_call(
        functools.partial(_sb_sample_kernel, pp=pp),
        grid_spec=grid_spec,
        out_shape=jax.ShapeDtypeStruct((b, BLK, 1), F32),
        compiler_params=_cparams(("parallel", "arbitrary")),
        name="sb_sample",
    )(page_table, q[:, :, None], *([cache_kt] * pp), *([cache_vt] * pp))
    return out[:, :, 0]


def _rot_half(x):
    half = HEAD_DIM // 2
    lane = lax.broadcasted_iota(jnp.int32, x.shape, 1) % HEAD_DIM
    n = x.shape[1]
    return jnp.where(lane < half, -pltpu.roll(x, n - half, 1), pltpu.roll(x, half, 1))


def _ret_kernel(q_ref, k_ref, v_ref, g_ref, cos_ref, sin_ref, dmask_ref, xi_ref, zeta_ref, cd_ref, gnw_ref, s0_ref,
                o_ref, st_ref, s_scr):
    c = pl.program_id(1)

    @pl.when(c == 0)
    def _():
        s_scr[...] = s0_ref[0]

    cos = cos_ref[...]
    sin = sin_ref[...]
    q = q_ref[0]
    k = k_ref[0]
    qr = (q * cos + _rot_half(q) * sin).astype(BF16)
    kr = (k * cos + _rot_half(k) * sin) * QK_SCALE
    v = v_ref[0].astype(BF16)
    g = g_ref[0]
    gate = g * jax.nn.sigmoid(g)
    gnw = gnw_ref[...]
    for h, sl in enumerate(_head_slices()):
        qh = qr[:, sl]
        kh = kr[:, sl]
        vh = v[:, sl]
        s = s_scr[h]
        att = _dot_nt(qh, kh.astype(BF16)) * dmask_ref[h]
        o = _dot(att.astype(BF16), vh) + _dot(qh, s.astype(BF16)) * xi_ref[h]
        s_scr[h] = cd_ref[h] * s + _dot_tn((kh * zeta_ref[h]).astype(BF16), vh)
        mu = jnp.mean(o, axis=-1, keepdims=True)
        d = o - mu
        var = jnp.mean(d * d, axis=-1, keepdims=True)
        o_ref[0, :, sl] = gate[:, sl] * (d * lax.rsqrt(var + GN_EPS) * gnw[:, sl])

    @pl.when(c == pl.num_programs(1) - 1)
    def _():
        st_ref[0] = s_scr[...]


def _ret_tables(chunk, n_valid):
    lg = np.log(1.0 - 2.0 ** (-5.0 - np.arange(HEADS, dtype=np.float64)))
    n = np.arange(chunk, dtype=np.float64)
    diff = n[:, None] - n[None, :]
    valid = (n < n_valid).astype(np.float64)
    dmask = np.where(diff[None] >= 0, np.exp(np.maximum(diff, 0.0)[None] * lg[:, None, None]), 0.0)
    dmask = dmask * valid[None, :, None] * valid[None, None, :]
    xi = np.exp((n[None, :] + 1.0) * lg[:, None]) * valid[None, :]
    zeta = np.exp((n_valid - 1.0 - n)[None, :] * lg[:, None]) * valid[None, :]
    cd = np.exp(n_valid * lg)
    f = lambda a: jnp.asarray(a.astype(np.float32))
    return f(dmask), f(xi[:, :, None]), f(zeta[:, :, None]), f(cd[:, None, None])


def _rope_tables(pos):
    half = HEAD_DIM // 2
    inv = ROPE_BASE ** (-jnp.arange(half, dtype=F32) / half)
    ang = pos.astype(F32)[:, None] * inv[None, :]
    cos = jnp.tile(jnp.cos(ang), (1, 2 * HEADS))
    sin = jnp.tile(jnp.sin(ang), (1, 2 * HEADS))
    return cos, sin


def _retention(h3, col0, cos, sin, tables, gnw, s0, chunk):
    b, t, _ = h3.shape
    nc = t // chunk
    cb = col0 // BLK
    dmask, xi, zeta, cd = tables

    def tok(k):
        return pl.BlockSpec((1, chunk, BLK), functools.partial(lambda bi, c, k: (bi, c, k), k=cb + k))

    const3 = lambda bi, c: (0, 0, 0)
    return pl.pallas_call(
        _ret_kernel,
        grid=(b, nc),
        in_specs=[tok(0), tok(1), tok(2), tok(3),
                  pl.BlockSpec((chunk, BLK), lambda bi, c: (c, 0)),
                  pl.BlockSpec((chunk, BLK), lambda bi, c: (c, 0)),
                  pl.BlockSpec((HEADS, chunk, chunk), const3),
                  pl.BlockSpec((HEADS, chunk, 1), const3),
                  pl.BlockSpec((HEADS, chunk, 1), const3),
                  pl.BlockSpec((HEADS, 1, 1), const3),
                  pl.BlockSpec((1, BLK), lambda bi, c: (0, 0)),
                  pl.BlockSpec((1, HEADS, HEAD_DIM, HEAD_DIM), lambda bi, c: (bi, 0, 0, 0))],
        out_specs=[pl.BlockSpec((1, chunk, BLK), lambda bi, c: (bi, c, 0)),
                   pl.BlockSpec((1, HEADS, HEAD_DIM, HEAD_DIM), lambda bi, c: (bi, 0, 0, 0))],
        out_shape=[jax.ShapeDtypeStruct((b, t, BLK), F32),
                   jax.ShapeDtypeStruct((b, HEADS, HEAD_DIM, HEAD_DIM), F32)],
        scratch_shapes=[pltpu.VMEM((HEADS, HEAD_DIM, HEAD_DIM), F32)],
        compiler_params=_cparams(("parallel", "arbitrary")),
        name="retention",
    )(h3, h3, h3, h3, cos, sin, dmask, xi, zeta, cd, gnw.reshape(1, BLK), s0)


def _head_sum_matrix():
    i = lax.broadcasted_iota(jnp.int32, (BLK, BLK), 0) // HEAD_DIM
    j = lax.broadcasted_iota(jnp.int32, (BLK, BLK), 1) // HEAD_DIM
    return (i == j).astype(F32)


def _rwkv_prep_kernel(hr_ref, prev_ref, valid_ref, mu_ref, w0_ref, a0_ref, kk_w_ref, ka_ref, rk_ref,
                      wup_ref, aup_ref, gup_ref,
                      r_ref, k_ref, v_ref, logd_ref, kk_ref, b_ref, g_ref, bonus_ref):
    hr = hr_ref[...]
    xm = hr + (prev_ref[...] - hr) * mu_ref[...]
    valid = valid_ref[...]
    r = xm[:, 0:BLK]
    k = xm[:, BLK:2 * BLK]
    v = xm[:, 2 * BLK:3 * BLK]
    o = 3 * BLK
    wd = xm[:, o:o + DECAY_LORA]
    ad = xm[:, o + DECAY_LORA:o + DECAY_LORA + AAA_LORA]
    gd = xm[:, o + DECAY_LORA + AAA_LORA:]
    wpre = w0_ref[...] + _dot(jnp.tanh(wd).astype(BF16), wup_ref[...])
    w = jnp.minimum(wpre, 0.0) - _softplus_neg_abs(wpre) - 0.5
    a = jax.nn.sigmoid(a0_ref[...] + _dot(ad.astype(BF16), aup_ref[...]))
    g = _dot(jax.nn.sigmoid(gd).astype(BF16), gup_ref[...])
    hs = _head_sum_matrix()
    kk = k * kk_w_ref[...]
    kk = kk * lax.rsqrt(_dot(kk * kk, hs, HIGHEST) + 1e-12)
    kmod = k * (1.0 + (a - 1.0) * ka_ref[...])
    bonus = _dot(r * kmod * rk_ref[...], hs, HIGHEST) * v
    r_ref[...] = r * valid
    k_ref[...] = kmod * valid
    v_ref[...] = v * valid
    logd_ref[...] = -jnp.exp(w) * valid
    kk_ref[...] = kk * valid
    b_ref[...] = kk * a * valid
    g_ref[...] = g
    bonus_ref[...] = bonus


def _rwkv_prep(hr, prev, valid, mu, w0, a0, k_k, k_a, r_k, w_up, a_up, g_up, tm):
    m = hr.shape[0]
    row = lambda n: pl.BlockSpec((1, n), lambda i: (0, 0))
    full = lambda a: pl.BlockSpec(a.shape, lambda i: (0, 0))
    tokb = pl.BlockSpec((tm, BLK), lambda i: (i, 0))
    return pl.pallas_call(
        _rwkv_prep_kernel,
        grid=(m // tm,),
        in_specs=[pl.BlockSpec((tm, RWKV_IN), lambda i: (i, 0)), pl.BlockSpec((tm, RWKV_IN), lambda i: (i, 0)),
                  pl.BlockSpec((tm, 1), lambda i: (i, 0)),
                  row(RWKV_IN), row(BLK), row(BLK), row(BLK), row(BLK), row(BLK),
                  full(w_up), full(a_up), full(g_up)],
        out_specs=[tokb] * 8,
        out_shape=[jax.ShapeDtypeStruct((m, BLK), F32)] * 8,
        compiler_params=_cparams(("parallel",)),
        name="rwkv_prep",
    )(hr, prev, valid, mu.reshape(1, -1), w0.reshape(1, -1), a0.reshape(1, -1), k_k.reshape(1, -1),
      k_a.reshape(1, -1), r_k.reshape(1, -1), w_up, a_up, g_up)


def _rwkv_chunk_kernel(r_ref, k_ref, v_ref, logd_ref, kk_ref, b_ref, r2_ref, y0_ref, g_ref, h_ref, *, chunk, per_step):
    for s in range(per_step):
        _rwkv_one_chunk(r_ref, k_ref, v_ref, logd_ref, kk_ref, b_ref, r2_ref, y0_ref, g_ref, h_ref, chunk, s)


def _rwkv_one_chunk(r_ref, k_ref, v_ref, logd_ref, kk_ref, b_ref, r2_ref, y0_ref, g_ref, h_ref, c, s):
    rows = slice(s * c, (s + 1) * c)
    ri = lax.broadcasted_iota(jnp.int32, (c, c), 0)
    ci = lax.broadcasted_iota(jnp.int32, (c, c), 1)
    incl = (ci <= ri).astype(F32)
    strict = ci < ri
    eye = (ri == ci).astype(F32)
    eye_d = (lax.broadcasted_iota(jnp.int32, (HEAD_DIM, HEAD_DIM), 0)
             == lax.broadcasted_iota(jnp.int32, (HEAD_DIM, HEAD_DIM), 1)).astype(F32)

    logd = logd_ref[0, rows, :]
    logw = _dot(incl, logd, HIGHEST)
    logw_c = logw[c - 1:c, :]
    e_neg = jnp.exp(-logw)
    e_end = jnp.exp(logw_c - logw)
    kk_t = kk_ref[0, rows, :] * jnp.exp(logw - logd)
    r_t = r_ref[0, rows, :] * jnp.exp(logw)
    b = b_ref[0, rows, :]
    k = k_ref[0, rows, :]
    b_t = b * e_neg
    k_t = k * e_neg
    b_e = b * e_end
    k_e = k * e_end
    w_c = jnp.exp(logw_c)
    v = v_ref[0, rows, :]

    for h, sl in enumerate(_head_slices()):
        kkh, rh, bh, kh, vh = kk_t[:, sl], r_t[:, sl], b_t[:, sl], k_t[:, sl], v[:, sl]
        lhs = jnp.concatenate([kkh, rh], axis=0)
        ab_mb = _mm(lhs, bh, _NT, PREC_PAIR)
        ak_mk = _mm(lhs, kh, _NT, PREC_PAIR)
        a_b = jnp.where(strict, ab_mb[:c], 0.0)
        a_k = jnp.where(strict, ak_mk[:c], 0.0)
        m_b = ab_mb[c:] * incl
        m_k = ak_mk[c:] * incl
        npow = -a_b
        tinv = eye + npow
        for _ in range(int(math.log2(c)) - 1):
            npow = _mm(npow, npow, _NN, PREC_INV)
            tinv = tinv + _mm(tinv, npow, _NN, PREC_INV)
        akv_mkv = _mm(jnp.concatenate([a_k, m_k], axis=0), vh, _NN, PREC_APPLY)
        p = _mm(tinv, kkh, _NN, PREC_APPLY)
        qv = _mm(tinv, akv_mkv[:c], _NN, PREC_APPLY)
        r2_ref[0, rows, sl] = rh - _mm(m_b, p, _NN, PREC_APPLY)
        y0_ref[0, rows, sl] = akv_mkv[c:] - _mm(m_b, qv, _NN, PREC_APPLY)
        g_ref[0, s, h] = eye_d * w_c[:, sl] - _mm(p, b_e[:, sl], _TN, PREC_APPLY)
        h_ref[0, s, h] = _mm(vh, k_e[:, sl], _TN, PREC_APPLY) - _mm(qv, b_e[:, sl], _TN, PREC_APPLY)


def _chunks_per_step(nc):
    return math.gcd(nc, RWKV_CHUNKS_PER_STEP)


def _rwkv_chunks(r, k, v, logd, kk, b, chunk):
    bsz, t, _ = r.shape
    nc = t // chunk
    ps = _chunks_per_step(nc)
    tok = pl.BlockSpec((1, ps * chunk, BLK), lambda bi, c: (bi, c, 0))
    mat = pl.BlockSpec((1, ps, HEADS, HEAD_DIM, HEAD_DIM), lambda bi, c: (bi, c, 0, 0, 0))
    mat_shape = jax.ShapeDtypeStruct((bsz, nc, HEADS, HEAD_DIM, HEAD_DIM), F32)
    return pl.pallas_call(
        functools.partial(_rwkv_chunk_kernel, chunk=chunk, per_step=ps),
        grid=(bsz, nc // ps),
        in_specs=[tok] * 6,
        out_specs=[tok, tok, mat, mat],
        out_shape=[jax.ShapeDtypeStruct((bsz, t, BLK), F32)] * 2 + [mat_shape] * 2,
        compiler_params=_cparams(("parallel", "parallel")),
        name="rwkv_chunks",
    )(r, k, v, logd, kk, b)


def _rwkv_scan_kernel(r2_ref, y0_ref, g_ref, h_ref, gate_ref, bonus_ref, lnw_ref, lnb_ref, s0_ref,
                      o_ref, st_ref, s_scr, *, chunk, per_step):
    step = pl.program_id(1)

    @pl.when(step == 0)
    def _():
        s_scr[...] = s0_ref[0]

    lnw = lnw_ref[...]
    lnb = lnb_ref[...]
    for h, sl in enumerate(_head_slices()):
        s = s_scr[h]
        for i in range(per_step):
            rows = slice(i * chunk, (i + 1) * chunk)
            y = _mm(r2_ref[0, rows, sl], s, _NT, PREC_SCAN) + y0_ref[0, rows, sl]
            s = _mm(s, g_ref[0, i, h], _NN, PREC_SCAN) + h_ref[0, i, h]
            mu = jnp.mean(y, axis=-1, keepdims=True)
            d = y - mu
            var = jnp.mean(d * d, axis=-1, keepdims=True)
            yn = d * lax.rsqrt(var + RWKV_GN_EPS) * lnw[:, sl] + lnb[:, sl]
            o_ref[0, rows, sl] = (yn + bonus_ref[0, rows, sl]) * gate_ref[0, rows, sl]
        s_scr[h] = s

    @pl.when(step == pl.num_programs(1) - 1)
    def _():
        st_ref[0] = s_scr[...]


def _rwkv_scan(r2, y0, g, hmat, gate, bonus, ln_w, ln_b, s0, chunk):
    bsz, t, _ = r2.shape
    nc = t // chunk
    ps = _chunks_per_step(nc)
    tok = pl.BlockSpec((1, ps * chunk, BLK), lambda bi, c: (bi, c, 0))
    mat = pl.BlockSpec((1, ps, HEADS, HEAD_DIM, HEAD_DIM), lambda bi, c: (bi, c, 0, 0, 0))
    state = pl.BlockSpec((1, HEADS, HEAD_DIM, HEAD_DIM), lambda bi, c: (bi, 0, 0, 0))
    row = pl.BlockSpec((1, BLK), lambda bi, c: (0, 0))
    return pl.pallas_call(
        functools.partial(_rwkv_scan_kernel, chunk=chunk, per_step=ps),
        grid=(bsz, nc // ps),
        in_specs=[tok, tok, mat, mat, tok, tok, row, row, state],
        out_specs=[tok, state],
        out_shape=[jax.ShapeDtypeStruct((bsz, t, BLK), F32),
                   jax.ShapeDtypeStruct((bsz, HEADS, HEAD_DIM, HEAD_DIM), F32)],
        scratch_shapes=[pltpu.VMEM((HEADS, HEAD_DIM, HEAD_DIM), F32)],
        compiler_params=_cparams(("parallel", "arbitrary")),
        name="rwkv_scan",
    )(r2, y0, g, hmat, gate, bonus, ln_w.reshape(1, BLK), ln_b.reshape(1, BLK), s0)


def _rwkv(hr3, shift0, s0, n_valid, p, chunk, tm):
    bsz, t, _ = hr3.shape
    prev = jnp.concatenate([shift0[:, None, :], hr3[:, :-1]], axis=1)
    valid = jnp.broadcast_to((jnp.arange(t) < n_valid).astype(F32)[None, :, None], (bsz, t, 1))
    m = bsz * t
    outs = _rwkv_prep(hr3.reshape(m, RWKV_IN), prev.reshape(m, RWKV_IN), valid.reshape(m, 1),
                      p["mu"], p["w0"], p["a0"], p["k_k"], p["k_a"], p["r_k"], p["w_up"], p["a_up"], p["g_up"], tm)
    r, k, v, logd, kk, b, gate, bonus = [o.reshape(bsz, t, BLK) for o in outs]
    r2, y0, g, hmat = _rwkv_chunks(r, k, v, logd, kk, b, chunk)
    return _rwkv_scan(r2, y0, g, hmat, gate, bonus, p["ln_w"], p["ln_b"], s0, chunk)


def _reorder_w_in(w):
    fox_end = 3 * BRANCH_W
    ff_end = fox_end + HEADS
    gate_start = w.shape[-1] - N_BRANCH * D_MODEL
    pad = jnp.zeros(w.shape[:-1] + (BRANCH_W - HEADS,), w.dtype)
    out = jnp.concatenate([w[..., gate_start:], w[..., :fox_end], w[..., ff_end:gate_start],
                           w[..., fox_end:ff_end], pad], axis=-1)
    assert out.shape[-1] == PROJ_COLS
    return out.astype(BF16)


def _heads4(x):
    return x.reshape(x.shape[:-1] + (HEADS, HEAD_DIM))


def kernel(x_prompt, x_sample, cache_fox_k, cache_fox_v, cache_fox_logf, cache_sb_k, cache_sb_v, state_ret, state_rwkv, state_rwkv_shift, page_table, w_in, fox_f_bias, ret_gn_w, rwkv_mu, rwkv_w0, rwkv_w_up, rwkv_a0, rwkv_a_up, rwkv_g_up, rwkv_k_k, rwkv_k_a, rwkv_r_k, rwkv_ln_w, rwkv_ln_b, w_branch, w_out, norm_ffn1, ffn1_w_in, ffn1_w_out, norm_mix, norm_ffn2, ffn2_w_in, ffn2_w_out, norm_final):
    depth = w_in.shape[0]
    bp, t, d = x_prompt.shape
    bs = x_sample.shape[0]
    n_pool = cache_fox_k.shape[1]
    past_len = page_table.shape[1] * PAGE_SIZE

    w_in_r = _reorder_w_in(w_in)
    f1i, f1o, f2i, f2o = (a.astype(BF16) for a in (ffn1_w_in, ffn1_w_out, ffn2_w_in, ffn2_w_out))
    wbr = w_branch.astype(BF16)
    wo = w_out.astype(BF16)
    w_up, a_up, g_up = rwkv_w_up.astype(BF16), rwkv_a_up.astype(BF16), rwkv_g_up.astype(BF16)

    pool = lambda c: jnp.transpose(c, (0, 1, 3, 4, 2)).reshape(depth * n_pool, BLK, PAGE_SIZE)
    ck_fox, cv_fox, ck_sb, cv_sb = pool(cache_fox_k), pool(cache_fox_v), pool(cache_sb_k), pool(cache_sb_v)
    lf_t = jnp.swapaxes(cache_fox_logf, 2, 3).reshape(depth * n_pool, HEADS, PAGE_SIZE)

    cos_p, sin_p = _rope_tables(jnp.arange(t))
    s_pad = 8
    cos_s, sin_s = _rope_tables(past_len + jnp.arange(s_pad))
    ret_tab_p = _ret_tables(RET_CHUNK, RET_CHUNK)
    ret_tab_s = _ret_tables(s_pad, 1)
    zeros_state = jnp.zeros((bp, HEADS, HEAD_DIM, HEAD_DIM), F32)

    xp = x_prompt.reshape(bp * t, d)
    xs = x_sample.reshape(bs, d)
    new_p = [[] for _ in range(8)]
    new_s = [[] for _ in range(8)]
    tm_p = 1024

    for l in range(depth):
        last = l == depth - 1
        rw = dict(mu=rwkv_mu[l], w0=rwkv_w0[l], a0=rwkv_a0[l], k_k=rwkv_k_k[l], k_a=rwkv_k_a[l], r_k=rwkv_r_k[l],
                  w_up=w_up[l], a_up=a_up[l], g_up=g_up[l], ln_w=rwkv_ln_w[l], ln_b=rwkv_ln_b[l])

        xp = _ffn(xp, norm_ffn1[l], f1i[l], f1o[l], norm_final, False, tm_p)
        h = _proj(xp, norm_mix[l], w_in_r[l], min(2 * tm_p, bp * t))
        h3 = h.reshape(bp, t, PROJ_COLS)
        ff_t = jnp.swapaxes(h3[:, :, COL_FORGET:COL_FORGET + HEADS], 1, 2).reshape(bp * HEADS, t)
        lf_rows, cum_rows = _forget(ff_t, jnp.tile(fox_f_bias[l], bp)[:, None])
        cum_k = cum_rows.reshape(bp, HEADS, t)
        lf = jnp.swapaxes(lf_rows.reshape(bp, HEADS, t), 1, 2)
        o_a = _fox_prompt(h3, jnp.swapaxes(cum_k, 1, 2), cum_k)
        o_b, ret_p = _retention(h3, COL_RET, cos_p, sin_p, ret_tab_p, ret_gn_w[l], zeros_state, RET_CHUNK)
        hr = h3[:, :, COL_RWKV:COL_RWKV + RWKV_IN]
        o_c, rw_p = _rwkv(hr, jnp.zeros((bp, RWKV_IN), F32), zeros_state, t, rw, RWKV_CHUNK, tm_p)
        o_d = _sb_prompt(h3)
        flat = lambda o: o.reshape(bp * t, BLK)
        xp = _merge(xp, [flat(o_a), flat(o_b), flat(o_c), flat(o_d)], h, wbr[l], wo[l], 512)
        xp = _ffn(xp, norm_ffn2[l], f2i[l], f2o[l], norm_final, last, tm_p)
        for i, a in enumerate((_heads4(h3[:, :, COL_FOX + BLK:COL_FOX + 2 * BLK]),
                               _heads4(h3[:, :, COL_FOX + 2 * BLK:COL_FOX + 3 * BLK]), lf,
                               _heads4(h3[:, :, COL_SB + BLK:COL_SB + 2 * BLK]),
                               _heads4(h3[:, :, COL_SB + 2 * BLK:COL_SB + 3 * BLK]), ret_p, rw_p, hr[:, -1])):
            new_p[i].append(a)

        xs = _ffn(xs, norm_ffn1[l], f1i[l], f1o[l], norm_final, False, bs)
        hs = _proj(xs, norm_mix[l], w_in_r[l], bs)
        col = lambda c0, n=BLK: hs[:, c0:c0 + n]
        ff_s = jnp.swapaxes(col(COL_FORGET, HEADS), 0, 1)
        ff_s = jnp.concatenate([ff_s, jnp.zeros((8 - HEADS, bs), F32)], axis=0)
        bias_s = jnp.concatenate([fox_f_bias[l], jnp.zeros((8 - HEADS,), F32)])[:, None]
        lf_s_rows, _ = _forget(jnp.pad(ff_s, ((0, 0), (0, 128 - bs))), bias_s)
        lf_s = jnp.swapaxes(lf_s_rows[:HEADS, :bs], 0, 1)
        o_a = _fox_sample(col(COL_FOX), col(COL_FOX + BLK), col(COL_FOX + 2 * BLK), lf_s,
                          ck_fox, cv_fox, lf_t, page_table, l * n_pool)
        hs_pad = jnp.pad(hs[:, None, :], ((0, 0), (0, s_pad - 1), (0, 0)))
        o_b, ret_s = _retention(hs_pad, COL_RET, cos_s, sin_s, ret_tab_s, ret_gn_w[l], state_ret[l], s_pad)
        hr_s = hs_pad[:, :, COL_RWKV:COL_RWKV + RWKV_IN]
        o_c, rw_s = _rwkv(hr_s, state_rwkv_shift[l], state_rwkv[l], 1, rw, s_pad, bs * s_pad)
        o_d = _sb_sample(col(COL_SB), ck_sb, cv_sb, page_table, l * n_pool)
        xs = _merge(xs, [o_a, o_b[:, 0], o_c[:, 0], o_d], hs, wbr[l], wo[l], bs)
        xs = _ffn(xs, norm_ffn2[l], f2i[l], f2o[l], norm_final, last, bs)
        for i, a in enumerate((_heads4(col(COL_FOX + BLK))[:, None], _heads4(col(COL_FOX + 2 * BLK))[:, None],
                               lf_s[:, None], _heads4(col(COL_SB + BLK))[:, None],
                               _heads4(col(COL_SB + 2 * BLK))[:, None], ret_s, rw_s, col(COL_RWKV, RWKV_IN))):
            new_s[i].append(a)

    sp = [jnp.stack(v) for v in new_p]
    ss = [jnp.stack(v) for v in new_s]
    return (xp.reshape(bp, t, d), xs.reshape(bs, 1, d), sp[0], sp[1], sp[2], sp[3], sp[4], sp[5], sp[6], sp[7],
            ss[0], ss[1], ss[2], ss[3], ss[4], ss[5], ss[6], ss[7])
```

```python
import functools
import math

import jax
import jax.numpy as jnp
import numpy as np
from jax import lax
from jax.experimental import pallas as pl
from jax.experimental.pallas import tpu as pltpu

F32 = jnp.float32
BF16 = jnp.bfloat16
HIGHEST = lax.Precision.HIGHEST

D_MODEL = 1024
HEADS = 4
HEAD_DIM = 64
BRANCH_W = HEADS * HEAD_DIM
N_BRANCH = 4
DECAY_LORA = 64
AAA_LORA = 64
GATE_LORA = 128
RWKV_IN = 3 * BRANCH_W + DECAY_LORA + AAA_LORA + GATE_LORA
PAGE_SIZE = 128
RET_CHUNK = 128
ROPE_BASE = 10000.0
RMS_EPS = 1e-6
GN_EPS = 1e-5
RWKV_GN_EPS = 64e-5
QK_SCALE = HEAD_DIM ** -0.5
NEG_BIG = -1e30

COL_GATE = 0
COL_FOX = N_BRANCH * D_MODEL
COL_RET = COL_FOX + 3 * BRANCH_W
COL_RWKV = COL_RET + 4 * BRANCH_W
COL_SB = COL_RWKV + RWKV_IN
COL_FORGET = COL_SB + 3 * BRANCH_W
PROJ_COLS = COL_FORGET + BRANCH_W
BLK = BRANCH_W

VMEM_LIMIT = 56 * 1024 * 1024
RWKV_CHUNK = 64
RWKV_CHUNKS_PER_STEP = 4


def _cparams(sem):
    return pltpu.CompilerParams(dimension_semantics=sem, vmem_limit_bytes=VMEM_LIMIT)


def _dot(a, b, precision=None):
    return jnp.dot(a, b, preferred_element_type=F32, precision=precision)


def _dot_nt(a, b, precision=None):
    return lax.dot_general(a, b, (((1,), (1,)), ((), ())), preferred_element_type=F32, precision=precision)


def _dot_tn(a, b, precision=None):
    return lax.dot_general(a, b, (((0,), (0,)), ((), ())), preferred_element_type=F32, precision=precision)


_NN = (((1,), (0,)), ((), ()))
_NT = (((1,), (1,)), ((), ()))
_TN = (((0,), (0,)), ((), ()))


def _mm(a, b, dims, passes):
    dg = lambda x, y: lax.dot_general(x, y, dims, preferred_element_type=F32)
    ah = a.astype(BF16)
    bh = b.astype(BF16)
    if passes == 1:
        return dg(ah, bh)
    al = (a - ah.astype(F32)).astype(BF16)
    bl = (b - bh.astype(F32)).astype(BF16)
    return dg(ah, bh) + (dg(ah, bl) + dg(al, bh))


PREC_PAIR = 1
PREC_INV = 1
PREC_APPLY = 1
PREC_SCAN = 1


def _rms(x, g):
    return x * lax.rsqrt(jnp.mean(x * x, axis=-1, keepdims=True) + RMS_EPS) * g


def _softplus_neg_abs(z):
    return jnp.log(1.0 + jnp.exp(-jnp.abs(z)))


def _head_slices():
    return [slice(h * HEAD_DIM, (h + 1) * HEAD_DIM) for h in range(HEADS)]


def _ffn_kernel(x_ref, g_ref, wa_ref, wb_ref, wo_ref, gf_ref, o_ref, xn_ref, acc_ref, *, final_norm):
    j = pl.program_id(1)

    @pl.when(j == 0)
    def _():
        xn_ref[...] = _rms(x_ref[...], g_ref[...]).astype(BF16)
        acc_ref[...] = jnp.zeros_like(acc_ref)

    xn = xn_ref[...]
    a = _dot(xn, wa_ref[...])
    b = _dot(xn, wb_ref[...])
    hmid = (a * jax.nn.sigmoid(a)) * b
    acc_ref[...] += _dot(hmid.astype(BF16), wo_ref[...])

    @pl.when(j == pl.num_programs(1) - 1)
    def _():
        y = x_ref[...] + 0.5 * acc_ref[...]
        if final_norm:
            o_ref[...] = _rms(y, gf_ref[...])
        else:
            o_ref[...] = y


def _ffn(x, g, wi, wo, gf, final_norm, tm, tf=256):
    m, d = x.shape
    f = wo.shape[0]
    nf = f // tf
    kern = functools.partial(_ffn_kernel, final_norm=final_norm)
    outs = pl.pallas_call(
        kern,
        grid=(m // tm, nf),
        in_specs=[
            pl.BlockSpec((tm, d), lambda i, j: (i, 0)),
            pl.BlockSpec((1, d), lambda i, j: (0, 0)),
            pl.BlockSpec((d, tf), lambda i, j: (0, j)),
            pl.BlockSpec((d, tf), lambda i, j: (0, j + nf)),
            pl.BlockSpec((tf, d), lambda i, j: (j, 0)),
            pl.BlockSpec((1, d), lambda i, j: (0, 0)),
        ],
        out_specs=pl.BlockSpec((tm, d), lambda i, j: (i, 0)),
        out_shape=jax.ShapeDtypeStruct((m, d), F32),
        scratch_shapes=[pltpu.VMEM((tm, d), BF16), pltpu.VMEM((tm, d), F32)],
        compiler_params=_cparams(("parallel", "arbitrary")),
        name="ffn",
    )(x, g.reshape(1, d), wi, wi, wo, gf.reshape(1, d))
    return outs


def _proj_kernel(x_ref, g_ref, w_ref, o_ref, xn_ref):
    @pl.when(pl.program_id(1) == 0)
    def _():
        xn_ref[...] = _rms(x_ref[...], g_ref[...]).astype(BF16)

    o_ref[...] = _dot(xn_ref[...], w_ref[...])


def _proj(x, g, w, tm, tn=256):
    m, d = x.shape
    n = w.shape[1]
    return pl.pallas_call(
        _proj_kernel,
        grid=(m // tm, n // tn),
        in_specs=[
            pl.BlockSpec((tm, d), lambda i, j: (i, 0)),
            pl.BlockSpec((1, d), lambda i, j: (0, 0)),
            pl.BlockSpec((d, tn), lambda i, j: (0, j)),
        ],
        out_specs=pl.BlockSpec((tm, tn), lambda i, j: (i, j)),
        out_shape=jax.ShapeDtypeStruct((m, n), F32),
        scratch_shapes=[pltpu.VMEM((tm, d), BF16)],
        compiler_params=_cparams(("parallel", "arbitrary")),
        name="proj",
    )(x, g.reshape(1, d), w)


def _merge_kernel(x_ref, oa_ref, ob_ref, oc_ref, od_ref, g0_ref, g1_ref, g2_ref, g3_ref, wb_ref, wo_ref, o_ref):
    merged = None
    for i, (o_r, g_r) in enumerate(((oa_ref, g0_ref), (ob_ref, g1_ref), (oc_ref, g2_ref), (od_ref, g3_ref))):
        t = jax.nn.sigmoid(g_r[...]) * _dot(o_r[...].astype(BF16), wb_ref[i])
        merged = t if merged is None else merged + t
    o_ref[...] = x_ref[...] + _dot(merged.astype(BF16), wo_ref[...])


def _merge(x, branches, h, wb, wo, tm):
    m, d = x.shape
    gate_specs = [pl.BlockSpec((tm, d), functools.partial(lambda i, k: (i, k), k=COL_GATE // d + k)) for k in range(N_BRANCH)]
    return pl.pallas_call(
        _merge_kernel,
        grid=(m // tm,),
        in_specs=[pl.BlockSpec((tm, d), lambda i: (i, 0))]
        + [pl.BlockSpec((tm, BRANCH_W), lambda i: (i, 0))] * N_BRANCH
        + gate_specs
        + [pl.BlockSpec((N_BRANCH, BRANCH_W, d), lambda i: (0, 0, 0)), pl.BlockSpec((d, d), lambda i: (0, 0))],
        out_specs=pl.BlockSpec((tm, d), lambda i: (i, 0)),
        out_shape=jax.ShapeDtypeStruct((m, d), F32),
        compiler_params=_cparams(("parallel",)),
        name="merge",
    )(x, *branches, h, h, h, h, wb, wo)


def _forget_kernel(ff_ref, bias_ref, lf_ref, cum_ref, *, t):
    x = ff_ref[...] + bias_ref[...]
    lf = jnp.minimum(x, 0.0) - _softplus_neg_abs(x)
    lf_ref[...] = lf
    w = 128
    tri = (lax.broadcasted_iota(jnp.int32, (w, w), 0) <= lax.broadcasted_iota(jnp.int32, (w, w), 1)).astype(F32)
    carry = jnp.zeros((x.shape[0], 1), F32)
    for c in range(t // w):
        cs = _dot(lf[:, c * w:(c + 1) * w], tri, HIGHEST) + carry
        cum_ref[:, c * w:(c + 1) * w] = cs
        carry = cs[:, w - 1:w]


def _forget(ff_t, bias_rows):
    r, t = ff_t.shape
    return pl.pallas_call(
        functools.partial(_forget_kernel, t=t),
        out_shape=(jax.ShapeDtypeStruct((r, t), F32), jax.ShapeDtypeStruct((r, t), F32)),
        compiler_params=pltpu.CompilerParams(vmem_limit_bytes=VMEM_LIMIT),
        name="forget",
    )(ff_t, bias_rows)


def _fox_kernel(q_ref, k_ref, v_ref, cq_ref, ck_ref, o_ref, m_scr, l_scr, acc_scr, *, tq):
    qi = pl.program_id(1)
    ki = pl.program_id(2)

    @pl.when(ki == 0)
    def _():
        m_scr[...] = jnp.full_like(m_scr, NEG_BIG)
        l_scr[...] = jnp.zeros_like(l_scr)
        acc_scr[...] = jnp.zeros_like(acc_scr)

    def block(diagonal):
        q = (q_ref[0] * QK_SCALE).astype(BF16)
        k = k_ref[0].astype(BF16)
        v = v_ref[0].astype(BF16)
        cq = cq_ref[0]
        ck = ck_ref[0]
        if diagonal:
            mask = lax.broadcasted_iota(jnp.int32, (tq, tq), 1) <= lax.broadcasted_iota(jnp.int32, (tq, tq), 0)
        for h, sl in enumerate(_head_slices()):
            s = _dot_nt(q[:, sl], k[:, sl]) + cq[:, h:h + 1] - ck[h:h + 1, :]
            if diagonal:
                s = jnp.where(mask, s, NEG_BIG)
            m_old = m_scr[h]
            m_new = jnp.maximum(m_old, jnp.max(s, axis=-1, keepdims=True))
            p = jnp.exp(s - m_new)
            alpha = jnp.exp(m_old - m_new)
            l_scr[h] = alpha * l_scr[h] + jnp.sum(p, axis=-1, keepdims=True)
            acc_scr[h] = alpha * acc_scr[h] + _dot(p.astype(BF16), v[:, sl])
            m_scr[h] = m_new

    pl.when(ki < qi)(functools.partial(block, False))
    pl.when(ki == qi)(functools.partial(block, True))

    @pl.when(ki == pl.num_programs(2) - 1)
    def _():
        for h, sl in enumerate(_head_slices()):
            o_ref[0, :, sl] = acc_scr[h] / l_scr[h]


def _fox_prompt(h3, cum_q, cum_k, tq=512):
    b, t, _ = h3.shape
    nq = t // tq
    cb = COL_FOX // BLK
    return pl.pallas_call(
        functools.partial(_fox_kernel, tq=tq),
        grid=(b, nq, nq),
        in_specs=[
            pl.BlockSpec((1, tq, BLK), lambda bi, qi, ki: (bi, qi, cb)),
            pl.BlockSpec((1, tq, BLK), lambda bi, qi, ki: (bi, jnp.minimum(ki, qi), cb + 1)),
            pl.BlockSpec((1, tq, BLK), lambda bi, qi, ki: (bi, jnp.minimum(ki, qi), cb + 2)),
            pl.BlockSpec((1, tq, HEADS), lambda bi, qi, ki: (bi, qi, 0)),
            pl.BlockSpec((1, HEADS, tq), lambda bi, qi, ki: (bi, 0, jnp.minimum(ki, qi))),
        ],
        out_specs=pl.BlockSpec((1, tq, BLK), lambda bi, qi, ki: (bi, qi, 0)),
        out_shape=jax.ShapeDtypeStruct((b, t, BLK), F32),
        scratch_shapes=[pltpu.VMEM((HEADS, tq, 1), F32), pltpu.VMEM((HEADS, tq, 1), F32),
                        pltpu.VMEM((HEADS, tq, HEAD_DIM), F32)],
        compiler_params=_cparams(("parallel", "parallel", "arbitrary")),
        name="fox_prompt",
    )(h3, h3, h3, cum_q, cum_k)


def _strict_upper(n):
    return lax.broadcasted_iota(jnp.int32, (n, n), 0) > lax.broadcasted_iota(jnp.int32, (n, n), 1)


def _suffix_excl(x, u_bf16):
    hi = x.astype(BF16)
    lo = (x - hi.astype(F32)).astype(BF16)
    return _dot(hi, u_bf16) + _dot(lo, u_bf16)


def _sb_kernel(q_ref, k_ref, v_ref, o_ref, r_scr, acc_scr, *, tq):
    qi = pl.program_id(1)
    j = pl.program_id(2)

    @pl.when(j == 0)
    def _():
        r_scr[...] = jnp.zeros_like(r_scr)
        acc_scr[...] = jnp.zeros_like(acc_scr)

    def block(diagonal):
        q = (q_ref[0] * QK_SCALE).astype(BF16)
        k = k_ref[0].astype(BF16)
        v = v_ref[0].astype(BF16)
        u = _strict_upper(tq).astype(BF16)
        if diagonal:
            mask = lax.broadcasted_iota(jnp.int32, (tq, tq), 1) < lax.broadcasted_iota(jnp.int32, (tq, tq), 0)
        for h, sl in enumerate(_head_slices()):
            z = _dot_nt(q[:, sl], k[:, sl])
            sp = _softplus_neg_abs(z)
            ls = jnp.minimum(z, 0.0) - sp
            l1m = -jnp.maximum(z, 0.0) - sp
            if diagonal:
                l1m = jnp.where(mask, l1m, 0.0)
            a = jnp.exp(ls + _suffix_excl(l1m, u) + r_scr[h])
            if diagonal:
                a = jnp.where(mask, a, 0.0)
            acc_scr[h] += _dot(a.astype(BF16), v[:, sl])
            r_scr[h] += jnp.sum(l1m, axis=-1, keepdims=True)

    pl.when(j == 0)(functools.partial(block, True))
    pl.when((j > 0) & (j <= qi))(functools.partial(block, False))

    @pl.when(j == pl.num_programs(2) - 1)
    def _():
        for h, sl in enumerate(_head_slices()):
            o_ref[0, :, sl] = acc_scr[h]


def _sb_prompt(h3, tq=256):
    b, t, _ = h3.shape
    nq = t // tq
    cb = COL_SB // BLK
    return pl.pallas_call(
        functools.partial(_sb_kernel, tq=tq),
        grid=(b, nq, nq),
        in_specs=[
            pl.BlockSpec((1, tq, BLK), lambda bi, qi, j: (bi, qi, cb)),
            pl.BlockSpec((1, tq, BLK), lambda bi, qi, j: (bi, jnp.maximum(qi - j, 0), cb + 1)),
            pl.BlockSpec((1, tq, BLK), lambda bi, qi, j: (bi, jnp.maximum(qi - j, 0), cb + 2)),
        ],
        out_specs=pl.BlockSpec((1, tq, BLK), lambda bi, qi, j: (bi, qi, 0)),
        out_shape=jax.ShapeDtypeStruct((b, t, BLK), F32),
        scratch_shapes=[pltpu.VMEM((HEADS, tq, 1), F32), pltpu.VMEM((HEADS, tq, HEAD_DIM), F32)],
        compiler_params=_cparams(("parallel", "parallel", "arbitrary")),
        name="sb_prompt",
    )(h3, h3, h3)


GROUP = 8
MAX_SAMPLE_PAGES_PER_STEP = 16


def _page_scores(k_refs, qx_scr, s_scr):
    for j, k_ref in enumerate(k_refs):
        for h, sl in enumerate(_head_slices()):
            s_scr[j * GROUP + h:j * GROUP + h + 1, :] = jnp.sum(k_ref[0, sl, :] * qx_scr[sl, :], axis=0, keepdims=True)


def _page_carries(r, tot, pp):
    carries = []
    for j in range(pp):
        carries.append(r)
        r = r + tot[j * GROUP:(j + 1) * GROUP]
    return jnp.concatenate(carries, axis=0), r


def _weighted_values(w_scr, v_refs, acc_scr, scale):
    for h, sl in enumerate(_head_slices()):
        acc = acc_scr[sl, :] if scale is None else acc_scr[sl, :] * scale[h:h + 1, :]
        for j, v_ref in enumerate(v_refs):
            acc = acc + w_scr[j * GROUP + h:j * GROUP + h + 1, :] * v_ref[0, sl, :]
        acc_scr[sl, :] = acc


def _fox_sample_kernel(pt_ref, q_ref, kn_ref, vn_ref, lfn_ref, *refs, pp):
    k_refs, v_refs, lf_refs = refs[:pp], refs[pp:2 * pp], refs[2 * pp:3 * pp]
    o_ref, qx_scr, s_scr, lf_scr, p_scr, m_scr, l_scr, r_scr, acc_scr = refs[3 * pp:]
    step = pl.program_id(1)

    @pl.when(step == 0)
    def _():
        qcol = q_ref[0] * QK_SCALE
        qx_scr[...] = jnp.broadcast_to(qcol, qx_scr.shape)
        s_scr[...] = jnp.zeros_like(s_scr)
        lf_scr[...] = jnp.zeros_like(lf_scr)
        m_scr[...] = jnp.zeros_like(m_scr)
        own = qcol * kn_ref[0]
        for h, sl in enumerate(_head_slices()):
            m_scr[h:h + 1, :] = jnp.sum(own[sl], axis=0, keepdims=True)
        l_scr[...] = jnp.ones_like(l_scr)
        r_scr[...] = lfn_ref[0]
        lane = lax.broadcasted_iota(jnp.int32, acc_scr.shape, 1)
        acc_scr[...] = jnp.where(lane == 0, vn_ref[0], 0.0)

    for j, lf_ref in enumerate(lf_refs):
        lf_scr[j * GROUP:j * GROUP + HEADS, :] = lf_ref[0]
    _page_scores(k_refs, qx_scr, s_scr)
    lf_all = lf_scr[...]
    excl = _dot(lf_all, _strict_upper(PAGE_SIZE).astype(F32), HIGHEST)
    carries, r_out = _page_carries(r_scr[...], jnp.sum(lf_all, axis=-1, keepdims=True), pp)
    r_scr[...] = r_out
    s_all = s_scr[...] + excl + carries
    row_max = jnp.max(s_all, axis=-1, keepdims=True)
    m_old = m_scr[...]
    m_new = m_old
    for j in range(pp):
        m_new = jnp.maximum(m_new, row_max[j * GROUP:(j + 1) * GROUP])
    alpha = jnp.exp(m_old - m_new)
    p_all = jnp.exp(s_all - jnp.concatenate([m_new] * pp, axis=0))
    p_scr[...] = p_all
    row_sum = jnp.sum(p_all, axis=-1, keepdims=True)
    l_new = alpha * l_scr[...]
    for j in range(pp):
        l_new = l_new + row_sum[j * GROUP:(j + 1) * GROUP]
    l_scr[...] = l_new
    m_scr[...] = m_new
    _weighted_values(p_scr, v_refs, acc_scr, alpha)

    @pl.when(step == pl.num_programs(1) - 1)
    def _():
        l = l_scr[...]
        for h, sl in enumerate(_head_slices()):
            o_ref[0, sl, :] = jnp.sum(acc_scr[sl, :], axis=-1, keepdims=True) / l[h:h + 1, :]


def _sb_sample_kernel(pt_ref, q_ref, *refs, pp):
    k_refs, v_refs = refs[:pp], refs[pp:2 * pp]
    o_ref, qx_scr, z_scr, a_scr, r_scr, acc_scr = refs[2 * pp:]
    step = pl.program_id(1)

    @pl.when(step == 0)
    def _():
        qx_scr[...] = jnp.broadcast_to(q_ref[0] * QK_SCALE, qx_scr.shape)
        z_scr[...] = jnp.zeros_like(z_scr)
        r_scr[...] = jnp.zeros_like(r_scr)
        acc_scr[...] = jnp.zeros_like(acc_scr)

    _page_scores(k_refs, qx_scr, z_scr)
    z = z_scr[...]
    sp = _softplus_neg_abs(z)
    ls = jnp.minimum(z, 0.0) - sp
    l1m = -jnp.maximum(z, 0.0) - sp
    excl = _suffix_excl(l1m, _strict_upper(PAGE_SIZE).astype(BF16))
    carries, r_out = _page_carries(r_scr[...], jnp.sum(l1m, axis=-1, keepdims=True), pp)
    r_scr[...] = r_out
    a_scr[...] = jnp.exp(ls + excl + carries)
    _weighted_values(a_scr, v_refs, acc_scr, None)

    @pl.when(step == pl.num_programs(1) - 1)
    def _():
        o_ref[...] = jnp.sum(acc_scr[...], axis=-1, keepdims=True)[None]


def _pages_per_step(n_pages):
    pp = math.gcd(n_pages, MAX_SAMPLE_PAGES_PER_STEP)
    return pp


def _page_specs(page_table, base, pp, block):
    n_pages = page_table.shape[1]

    def spec(j):
        return pl.BlockSpec(block, lambda bi, st, pt: (base + pt[bi, n_pages - 1 - (st * pp + j)], 0, 0))

    return [spec(j) for j in range(pp)]


def _per_sample(bi, st, pt):
    return (bi, 0, 0)


def _fox_sample(q, k_new, v_new, lf_new, cache_kt, cache_vt, cache_lft, page_table, base):
    b, n_pages = page_table.shape
    pp = _pages_per_step(n_pages)
    col = lambda x: x[:, :, None]
    lfn = col(jnp.concatenate([lf_new, jnp.zeros((b, GROUP - HEADS), F32)], axis=1))
    kv_block = (1, BLK, PAGE_SIZE)
    grid_spec = pltpu.PrefetchScalarGridSpec(
        num_scalar_prefetch=1,
        grid=(b, n_pages // pp),
        in_specs=[pl.BlockSpec((1, BLK, 1), _per_sample)] * 3 + [pl.BlockSpec((1, GROUP, 1), _per_sample)]
        + _page_specs(page_table, base, pp, kv_block) + _page_specs(page_table, base, pp, kv_block)
        + _page_specs(page_table, base, pp, (1, HEADS, PAGE_SIZE)),
        out_specs=pl.BlockSpec((1, BLK, 1), _per_sample),
        scratch_shapes=[pltpu.VMEM((BLK, PAGE_SIZE), F32)] + [pltpu.VMEM((pp * GROUP, PAGE_SIZE), F32)] * 3
        + [pltpu.VMEM((GROUP, 1), F32)] * 3 + [pltpu.VMEM((BLK, PAGE_SIZE), F32)],
    )
    out = pl.pallas_call(
        functools.partial(_fox_sample_kernel, pp=pp),
        grid_spec=grid_spec,
        out_shape=jax.ShapeDtypeStruct((b, BLK, 1), F32),
        compiler_params=_cparams(("parallel", "arbitrary")),
        name="fox_sample",
    )(page_table, col(q), col(k_new), col(v_new), lfn, *([cache_kt] * pp), *([cache_vt] * pp), *([cache_lft] * pp))
    return out[:, :, 0]


def _sb_sample(q, cache_kt, cache_vt, page_table, base):
    b, n_pages = page_table.shape
    pp = _pages_per_step(n_pages)
    kv_block = (1, BLK, PAGE_SIZE)
    grid_spec = pltpu.PrefetchScalarGridSpec(
        num_scalar_prefetch=1,
        grid=(b, n_pages // pp),
        in_specs=[pl.BlockSpec((1, BLK, 1), _per_sample)]
        + _page_specs(page_table, base, pp, kv_block) + _page_specs(page_table, base, pp, kv_block),
        out_specs=pl.BlockSpec((1, BLK, 1), _per_sample),
        scratch_shapes=[pltpu.VMEM((BLK, PAGE_SIZE), F32)] + [pltpu.VMEM((pp * GROUP, PAGE_SIZE), F32)] * 2
        + [pltpu.VMEM((GROUP, 1), F32), pltpu.VMEM((BLK, PAGE_SIZE), F32)],
    )
    out = pl.pallas_call(
        functools.partial(_sb_sample_kernel, pp=pp),
        grid_spec=grid_spec,
        out_shape=jax.ShapeDtypeStruct((b, BLK, 1), F32),
        compiler_params=_cparams(("parallel", "arbitrary")),
        name="sb_sample",
    )(page_table, q[:, :, None], *([cache_kt] * pp), *([cache_vt] * pp))
    return out[:, :, 0]


def _rot_half(x):
    half = HEAD_DIM // 2
    lane = lax.broadcasted_iota(jnp.int32, x.shape, 1) % HEAD_DIM
    n = x.shape[1]
    return jnp.where(lane < half, -pltpu.roll(x, n - half, 1), pltpu.roll(x, half, 1))


def _ret_kernel(q_ref, k_ref, v_ref, g_ref, cos_ref, sin_ref, dmask_ref, xi_ref, zeta_ref, cd_ref, gnw_ref, s0_ref,
                o_ref, st_ref, s_scr):
    c = pl.program_id(1)

    @pl.when(c == 0)
    def _():
        s_scr[...] = s0_ref[0]

    cos = cos_ref[...]
    sin = sin_ref[...]
    q = q_ref[0]
    k = k_ref[0]
    qr = (q * cos + _rot_half(q) * sin).astype(BF16)
    kr = (k * cos + _rot_half(k) * sin) * QK_SCALE
    v = v_ref[0].astype(BF16)
    g = g_ref[0]
    gate = g * jax.nn.sigmoid(g)
    gnw = gnw_ref[...]
    for h, sl in enumerate(_head_slices()):
        qh = qr[:, sl]
        kh = kr[:, sl]
        vh = v[:, sl]
        s = s_scr[h]
        att = _dot_nt(qh, kh.astype(BF16)) * dmask_ref[h]
        o = _dot(att.astype(BF16), vh) + _dot(qh, s.astype(BF16)) * xi_ref[h]
        s_scr[h] = cd_ref[h] * s + _dot_tn((kh * zeta_ref[h]).astype(BF16), vh)
        mu = jnp.mean(o, axis=-1, keepdims=True)
        d = o - mu
        var = jnp.mean(d * d, axis=-1, keepdims=True)
        o_ref[0, :, sl] = gate[:, sl] * (d * lax.rsqrt(var + GN_EPS) * gnw[:, sl])

    @pl.when(c == pl.num_programs(1) - 1)
    def _():
        st_ref[0] = s_scr[...]


def _ret_tables(chunk, n_valid):
    lg = np.log(1.0 - 2.0 ** (-5.0 - np.arange(HEADS, dtype=np.float64)))
    n = np.arange(chunk, dtype=np.float64)
    diff = n[:, None] - n[None, :]
    valid = (n < n_valid).astype(np.float64)
    dmask = np.where(diff[None] >= 0, np.exp(np.maximum(diff, 0.0)[None] * lg[:, None, None]), 0.0)
    dmask = dmask * valid[None, :, None] * valid[None, None, :]
    xi = np.exp((n[None, :] + 1.0) * lg[:, None]) * valid[None, :]
    zeta = np.exp((n_valid - 1.0 - n)[None, :] * lg[:, None]) * valid[None, :]
    cd = np.exp(n_valid * lg)
    f = lambda a: jnp.asarray(a.astype(np.float32))
    return f(dmask), f(xi[:, :, None]), f(zeta[:, :, None]), f(cd[:, None, None])


def _rope_tables(pos):
    half = HEAD_DIM // 2
    inv = ROPE_BASE ** (-jnp.arange(half, dtype=F32) / half)
    ang = pos.astype(F32)[:, None] * inv[None, :]
    cos = jnp.tile(jnp.cos(ang), (1, 2 * HEADS))
    sin = jnp.tile(jnp.sin(ang), (1, 2 * HEADS))
    return cos, sin


def _retention(h3, col0, cos, sin, tables, gnw, s0, chunk):
    b, t, _ = h3.shape
    nc = t // chunk
    cb = col0 // BLK
    dmask, xi, zeta, cd = tables

    def tok(k):
        return pl.BlockSpec((1, chunk, BLK), functools.partial(lambda bi, c, k: (bi, c, k), k=cb + k))

    const3 = lambda bi, c: (0, 0, 0)
    return pl.pallas_call(
        _ret_kernel,
        grid=(b, nc),
        in_specs=[tok(0), tok(1), tok(2), tok(3),
                  pl.BlockSpec((chunk, BLK), lambda bi, c: (c, 0)),
                  pl.BlockSpec((chunk, BLK), lambda bi, c: (c, 0)),
                  pl.BlockSpec((HEADS, chunk, chunk), const3),
                  pl.BlockSpec((HEADS, chunk, 1), const3),
                  pl.BlockSpec((HEADS, chunk, 1), const3),
                  pl.BlockSpec((HEADS, 1, 1), const3),
                  pl.BlockSpec((1, BLK), lambda bi, c: (0, 0)),
                  pl.BlockSpec((1, HEADS, HEAD_DIM, HEAD_DIM), lambda bi, c: (bi, 0, 0, 0))],
        out_specs=[pl.BlockSpec((1, chunk, BLK), lambda bi, c: (bi, c, 0)),
                   pl.BlockSpec((1, HEADS, HEAD_DIM, HEAD_DIM), lambda bi, c: (bi, 0, 0, 0))],
        out_shape=[jax.ShapeDtypeStruct((b, t, BLK), F32),
                   jax.ShapeDtypeStruct((b, HEADS, HEAD_DIM, HEAD_DIM), F32)],
        scratch_shapes=[pltpu.VMEM((HEADS, HEAD_DIM, HEAD_DIM), F32)],
        compiler_params=_cparams(("parallel", "arbitrary")),
        name="retention",
    )(h3, h3, h3, h3, cos, sin, dmask, xi, zeta, cd, gnw.reshape(1, BLK), s0)


def _head_sum_matrix():
    i = lax.broadcasted_iota(jnp.int32, (BLK, BLK), 0) // HEAD_DIM
    j = lax.broadcasted_iota(jnp.int32, (BLK, BLK), 1) // HEAD_DIM
    return (i == j).astype(F32)


def _rwkv_prep_kernel(hr_ref, prev_ref, valid_ref, mu_ref, w0_ref, a0_ref, kk_w_ref, ka_ref, rk_ref,
                      wup_ref, aup_ref, gup_ref,
                      r_ref, k_ref, v_ref, logd_ref, kk_ref, b_ref, g_ref, bonus_ref):
    hr = hr_ref[...]
    xm = hr + (prev_ref[...] - hr) * mu_ref[...]
    valid = valid_ref[...]
    r = xm[:, 0:BLK]
    k = xm[:, BLK:2 * BLK]
    v = xm[:, 2 * BLK:3 * BLK]
    o = 3 * BLK
    wd = xm[:, o:o + DECAY_LORA]
    ad = xm[:, o + DECAY_LORA:o + DECAY_LORA + AAA_LORA]
    gd = xm[:, o + DECAY_LORA + AAA_LORA:]
    wpre = w0_ref[...] + _dot(jnp.tanh(wd).astype(BF16), wup_ref[...])
    w = jnp.minimum(wpre, 0.0) - _softplus_neg_abs(wpre) - 0.5
    a = jax.nn.sigmoid(a0_ref[...] + _dot(ad.astype(BF16), aup_ref[...]))
    g = _dot(jax.nn.sigmoid(gd).astype(BF16), gup_ref[...])
    hs = _head_sum_matrix()
    kk = k * kk_w_ref[...]
    kk = kk * lax.rsqrt(_dot(kk * kk, hs, HIGHEST) + 1e-12)
    kmod = k * (1.0 + (a - 1.0) * ka_ref[...])
    bonus = _dot(r * kmod * rk_ref[...], hs, HIGHEST) * v
    r_ref[...] = r * valid
    k_ref[...] = kmod * valid
    v_ref[...] = v * valid
    logd_ref[...] = -jnp.exp(w) * valid
    kk_ref[...] = kk * valid
    b_ref[...] = kk * a * valid
    g_ref[...] = g
    bonus_ref[...] = bonus


def _rwkv_prep(hr, prev, valid, mu, w0, a0, k_k, k_a, r_k, w_up, a_up, g_up, tm):
    m = hr.shape[0]
    row = lambda n: pl.BlockSpec((1, n), lambda i: (0, 0))
    full = lambda a: pl.BlockSpec(a.shape, lambda i: (0, 0))
    tokb = pl.BlockSpec((tm, BLK), lambda i: (i, 0))
    return pl.pallas_call(
        _rwkv_prep_kernel,
        grid=(m // tm,),
        in_specs=[pl.BlockSpec((tm, RWKV_IN), lambda i: (i, 0)), pl.BlockSpec((tm, RWKV_IN), lambda i: (i, 0)),
                  pl.BlockSpec((tm, 1), lambda i: (i, 0)),
                  row(RWKV_IN), row(BLK), row(BLK), row(BLK), row(BLK), row(BLK),
                  full(w_up), full(a_up), full(g_up)],
        out_specs=[tokb] * 8,
        out_shape=[jax.ShapeDtypeStruct((m, BLK), F32)] * 8,
        compiler_params=_cparams(("parallel",)),
        name="rwkv_prep",
    )(hr, prev, valid, mu.reshape(1, -1), w0.reshape(1, -1), a0.reshape(1, -1), k_k.reshape(1, -1),
      k_a.reshape(1, -1), r_k.reshape(1, -1), w_up, a_up, g_up)


def _rwkv_chunk_kernel(r_ref, k_ref, v_ref, logd_ref, kk_ref, b_ref, r2_ref, y0_ref, gt_ref, ht_ref, *, chunk, per_step):
    c = chunk
    n = HEADS * c
    ri = lax.broadcasted_iota(jnp.int32, (c, c), 0)
    ci = lax.broadcasted_iota(jnp.int32, (c, c), 1)
    cum = (ci <= ri).astype(F32)
    rr = lax.broadcasted_iota(jnp.int32, (n, n), 0)
    cc = lax.broadcasted_iota(jnp.int32, (n, n), 1)
    assert c & (c - 1) == 0
    strict = (cc & (c - 1)) < (rr & (c - 1))
    incl = (cc & (c - 1)) <= (rr & (c - 1))
    eye = (rr == cc).astype(F32)
    eye_w = (lax.broadcasted_iota(jnp.int32, (BLK, BLK), 0) == lax.broadcasted_iota(jnp.int32, (BLK, BLK), 1)).astype(F32)
    lane_head = lax.broadcasted_iota(jnp.int32, (c, BLK), 1) // HEAD_DIM

    def stack(x):
        return jnp.concatenate([jnp.where(lane_head == h, x, 0.0) for h in range(HEADS)], axis=0)

    def unstack(xw):
        out = xw[0:c]
        for h in range(1, HEADS):
            out = out + xw[h * c:(h + 1) * c]
        return out

    chunks = range(per_step)
    rows = [slice(s * c, (s + 1) * c) for s in chunks]
    lk, lr, rb, rk, vw, be_w, ke_w, w_c = [], [], [], [], [], [], [], []
    for s in chunks:
        logd = logd_ref[0, rows[s], :]
        logw = _dot(cum, logd, HIGHEST)
        logw_c = logw[c - 1:c, :]
        e_neg = jnp.exp(-logw)
        e_end = jnp.exp(logw_c - logw)
        b = b_ref[0, rows[s], :]
        k = k_ref[0, rows[s], :]
        lk.append(stack(kk_ref[0, rows[s], :] * jnp.exp(logw - logd)))
        lr.append(stack(r_ref[0, rows[s], :] * jnp.exp(logw)))
        rb.append(stack(b * e_neg))
        rk.append(stack(k * e_neg))
        be_w.append(stack(b * e_end))
        ke_w.append(stack(k * e_end))
        vw.append(stack(v_ref[0, rows[s], :]))
        w_c.append(jnp.exp(logw_c))

    lhs = [jnp.concatenate([lk[s], lr[s]], axis=0) for s in chunks]
    ab_mb = [_mm(lhs[s], rb[s], _NT, 1) for s in chunks]
    ak_mk = [_mm(lhs[s], rk[s], _NT, 1) for s in chunks]
    m_b = [jnp.where(incl, ab_mb[s][n:], 0.0) for s in chunks]
    am_k = [jnp.concatenate([jnp.where(strict, ak_mk[s][:n], 0.0), jnp.where(incl, ak_mk[s][n:], 0.0)], axis=0)
            for s in chunks]
    npow = [jnp.where(strict, -ab_mb[s][:n], 0.0) for s in chunks]
    tinv = [eye + npow[s] for s in chunks]
    for _ in range(int(math.log2(c)) - 1):
        npow = [_mm(npow[s], npow[s], _NN, 1) for s in chunks]
        tinv = [tinv[s] + _mm(tinv[s], npow[s], _NN, 1) for s in chunks]
    akv_mkv = [_mm(am_k[s], vw[s], _NN, 1) for s in chunks]
    pq = [_mm(tinv[s], jnp.concatenate([lk[s], akv_mkv[s][:n]], axis=1), _NN, 1) for s in chunks]
    mb_pq = [_mm(m_b[s], pq[s], _NN, 1) for s in chunks]
    for s in chunks:
        r2_ref[0, rows[s], :] = unstack(lr[s] - mb_pq[s][:, :BLK])
        y0_ref[0, rows[s], :] = unstack(akv_mkv[s][n:] - mb_pq[s][:, BLK:])
    for s in chunks:
        p_w, q_w = pq[s][:, :BLK], pq[s][:, BLK:]
        gt_ref[0, s] = eye_w * w_c[s] - _mm(be_w[s], p_w, _TN, 1)
        ht_ref[0, s] = _mm(ke_w[s], vw[s], _TN, 1) - _mm(be_w[s], q_w, _TN, 1)


def _chunks_per_step(nc):
    return math.gcd(nc, RWKV_CHUNKS_PER_STEP)


def _rwkv_chunks(r, k, v, logd, kk, b, chunk):
    bsz, t, _ = r.shape
    nc = t // chunk
    ps = _chunks_per_step(nc)
    tok = pl.BlockSpec((1, ps * chunk, BLK), lambda bi, c: (bi, c, 0))
    mat = pl.BlockSpec((1, ps, BLK, BLK), lambda bi, c: (bi, c, 0, 0))
    mat_shape = jax.ShapeDtypeStruct((bsz, nc, BLK, BLK), F32)
    return pl.pallas_call(
        functools.partial(_rwkv_chunk_kernel, chunk=chunk, per_step=ps),
        grid=(bsz, nc // ps),
        in_specs=[tok] * 6,
        out_specs=[tok, tok, mat, mat],
        out_shape=[jax.ShapeDtypeStruct((bsz, t, BLK), F32)] * 2 + [mat_shape] * 2,
        compiler_params=_cparams(("parallel", "parallel")),
        name="rwkv_chunks",
    )(r, k, v, logd, kk, b)


def _rwkv_scan_kernel(r2_ref, y0_ref, gt_ref, ht_ref, gate_ref, bonus_ref, lnw_ref, lnb_ref, s0_ref,
                      o_ref, st_ref, s_scr, *, chunk, per_step):
    step = pl.program_id(1)

    @pl.when(step == 0)
    def _():
        s_scr[...] = s0_ref[0]

    lnw = lnw_ref[...]
    lnb = lnb_ref[...]
    st = s_scr[...]
    for i in range(per_step):
        rows = slice(i * chunk, (i + 1) * chunk)
        y = _mm(r2_ref[0, rows, :], st, _NN, 1) + y0_ref[0, rows, :]
        st = _mm(gt_ref[0, i], st, _NN, 1) + ht_ref[0, i]
        for sl in _head_slices():
            yh = y[:, sl]
            mu = jnp.mean(yh, axis=-1, keepdims=True)
            d = yh - mu
            var = jnp.mean(d * d, axis=-1, keepdims=True)
            yn = d * lax.rsqrt(var + RWKV_GN_EPS) * lnw[:, sl] + lnb[:, sl]
            o_ref[0, rows, sl] = (yn + bonus_ref[0, rows, sl]) * gate_ref[0, rows, sl]
    s_scr[...] = st

    @pl.when(step == pl.num_programs(1) - 1)
    def _():
        st_ref[0] = st


def _rwkv_scan(r2, y0, gt, ht, gate, bonus, ln_w, ln_b, st0, chunk):
    bsz, t, _ = r2.shape
    nc = t // chunk
    ps = _chunks_per_step(nc)
    tok = pl.BlockSpec((1, ps * chunk, BLK), lambda bi, c: (bi, c, 0))
    mat = pl.BlockSpec((1, ps, BLK, BLK), lambda bi, c: (bi, c, 0, 0))
    state = pl.BlockSpec((1, BLK, BLK), lambda bi, c: (bi, 0, 0))
    row = pl.BlockSpec((1, BLK), lambda bi, c: (0, 0))
    return pl.pallas_call(
        functools.partial(_rwkv_scan_kernel, chunk=chunk, per_step=ps),
        grid=(bsz, nc // ps),
        in_specs=[tok, tok, mat, mat, tok, tok, row, row, state],
        out_specs=[tok, state],
        out_shape=[jax.ShapeDtypeStruct((bsz, t, BLK), F32), jax.ShapeDtypeStruct((bsz, BLK, BLK), F32)],
        scratch_shapes=[pltpu.VMEM((BLK, BLK), F32)],
        compiler_params=_cparams(("parallel", "arbitrary")),
        name="rwkv_scan",
    )(r2, y0, gt, ht, gate, bonus, ln_w.reshape(1, BLK), ln_b.reshape(1, BLK), st0)


def _rwkv(hr3, shift0, s0, n_valid, p, chunk, tm):
    bsz, t, _ = hr3.shape
    prev = jnp.concatenate([shift0[:, None, :], hr3[:, :-1]], axis=1)
    valid = jnp.broadcast_to((jnp.arange(t) < n_valid).astype(F32)[None, :, None], (bsz, t, 1))
    m = bsz * t
    outs = _rwkv_prep(hr3.reshape(m, RWKV_IN), prev.reshape(m, RWKV_IN), valid.reshape(m, 1),
                      p["mu"], p["w0"], p["a0"], p["k_k"], p["k_a"], p["r_k"], p["w_up"], p["a_up"], p["g_up"], tm)
    r, k, v, logd, kk, b, gate, bonus = [o.reshape(bsz, t, BLK) for o in outs]
    r2, y0, gt, ht = _rwkv_chunks(r, k, v, logd, kk, b, chunk)
    eye_h = jnp.eye(HEADS, dtype=F32)[None, :, None, :, None]
    st0 = (jnp.swapaxes(s0, 2, 3)[:, :, :, None, :] * eye_h).reshape(bsz, BLK, BLK)
    out, st = _rwkv_scan(r2, y0, gt, ht, gate, bonus, p["ln_w"], p["ln_b"], st0, chunk)
    s_new = jnp.swapaxes(jnp.sum(st.reshape(bsz, HEADS, HEAD_DIM, HEADS, HEAD_DIM) * eye_h, axis=3), 2, 3)
    return out, s_new


def _reorder_w_in(w):
    fox_end = 3 * BRANCH_W
    ff_end = fox_end + HEADS
    gate_start = w.shape[-1] - N_BRANCH * D_MODEL
    pad = jnp.zeros(w.shape[:-1] + (BRANCH_W - HEADS,), w.dtype)
    out = jnp.concatenate([w[..., gate_start:], w[..., :fox_end], w[..., ff_end:gate_start],
                           w[..., fox_end:ff_end], pad], axis=-1)
    assert out.shape[-1] == PROJ_COLS
    return out.astype(BF16)


def _heads4(x):
    return x.reshape(x.shape[:-1] + (HEADS, HEAD_DIM))


def kernel(x_prompt, x_sample, cache_fox_k, cache_fox_v, cache_fox_logf, cache_sb_k, cache_sb_v, state_ret, state_rwkv, state_rwkv_shift, page_table, w_in, fox_f_bias, ret_gn_w, rwkv_mu, rwkv_w0, rwkv_w_up, rwkv_a0, rwkv_a_up, rwkv_g_up, rwkv_k_k, rwkv_k_a, rwkv_r_k, rwkv_ln_w, rwkv_ln_b, w_branch, w_out, norm_ffn1, ffn1_w_in, ffn1_w_out, norm_mix, norm_ffn2, ffn2_w_in, ffn2_w_out, norm_final):
    depth = w_in.shape[0]
    bp, t, d = x_prompt.shape
    bs = x_sample.shape[0]
    n_pool = cache_fox_k.shape[1]
    past_len = page_table.shape[1] * PAGE_SIZE

    w_in_r = _reorder_w_in(w_in)
    f1i, f1o, f2i, f2o = (a.astype(BF16) for a in (ffn1_w_in, ffn1_w_out, ffn2_w_in, ffn2_w_out))
    wbr = w_branch.astype(BF16)
    wo = w_out.astype(BF16)
    w_up, a_up, g_up = rwkv_w_up.astype(BF16), rwkv_a_up.astype(BF16), rwkv_g_up.astype(BF16)

    pool = lambda c: jnp.transpose(c, (0, 1, 3, 4, 2)).reshape(depth * n_pool, BLK, PAGE_SIZE)
    ck_fox, cv_fox, ck_sb, cv_sb = pool(cache_fox_k), pool(cache_fox_v), pool(cache_sb_k), pool(cache_sb_v)
    lf_t = jnp.swapaxes(cache_fox_logf, 2, 3).reshape(depth * n_pool, HEADS, PAGE_SIZE)

    cos_p, sin_p = _rope_tables(jnp.arange(t))
    s_pad = 8
    cos_s, sin_s = _rope_tables(past_len + jnp.arange(s_pad))
    ret_tab_p = _ret_tables(RET_CHUNK, RET_CHUNK)
    ret_tab_s = _ret_tables(s_pad, 1)
    zeros_state = jnp.zeros((bp, HEADS, HEAD_DIM, HEAD_DIM), F32)

    xp = x_prompt.reshape(bp * t, d)
    xs = x_sample.reshape(bs, d)
    new_p = [[] for _ in range(8)]
    new_s = [[] for _ in range(8)]
    tm_p = 1024

    for l in range(depth):
        last = l == depth - 1
        rw = dict(mu=rwkv_mu[l], w0=rwkv_w0[l], a0=rwkv_a0[l], k_k=rwkv_k_k[l], k_a=rwkv_k_a[l], r_k=rwkv_r_k[l],
                  w_up=w_up[l], a_up=a_up[l], g_up=g_up[l], ln_w=rwkv_ln_w[l], ln_b=rwkv_ln_b[l])

        xp = _ffn(xp, norm_ffn1[l], f1i[l], f1o[l], norm_final, False, tm_p)
        h = _proj(xp, norm_mix[l], w_in_r[l], min(2 * tm_p, bp * t))
        h3 = h.reshape(bp, t, PROJ_COLS)
        ff_t = jnp.swapaxes(h3[:, :, COL_FORGET:COL_FORGET + HEADS], 1, 2).reshape(bp * HEADS, t)
        lf_rows, cum_rows = _forget(ff_t, jnp.tile(fox_f_bias[l], bp)[:, None])
        cum_k = cum_rows.reshape(bp, HEADS, t)
        lf = jnp.swapaxes(lf_rows.reshape(bp, HEADS, t), 1, 2)
        o_a = _fox_prompt(h3, jnp.swapaxes(cum_k, 1, 2), cum_k)
        o_b, ret_p = _retention(h3, COL_RET, cos_p, sin_p, ret_tab_p, ret_gn_w[l], zeros_state, RET_CHUNK)
        hr = h3[:, :, COL_RWKV:COL_RWKV + RWKV_IN]
        o_c, rw_p = _rwkv(hr, jnp.zeros((bp, RWKV_IN), F32), zeros_state, t, rw, RWKV_CHUNK, tm_p)
        o_d = _sb_prompt(h3)
        flat = lambda o: o.reshape(bp * t, BLK)
        xp = _merge(xp, [flat(o_a), flat(o_b), flat(o_c), flat(o_d)], h, wbr[l], wo[l], 512)
        xp = _ffn(xp, norm_ffn2[l], f2i[l], f2o[l], norm_final, last, tm_p)
        for i, a in enumerate((_heads4(h3[:, :, COL_FOX + BLK:COL_FOX + 2 * BLK]),
                               _heads4(h3[:, :, COL_FOX + 2 * BLK:COL_FOX + 3 * BLK]), lf,
                               _heads4(h3[:, :, COL_SB + BLK:COL_SB + 2 * BLK]),
                               _heads4(h3[:, :, COL_SB + 2 * BLK:COL_SB + 3 * BLK]), ret_p, rw_p, hr[:, -1])):
            new_p[i].append(a)

        xs = _ffn(xs, norm_ffn1[l], f1i[l], f1o[l], norm_final, False, bs)
        hs = _proj(xs, norm_mix[l], w_in_r[l], bs)
        col = lambda c0, n=BLK: hs[:, c0:c0 + n]
        ff_s = jnp.swapaxes(col(COL_FORGET, HEADS), 0, 1)
        ff_s = jnp.concatenate([ff_s, jnp.zeros((8 - HEADS, bs), F32)], axis=0)
        bias_s = jnp.concatenate([fox_f_bias[l], jnp.zeros((8 - HEADS,), F32)])[:, None]
        lf_s_rows, _ = _forget(jnp.pad(ff_s, ((0, 0), (0, 128 - bs))), bias_s)
        lf_s = jnp.swapaxes(lf_s_rows[:HEADS, :bs], 0, 1)
        o_a = _fox_sample(col(COL_FOX), col(COL_FOX + BLK), col(COL_FOX + 2 * BLK), lf_s,
                          ck_fox, cv_fox, lf_t, page_table, l * n_pool)
        hs_pad = jnp.pad(hs[:, None, :], ((0, 0), (0, s_pad - 1), (0, 0)))
        o_b, ret_s = _retention(hs_pad, COL_RET, cos_s, sin_s, ret_tab_s, ret_gn_w[l], state_ret[l], s_pad)
        hr_s = hs_pad[:, :, COL_RWKV:COL_RWKV + RWKV_IN]
        o_c, rw_s = _rwkv(hr_s, state_rwkv_shift[l], state_rwkv[l], 1, rw, s_pad, bs * s_pad)
        o_d = _sb_sample(col(COL_SB), ck_sb, cv_sb, page_table, l * n_pool)
        xs = _merge(xs, [o_a, o_b[:, 0], o_c[:, 0], o_d], hs, wbr[l], wo[l], bs)
        xs = _ffn(xs, norm_ffn2[l], f2i[l], f2o[l], norm_final, last, bs)
        for i, a in enumerate((_heads4(col(COL_FOX + BLK))[:, None], _heads4(col(COL_FOX + 2 * BLK))[:, None],
                               lf_s[:, None], _heads4(col(COL_SB + BLK))[:, None],
                               _heads4(col(COL_SB + 2 * BLK))[:, None], ret_s, rw_s, col(COL_RWKV, RWKV_IN))):
            new_s[i].append(a)

    sp = [jnp.stack(v) for v in new_p]
    ss = [jnp.stack(v) for v in new_s]
    return (xp.reshape(bp, t, d), xs.reshape(bs, 1, d), sp[0], sp[1], sp[2], sp[3], sp[4], sp[5], sp[6], sp[7],
            ss[0], ss[1], ss[2], ss[3], ss[4], ss[5], ss[6], ss[7])
```

```python
import functools
import math

import jax
import jax.numpy as jnp
import numpy as np
from jax import lax
from jax.experimental import pallas as pl
from jax.experimental.pallas import tpu as pltpu

F32 = jnp.float32
BF16 = jnp.bfloat16
HIGHEST = lax.Precision.HIGHEST

D_MODEL = 1024
HEADS = 4
HEAD_DIM = 64
BRANCH_W = HEADS * HEAD_DIM
N_BRANCH = 4
DECAY_LORA = 64
AAA_LORA = 64
GATE_LORA = 128
RWKV_IN = 3 * BRANCH_W + DECAY_LORA + AAA_LORA + GATE_LORA
PAGE_SIZE = 128
RET_CHUNK = 128
ROPE_BASE = 10000.0
RMS_EPS = 1e-6
GN_EPS = 1e-5
RWKV_GN_EPS = 64e-5
QK_SCALE = HEAD_DIM ** -0.5
NEG_BIG = -1e30

COL_GATE = 0
COL_FOX = N_BRANCH * D_MODEL
COL_RET = COL_FOX + 3 * BRANCH_W
COL_RWKV = COL_RET + 4 * BRANCH_W
COL_SB = COL_RWKV + RWKV_IN
COL_FORGET = COL_SB + 3 * BRANCH_W
PROJ_COLS = COL_FORGET + BRANCH_W
BLK = BRANCH_W

VMEM_LIMIT = 56 * 1024 * 1024
RWKV_CHUNK = 64
RWKV_CHUNKS_PER_STEP = 4


def _cparams(sem):
    return pltpu.CompilerParams(dimension_semantics=sem, vmem_limit_bytes=VMEM_LIMIT)


def _dot(a, b, precision=None):
    return jnp.dot(a, b, preferred_element_type=F32, precision=precision)


def _dot_nt(a, b, precision=None):
    return lax.dot_general(a, b, (((1,), (1,)), ((), ())), preferred_element_type=F32, precision=precision)


def _dot_tn(a, b, precision=None):
    return lax.dot_general(a, b, (((0,), (0,)), ((), ())), preferred_element_type=F32, precision=precision)


_NN = (((1,), (0,)), ((), ()))
_NT = (((1,), (1,)), ((), ()))
_TN = (((0,), (0,)), ((), ()))


def _mm(a, b, dims, passes):
    dg = lambda x, y: lax.dot_general(x, y, dims, preferred_element_type=F32)
    ah = a.astype(BF16)
    bh = b.astype(BF16)
    if passes == 1:
        return dg(ah, bh)
    al = (a - ah.astype(F32)).astype(BF16)
    bl = (b - bh.astype(F32)).astype(BF16)
    return dg(ah, bh) + (dg(ah, bl) + dg(al, bh))


PREC_PAIR = 1
PREC_INV = 1
PREC_APPLY = 1
PREC_SCAN = 1


def _rms(x, g):
    return x * lax.rsqrt(jnp.mean(x * x, axis=-1, keepdims=True) + RMS_EPS) * g


def _softplus_neg_abs(z):
    return jnp.log(1.0 + jnp.exp(-jnp.abs(z)))


def _head_slices():
    return [slice(h * HEAD_DIM, (h + 1) * HEAD_DIM) for h in range(HEADS)]


def _ffn_kernel(x_ref, g_ref, wa_ref, wb_ref, wo_ref, gf_ref, o_ref, xn_ref, acc_ref, *, final_norm):
    j = pl.program_id(1)

    @pl.when(j == 0)
    def _():
        xn_ref[...] = _rms(x_ref[...], g_ref[...]).astype(BF16)
        acc_ref[...] = jnp.zeros_like(acc_ref)

    xn = xn_ref[...]
    a = _dot(xn, wa_ref[...])
    b = _dot(xn, wb_ref[...])
    hmid = (a * jax.nn.sigmoid(a)) * b
    acc_ref[...] += _dot(hmid.astype(BF16), wo_ref[...])

    @pl.when(j == pl.num_programs(1) - 1)
    def _():
        y = x_ref[...] + 0.5 * acc_ref[...]
        if final_norm:
            o_ref[...] = _rms(y, gf_ref[...])
        else:
            o_ref[...] = y


def _ffn(x, g, wi, wo, gf, final_norm, tm, tf=256):
    m, d = x.shape
    f = wo.shape[0]
    nf = f // tf
    kern = functools.partial(_ffn_kernel, final_norm=final_norm)
    outs = pl.pallas_call(
        kern,
        grid=(m // tm, nf),
        in_specs=[
            pl.BlockSpec((tm, d), lambda i, j: (i, 0)),
            pl.BlockSpec((1, d), lambda i, j: (0, 0)),
            pl.BlockSpec((d, tf), lambda i, j: (0, j)),
            pl.BlockSpec((d, tf), lambda i, j: (0, j + nf)),
            pl.BlockSpec((tf, d), lambda i, j: (j, 0)),
            pl.BlockSpec((1, d), lambda i, j: (0, 0)),
        ],
        out_specs=pl.BlockSpec((tm, d), lambda i, j: (i, 0)),
        out_shape=jax.ShapeDtypeStruct((m, d), F32),
        scratch_shapes=[pltpu.VMEM((tm, d), BF16), pltpu.VMEM((tm, d), F32)],
        compiler_params=_cparams(("parallel", "arbitrary")),
        name="ffn",
    )(x, g.reshape(1, d), wi, wi, wo, gf.reshape(1, d))
    return outs


def _proj_kernel(x_ref, g_ref, w_ref, o_ref, xn_ref):
    @pl.when(pl.program_id(1) == 0)
    def _():
        xn_ref[...] = _rms(x_ref[...], g_ref[...]).astype(BF16)

    o_ref[...] = _dot(xn_ref[...], w_ref[...])


def _proj(x, g, w, tm, tn=256):
    m, d = x.shape
    n = w.shape[1]
    return pl.pallas_call(
        _proj_kernel,
        grid=(m // tm, n // tn),
        in_specs=[
            pl.BlockSpec((tm, d), lambda i, j: (i, 0)),
            pl.BlockSpec((1, d), lambda i, j: (0, 0)),
            pl.BlockSpec((d, tn), lambda i, j: (0, j)),
        ],
        out_specs=pl.BlockSpec((tm, tn), lambda i, j: (i, j)),
        out_shape=jax.ShapeDtypeStruct((m, n), F32),
        scratch_shapes=[pltpu.VMEM((tm, d), BF16)],
        compiler_params=_cparams(("parallel", "arbitrary")),
        name="proj",
    )(x, g.reshape(1, d), w)


def _merge_kernel(x_ref, oa_ref, ob_ref, oc_ref, od_ref, g0_ref, g1_ref, g2_ref, g3_ref, wb_ref, wo_ref, o_ref):
    merged = None
    for i, (o_r, g_r) in enumerate(((oa_ref, g0_ref), (ob_ref, g1_ref), (oc_ref, g2_ref), (od_ref, g3_ref))):
        t = jax.nn.sigmoid(g_r[...]) * _dot(o_r[...].astype(BF16), wb_ref[i])
        merged = t if merged is None else merged + t
    o_ref[...] = x_ref[...] + _dot(merged.astype(BF16), wo_ref[...])


def _merge(x, branches, h, wb, wo, tm):
    m, d = x.shape
    gate_specs = [pl.BlockSpec((tm, d), functools.partial(lambda i, k: (i, k), k=COL_GATE // d + k)) for k in range(N_BRANCH)]
    return pl.pallas_call(
        _merge_kernel,
        grid=(m // tm,),
        in_specs=[pl.BlockSpec((tm, d), lambda i: (i, 0))]
        + [pl.BlockSpec((tm, BRANCH_W), lambda i: (i, 0))] * N_BRANCH
        + gate_specs
        + [pl.BlockSpec((N_BRANCH, BRANCH_W, d), lambda i: (0, 0, 0)), pl.BlockSpec((d, d), lambda i: (0, 0))],
        out_specs=pl.BlockSpec((tm, d), lambda i: (i, 0)),
        out_shape=jax.ShapeDtypeStruct((m, d), F32),
        compiler_params=_cparams(("parallel",)),
        name="merge",
    )(x, *branches, h, h, h, h, wb, wo)


def _forget_kernel(ff_ref, bias_ref, lf_ref, hi_ref, mid_ref, lo_ref, *, t):
    x = ff_ref[...] + bias_ref[...]
    lf = jnp.minimum(x, 0.0) - _softplus_neg_abs(x)
    lf_ref[...] = lf
    w = 128
    tri = (lax.broadcasted_iota(jnp.int32, (w, w), 0) <= lax.broadcasted_iota(jnp.int32, (w, w), 1)).astype(F32)
    carry = jnp.zeros((x.shape[0], 1), F32)
    for c in range(t // w):
        cols = slice(c * w, (c + 1) * w)
        cs = _dot(lf[:, cols], tri, HIGHEST) + carry
        carry = cs[:, w - 1:w]
        hi = cs.astype(BF16)
        r1 = cs - hi.astype(F32)
        mid = r1.astype(BF16)
        hi_ref[:, cols] = hi
        mid_ref[:, cols] = mid
        lo_ref[:, cols] = (r1 - mid.astype(F32)).astype(BF16)


def _forget(ff_t, bias_rows):
    r, t = ff_t.shape
    term = jax.ShapeDtypeStruct((r, t), BF16)
    return pl.pallas_call(
        functools.partial(_forget_kernel, t=t),
        out_shape=(jax.ShapeDtypeStruct((r, t), F32), term, term, term),
        compiler_params=pltpu.CompilerParams(vmem_limit_bytes=VMEM_LIMIT),
        name="forget",
    )(ff_t, bias_rows)


PACK = 128


def _pack_heads(x, extra):
    b, t, _ = x.shape
    xh = x.reshape(b, t, HEADS, HEAD_DIM).astype(BF16)
    parts = [xh] if extra is None else [xh, extra]
    used = sum(p.shape[-1] for p in parts)
    parts.append(jnp.zeros((b, t, HEADS, PACK - used), BF16))
    return jnp.concatenate(parts, axis=-1).reshape(b, t, HEADS * PACK)


def _fox_operands(q, k, v, terms):
    b, t = terms.shape[:2]
    ones3 = jnp.ones((b, t, HEADS, 3), BF16)
    qa = _pack_heads(q * QK_SCALE, jnp.concatenate([terms, ones3], axis=-1))
    ka = _pack_heads(k, jnp.concatenate([ones3, -terms], axis=-1))
    va = _pack_heads(v, jnp.ones((b, t, HEADS, 1), BF16))
    return qa, ka, va


def _causal_pairs(nq, keys_descending):
    qs, ks = [], []
    for qi in range(nq):
        order = range(qi, -1, -1) if keys_descending else range(qi + 1)
        for ki in order:
            qs.append(qi)
            ks.append(ki)
    return jnp.asarray(np.array(qs, np.int32)), jnp.asarray(np.array(ks, np.int32))


def _fox_kernel(qtab_ref, ktab_ref, q_ref, k_ref, v_ref, o_ref, m_scr, acc_scr, *, tq):
    qi = qtab_ref[pl.program_id(1)]
    ki = ktab_ref[pl.program_id(1)]
    heads = range(HEADS)
    lanes = [slice(h * PACK, (h + 1) * PACK) for h in heads]

    @pl.when(ki == 0)
    def _():
        m_scr[...] = jnp.full_like(m_scr, NEG_BIG)
        acc_scr[...] = jnp.zeros_like(acc_scr)

    def block(diagonal):
        s = [_dot_nt(q_ref[0, :, lanes[h]], k_ref[0, :, lanes[h]]) for h in heads]
        if diagonal:
            mask = lax.broadcasted_iota(jnp.int32, (tq, tq), 1) <= lax.broadcasted_iota(jnp.int32, (tq, tq), 0)
            s = [jnp.where(mask, x, NEG_BIG) for x in s]
        m_old = [m_scr[h] for h in heads]
        m_new = [jnp.maximum(m_old[h], jnp.max(s[h], axis=-1, keepdims=True)) for h in heads]
        p = [jnp.exp(s[h] - m_new[h]).astype(BF16) for h in heads]
        pv = [_dot(p[h], v_ref[0, :, lanes[h]]) for h in heads]
        for h in heads:
            acc_scr[h] = jnp.exp(m_old[h] - m_new[h]) * acc_scr[h] + pv[h]
            m_scr[h] = m_new[h]

    pl.when(ki < qi)(functools.partial(block, False))

    @pl.when(ki == qi)
    def _():
        block(True)
        for h, sl in enumerate(_head_slices()):
            acc = acc_scr[h]
            o_ref[0, :, sl] = acc[:, :HEAD_DIM] / acc[:, HEAD_DIM:HEAD_DIM + 1]


def _pair_call(kern, name, qa, ka, va, tq, keys_descending, scratch):
    b, t, w = qa.shape
    qtab, ktab = _causal_pairs(t // tq, keys_descending)
    grid_spec = pltpu.PrefetchScalarGridSpec(
        num_scalar_prefetch=2,
        grid=(b, qtab.shape[0]),
        in_specs=[
            pl.BlockSpec((1, tq, w), lambda bi, p, qt, kt: (bi, qt[p], 0)),
            pl.BlockSpec((1, tq, w), lambda bi, p, qt, kt: (bi, kt[p], 0)),
            pl.BlockSpec((1, tq, w), lambda bi, p, qt, kt: (bi, kt[p], 0)),
        ],
        out_specs=pl.BlockSpec((1, tq, BLK), lambda bi, p, qt, kt: (bi, qt[p], 0)),
        scratch_shapes=scratch,
    )
    return pl.pallas_call(
        functools.partial(kern, tq=tq),
        grid_spec=grid_spec,
        out_shape=jax.ShapeDtypeStruct((b, t, BLK), F32),
        compiler_params=_cparams(("parallel", "arbitrary")),
        name=name,
    )(qtab, ktab, qa, ka, va)


def _fox_prompt(qa, ka, va, tq=512):
    scratch = [pltpu.VMEM((HEADS, tq, 1), F32), pltpu.VMEM((HEADS, tq, PACK), F32)]
    return _pair_call(_fox_kernel, "fox_prompt", qa, ka, va, tq, False, scratch)


def _strict_upper(n):
    return lax.broadcasted_iota(jnp.int32, (n, n), 0) > lax.broadcasted_iota(jnp.int32, (n, n), 1)


def _suffix_excl(x, u_bf16):
    hi = x.astype(BF16)
    lo = (x - hi.astype(F32)).astype(BF16)
    return _dot(hi, u_bf16) + _dot(lo, u_bf16)


def _sb_kernel(qtab_ref, ktab_ref, q_ref, k_ref, v_ref, o_ref, r_scr, acc_scr, *, tq):
    qi = qtab_ref[pl.program_id(1)]
    ki = ktab_ref[pl.program_id(1)]

    @pl.when(ki == qi)
    def _():
        r_scr[...] = jnp.zeros_like(r_scr)
        acc_scr[...] = jnp.zeros_like(acc_scr)

    heads = range(HEADS)
    lanes = [slice(h * PACK, (h + 1) * PACK) for h in heads]

    def block(diagonal):
        u = _strict_upper(tq).astype(BF16)
        z = [_dot_nt(q_ref[0, :, lanes[h]], k_ref[0, :, lanes[h]]) for h in heads]
        sp = [_softplus_neg_abs(x) for x in z]
        ls = [jnp.minimum(z[h], 0.0) - sp[h] for h in heads]
        l1m = [-jnp.maximum(z[h], 0.0) - sp[h] for h in heads]
        if diagonal:
            mask = lax.broadcasted_iota(jnp.int32, (tq, tq), 1) < lax.broadcasted_iota(jnp.int32, (tq, tq), 0)
            l1m = [jnp.where(mask, x, 0.0) for x in l1m]
        excl = [_suffix_excl(x, u) for x in l1m]
        a = [jnp.exp(ls[h] + excl[h] + r_scr[h]) for h in heads]
        if diagonal:
            a = [jnp.where(mask, x, 0.0) for x in a]
        av = [_dot(a[h].astype(BF16), v_ref[0, :, lanes[h]]) for h in heads]
        for h in heads:
            acc_scr[h] += av[h]
            r_scr[h] += jnp.sum(l1m[h], axis=-1, keepdims=True)

    pl.when(ki == qi)(functools.partial(block, True))
    pl.when(ki < qi)(functools.partial(block, False))

    @pl.when(ki == 0)
    def _():
        for h, sl in enumerate(_head_slices()):
            o_ref[0, :, sl] = acc_scr[h][:, :HEAD_DIM]


def _sb_prompt(qa, ka, va, tq=256):
    scratch = [pltpu.VMEM((HEADS, tq, 1), F32), pltpu.VMEM((HEADS, tq, PACK), F32)]
    return _pair_call(_sb_kernel, "sb_prompt", qa, ka, va, tq, True, scratch)


GROUP = 8
MAX_SAMPLE_PAGES_PER_STEP = 16


def _page_scores(k_refs, qx_scr, s_scr):
    for j, k_ref in enumerate(k_refs):
        for h, sl in enumerate(_head_slices()):
            s_scr[j * GROUP + h:j * GROUP + h + 1, :] = jnp.sum(k_ref[0, sl, :] * qx_scr[sl, :], axis=0, keepdims=True)


def _page_carries(r, tot, pp):
    carries = []
    for j in range(pp):
        carries.append(r)
        r = r + tot[j * GROUP:(j + 1) * GROUP]
    return jnp.concatenate(carries, axis=0), r


def _weighted_values(w_scr, v_refs, acc_scr, scale):
    for h, sl in enumerate(_head_slices()):
        acc = acc_scr[sl, :] if scale is None else acc_scr[sl, :] * scale[h:h + 1, :]
        for j, v_ref in enumerate(v_refs):
            acc = acc + w_scr[j * GROUP + h:j * GROUP + h + 1, :] * v_ref[0, sl, :]
        acc_scr[sl, :] = acc


def _fox_sample_kernel(pt_ref, q_ref, kn_ref, vn_ref, lfn_ref, *refs, pp):
    k_refs, v_refs, lf_refs = refs[:pp], refs[pp:2 * pp], refs[2 * pp:3 * pp]
    o_ref, qx_scr, s_scr, lf_scr, p_scr, m_scr, l_scr, r_scr, acc_scr = refs[3 * pp:]
    step = pl.program_id(1)

    @pl.when(step == 0)
    def _():
        qcol = q_ref[0] * QK_SCALE
        qx_scr[...] = jnp.broadcast_to(qcol, qx_scr.shape)
        s_scr[...] = jnp.zeros_like(s_scr)
        lf_scr[...] = jnp.zeros_like(lf_scr)
        m_scr[...] = jnp.zeros_like(m_scr)
        own = qcol * kn_ref[0]
        for h, sl in enumerate(_head_slices()):
            m_scr[h:h + 1, :] = jnp.sum(own[sl], axis=0, keepdims=True)
        l_scr[...] = jnp.ones_like(l_scr)
        r_scr[...] = lfn_ref[0]
        lane = lax.broadcasted_iota(jnp.int32, acc_scr.shape, 1)
        acc_scr[...] = jnp.where(lane == 0, vn_ref[0], 0.0)

    for j, lf_ref in enumerate(lf_refs):
        lf_scr[j * GROUP:j * GROUP + HEADS, :] = lf_ref[0]
    _page_scores(k_refs, qx_scr, s_scr)
    lf_all = lf_scr[...]
    excl = _dot(lf_all, _strict_upper(PAGE_SIZE).astype(F32), HIGHEST)
    carries, r_out = _page_carries(r_scr[...], jnp.sum(lf_all, axis=-1, keepdims=True), pp)
    r_scr[...] = r_out
    s_all = s_scr[...] + excl + carries
    row_max = jnp.max(s_all, axis=-1, keepdims=True)
    m_old = m_scr[...]
    m_new = m_old
    for j in range(pp):
        m_new = jnp.maximum(m_new, row_max[j * GROUP:(j + 1) * GROUP])
    alpha = jnp.exp(m_old - m_new)
    p_all = jnp.exp(s_all - jnp.concatenate([m_new] * pp, axis=0))
    p_scr[...] = p_all
    row_sum = jnp.sum(p_all, axis=-1, keepdims=True)
    l_new = alpha * l_scr[...]
    for j in range(pp):
        l_new = l_new + row_sum[j * GROUP:(j + 1) * GROUP]
    l_scr[...] = l_new
    m_scr[...] = m_new
    _weighted_values(p_scr, v_refs, acc_scr, alpha)

    @pl.when(step == pl.num_programs(1) - 1)
    def _():
        l = l_scr[...]
        for h, sl in enumerate(_head_slices()):
            o_ref[0, sl, :] = jnp.sum(acc_scr[sl, :], axis=-1, keepdims=True) / l[h:h + 1, :]


def _sb_sample_kernel(pt_ref, q_ref, *refs, pp):
    k_refs, v_refs = refs[:pp], refs[pp:2 * pp]
    o_ref, qx_scr, z_scr, a_scr, r_scr, acc_scr = refs[2 * pp:]
    step = pl.program_id(1)

    @pl.when(step == 0)
    def _():
        qx_scr[...] = jnp.broadcast_to(q_ref[0] * QK_SCALE, qx_scr.shape)
        z_scr[...] = jnp.zeros_like(z_scr)
        r_scr[...] = jnp.zeros_like(r_scr)
        acc_scr[...] = jnp.zeros_like(acc_scr)

    _page_scores(k_refs, qx_scr, z_scr)
    z = z_scr[...]
    sp = _softplus_neg_abs(z)
    ls = jnp.minimum(z, 0.0) - sp
    l1m = -jnp.maximum(z, 0.0) - sp
    excl = _suffix_excl(l1m, _strict_upper(PAGE_SIZE).astype(BF16))
    carries, r_out = _page_carries(r_scr[...], jnp.sum(l1m, axis=-1, keepdims=True), pp)
    r_scr[...] = r_out
    a_scr[...] = jnp.exp(ls + excl + carries)
    _weighted_values(a_scr, v_refs, acc_scr, None)

    @pl.when(step == pl.num_programs(1) - 1)
    def _():
        o_ref[...] = jnp.sum(acc_scr[...], axis=-1, keepdims=True)[None]


def _pages_per_step(n_pages):
    pp = math.gcd(n_pages, MAX_SAMPLE_PAGES_PER_STEP)
    return pp


def _page_specs(page_table, base, pp, block):
    n_pages = page_table.shape[1]

    def spec(j):
        return pl.BlockSpec(block, lambda bi, st, pt: (base + pt[bi, n_pages - 1 - (st * pp + j)], 0, 0))

    return [spec(j) for j in range(pp)]


def _per_sample(bi, st, pt):
    return (bi, 0, 0)


def _fox_sample(q, k_new, v_new, lf_new, cache_kt, cache_vt, cache_lft, page_table, base):
    b, n_pages = page_table.shape
    pp = _pages_per_step(n_pages)
    col = lambda x: x[:, :, None]
    lfn = col(jnp.concatenate([lf_new, jnp.zeros((b, GROUP - HEADS), F32)], axis=1))
    kv_block = (1, BLK, PAGE_SIZE)
    grid_spec = pltpu.PrefetchScalarGridSpec(
        num_scalar_prefetch=1,
        grid=(b, n_pages // pp),
        in_specs=[pl.BlockSpec((1, BLK, 1), _per_sample)] * 3 + [pl.BlockSpec((1, GROUP, 1), _per_sample)]
        + _page_specs(page_table, base, pp, kv_block) + _page_specs(page_table, base, pp, kv_block)
        + _page_specs(page_table, base, pp, (1, HEADS, PAGE_SIZE)),
        out_specs=pl.BlockSpec((1, BLK, 1), _per_sample),
        scratch_shapes=[pltpu.VMEM((BLK, PAGE_SIZE), F32)] + [pltpu.VMEM((pp * GROUP, PAGE_SIZE), F32)] * 3
        + [pltpu.VMEM((GROUP, 1), F32)] * 3 + [pltpu.VMEM((BLK, PAGE_SIZE), F32)],
    )
    out = pl.pallas_call(
        functools.partial(_fox_sample_kernel, pp=pp),
        grid_spec=grid_spec,
        out_shape=jax.ShapeDtypeStruct((b, BLK, 1), F32),
        compiler_params=_cparams(("parallel", "arbitrary")),
        name="fox_sample",
    )(page_table, col(q), col(k_new), col(v_new), lfn, *([cache_kt] * pp), *([cache_vt] * pp), *([cache_lft] * pp))
    return out[:, :, 0]


def _sb_sample(q, cache_kt, cache_vt, page_table, base):
    b, n_pages = page_table.shape
    pp = _pages_per_step(n_pages)
    kv_block = (1, BLK, PAGE_SIZE)
    grid_spec = pltpu.PrefetchScalarGridSpec(
        num_scalar_prefetch=1,
        grid=(b, n_pages // pp),
        in_specs=[pl.BlockSpec((1, BLK, 1), _per_sample)]
        + _page_specs(page_table, base, pp, kv_block) + _page_specs(page_table, base, pp, kv_block),
        out_specs=pl.BlockSpec((1, BLK, 1), _per_sample),
        scratch_shapes=[pltpu.VMEM((BLK, PAGE_SIZE), F32)] + [pltpu.VMEM((pp * GROUP, PAGE_SIZE), F32)] * 2
        + [pltpu.VMEM((GROUP, 1), F32), pltpu.VMEM((BLK, PAGE_SIZE), F32)],
    )
    out = pl.pallas_call(
        functools.partial(_sb_sample_kernel, pp=pp),
        grid_spec=grid_spec,
        out_shape=jax.ShapeDtypeStruct((b, BLK, 1), F32),
        compiler_params=_cparams(("parallel", "arbitrary")),
        name="sb_sample",
    )(page_table, q[:, :, None], *([cache_kt] * pp), *([cache_vt] * pp))
    return out[:, :, 0]


def _rot_half(x):
    half = HEAD_DIM // 2
    lane = lax.broadcasted_iota(jnp.int32, x.shape, 1) % HEAD_DIM
    n = x.shape[1]
    return jnp.where(lane < half, -pltpu.roll(x, n - half, 1), pltpu.roll(x, half, 1))


def _ret_kernel(q_ref, k_ref, v_ref, g_ref, cos_ref, sin_ref, dmask_ref, xi_ref, zeta_ref, cd_ref, gnw_ref, s0_ref,
                o_ref, st_ref, s_scr):
    c = pl.program_id(1)

    @pl.when(c == 0)
    def _():
        s_scr[...] = s0_ref[0]

    cos = cos_ref[...]
    sin = sin_ref[...]
    q = q_ref[0]
    k = k_ref[0]
    qr = (q * cos + _rot_half(q) * sin).astype(BF16)
    kr = (k * cos + _rot_half(k) * sin) * QK_SCALE
    v = v_ref[0].astype(BF16)
    g = g_ref[0]
    gate = g * jax.nn.sigmoid(g)
    gnw = gnw_ref[...]
    heads = range(HEADS)
    sls = _head_slices()
    qh = [qr[:, sl] for sl in sls]
    kh = [kr[:, sl] for sl in sls]
    vh = [v[:, sl] for sl in sls]
    s = [s_scr[h] for h in heads]
    att = [(_dot_nt(qh[h], kh[h].astype(BF16)) * dmask_ref[h]).astype(BF16) for h in heads]
    cross = [_dot(qh[h], s[h].astype(BF16)) * xi_ref[h] for h in heads]
    o = [_dot(att[h], vh[h]) + cross[h] for h in heads]
    kv = [_dot_tn((kh[h] * zeta_ref[h]).astype(BF16), vh[h]) for h in heads]
    for h in heads:
        s_scr[h] = cd_ref[h] * s[h] + kv[h]
    mu = [jnp.mean(x, axis=-1, keepdims=True) for x in o]
    d = [o[h] - mu[h] for h in heads]
    var = [jnp.mean(x * x, axis=-1, keepdims=True) for x in d]
    for h, sl in enumerate(sls):
        o_ref[0, :, sl] = gate[:, sl] * (d[h] * lax.rsqrt(var[h] + GN_EPS) * gnw[:, sl])

    @pl.when(c == pl.num_programs(1) - 1)
    def _():
        st_ref[0] = s_scr[...]


def _ret_tables(chunk, n_valid):
    lg = np.log(1.0 - 2.0 ** (-5.0 - np.arange(HEADS, dtype=np.float64)))
    n = np.arange(chunk, dtype=np.float64)
    diff = n[:, None] - n[None, :]
    valid = (n < n_valid).astype(np.float64)
    dmask = np.where(diff[None] >= 0, np.exp(np.maximum(diff, 0.0)[None] * lg[:, None, None]), 0.0)
    dmask = dmask * valid[None, :, None] * valid[None, None, :]
    xi = np.exp((n[None, :] + 1.0) * lg[:, None]) * valid[None, :]
    zeta = np.exp((n_valid - 1.0 - n)[None, :] * lg[:, None]) * valid[None, :]
    cd = np.exp(n_valid * lg)
    f = lambda a: jnp.asarray(a.astype(np.float32))
    return f(dmask), f(xi[:, :, None]), f(zeta[:, :, None]), f(cd[:, None, None])


def _rope_tables(pos):
    half = HEAD_DIM // 2
    inv = ROPE_BASE ** (-jnp.arange(half, dtype=F32) / half)
    ang = pos.astype(F32)[:, None] * inv[None, :]
    cos = jnp.tile(jnp.cos(ang), (1, 2 * HEADS))
    sin = jnp.tile(jnp.sin(ang), (1, 2 * HEADS))
    return cos, sin


def _retention(h3, col0, cos, sin, tables, gnw, s0, chunk):
    b, t, _ = h3.shape
    nc = t // chunk
    cb = col0 // BLK
    dmask, xi, zeta, cd = tables

    def tok(k):
        return pl.BlockSpec((1, chunk, BLK), functools.partial(lambda bi, c, k: (bi, c, k), k=cb + k))

    const3 = lambda bi, c: (0, 0, 0)
    return pl.pallas_call(
        _ret_kernel,
        grid=(b, nc),
        in_specs=[tok(0), tok(1), tok(2), tok(3),
                  pl.BlockSpec((chunk, BLK), lambda bi, c: (c, 0)),
                  pl.BlockSpec((chunk, BLK), lambda bi, c: (c, 0)),
                  pl.BlockSpec((HEADS, chunk, chunk), const3),
                  pl.BlockSpec((HEADS, chunk, 1), const3),
                  pl.BlockSpec((HEADS, chunk, 1), const3),
                  pl.BlockSpec((HEADS, 1, 1), const3),
                  pl.BlockSpec((1, BLK), lambda bi, c: (0, 0)),
                  pl.BlockSpec((1, HEADS, HEAD_DIM, HEAD_DIM), lambda bi, c: (bi, 0, 0, 0))],
        out_specs=[pl.BlockSpec((1, chunk, BLK), lambda bi, c: (bi, c, 0)),
                   pl.BlockSpec((1, HEADS, HEAD_DIM, HEAD_DIM), lambda bi, c: (bi, 0, 0, 0))],
        out_shape=[jax.ShapeDtypeStruct((b, t, BLK), F32),
                   jax.ShapeDtypeStruct((b, HEADS, HEAD_DIM, HEAD_DIM), F32)],
        scratch_shapes=[pltpu.VMEM((HEADS, HEAD_DIM, HEAD_DIM), F32)],
        compiler_params=_cparams(("parallel", "arbitrary")),
        name="retention",
    )(h3, h3, h3, h3, cos, sin, dmask, xi, zeta, cd, gnw.reshape(1, BLK), s0)


def _head_sum_matrix():
    i = lax.broadcasted_iota(jnp.int32, (BLK, BLK), 0) // HEAD_DIM
    j = lax.broadcasted_iota(jnp.int32, (BLK, BLK), 1) // HEAD_DIM
    return (i == j).astype(F32)


def _rwkv_prep_kernel(h0_ref, h1_ref, h2_ref, h3_ref, c0_ref, c1_ref, c2_ref, c3_ref, shift_ref,
                      mu_ref, w0_ref, a0_ref, kk_w_ref, ka_ref, rk_ref, wup_ref, aup_ref, gup_ref,
                      r_ref, k_ref, v_ref, logd_ref, kk_ref, b_ref, g_ref, bonus_ref, *, tm, seq_len, n_valid):
    hr = jnp.concatenate([h0_ref[...], h1_ref[...], h2_ref[...], h3_ref[...]], axis=1)
    before = jnp.concatenate([c0_ref[...], c1_ref[...], c2_ref[...], c3_ref[...]], axis=1)[7:8]
    row = lax.broadcasted_iota(jnp.int32, (tm, 1), 0)
    pos = (pl.program_id(0) * tm + row) & (seq_len - 1)
    prev = jnp.where(row == 0, before, pltpu.roll(hr, 1, 0))
    prev = jnp.where(pos == 0, shift_ref[0], prev)
    valid = (pos < n_valid).astype(F32)
    xm = hr + (prev - hr) * mu_ref[...]
    r = xm[:, 0:BLK]
    k = xm[:, BLK:2 * BLK]
    v = xm[:, 2 * BLK:3 * BLK]
    o = 3 * BLK
    wd = xm[:, o:o + DECAY_LORA]
    ad = xm[:, o + DECAY_LORA:o + DECAY_LORA + AAA_LORA]
    gd = xm[:, o + DECAY_LORA + AAA_LORA:]
    wpre = w0_ref[...] + _dot(jnp.tanh(wd).astype(BF16), wup_ref[...])
    w = jnp.minimum(wpre, 0.0) - _softplus_neg_abs(wpre) - 0.5
    a = jax.nn.sigmoid(a0_ref[...] + _dot(ad.astype(BF16), aup_ref[...]))
    g = _dot(jax.nn.sigmoid(gd).astype(BF16), gup_ref[...])
    hs = _head_sum_matrix()
    kk = k * kk_w_ref[...]
    kk = kk * lax.rsqrt(_dot(kk * kk, hs, HIGHEST) + 1e-12)
    kmod = k * (1.0 + (a - 1.0) * ka_ref[...])
    bonus = _dot(r * kmod * rk_ref[...], hs, HIGHEST) * v
    r_ref[...] = r * valid
    k_ref[...] = kmod * valid
    v_ref[...] = v * valid
    logd_ref[...] = -jnp.exp(w) * valid
    kk_ref[...] = kk * valid
    b_ref[...] = kk * a * valid
    g_ref[...] = g
    bonus_ref[...] = bonus


def _rwkv_prep(h, shift0, seq_len, n_valid, mu, w0, a0, k_k, k_a, r_k, w_up, a_up, g_up, tm):
    m = h.shape[0]
    assert seq_len & (seq_len - 1) == 0 and seq_len % tm == 0 and tm % 8 == 0
    cb = COL_RWKV // BLK
    row = lambda n: pl.BlockSpec((1, n), lambda i: (0, 0))
    full = lambda a: pl.BlockSpec(a.shape, lambda i: (0, 0))
    tokb = pl.BlockSpec((tm, BLK), lambda i: (i, 0))
    cur = [pl.BlockSpec((tm, BLK), functools.partial(lambda i, c: (i, c), c=cb + c)) for c in range(4)]
    before = [pl.BlockSpec((8, BLK), functools.partial(lambda i, c: (jnp.maximum(i * (tm // 8) - 1, 0), c), c=cb + c))
              for c in range(4)]
    return pl.pallas_call(
        functools.partial(_rwkv_prep_kernel, tm=tm, seq_len=seq_len, n_valid=n_valid),
        grid=(m // tm,),
        in_specs=cur + before + [pl.BlockSpec((1, 1, RWKV_IN), lambda i: (i * tm // seq_len, 0, 0)),
                                 row(RWKV_IN), row(BLK), row(BLK), row(BLK), row(BLK), row(BLK),
                                 full(w_up), full(a_up), full(g_up)],
        out_specs=[tokb] * 8,
        out_shape=[jax.ShapeDtypeStruct((m, BLK), F32)] * 8,
        compiler_params=_cparams(("parallel",)),
        name="rwkv_prep",
    )(h, h, h, h, h, h, h, h, shift0[:, None, :], mu.reshape(1, -1), w0.reshape(1, -1), a0.reshape(1, -1),
      k_k.reshape(1, -1), k_a.reshape(1, -1), r_k.reshape(1, -1), w_up, a_up, g_up)


def _rwkv_chunk_kernel(r_ref, k_ref, v_ref, logd_ref, kk_ref, b_ref, r2_ref, y0_ref, gt_ref, ht_ref, *, chunk, per_step):
    c = chunk
    n = HEADS * c
    ri = lax.broadcasted_iota(jnp.int32, (c, c), 0)
    ci = lax.broadcasted_iota(jnp.int32, (c, c), 1)
    cum = (ci <= ri).astype(F32)
    rr = lax.broadcasted_iota(jnp.int32, (n, n), 0)
    cc = lax.broadcasted_iota(jnp.int32, (n, n), 1)
    assert c & (c - 1) == 0
    strict = (cc & (c - 1)) < (rr & (c - 1))
    incl = (cc & (c - 1)) <= (rr & (c - 1))
    eye = (rr == cc).astype(F32)
    eye_w = (lax.broadcasted_iota(jnp.int32, (BLK, BLK), 0) == lax.broadcasted_iota(jnp.int32, (BLK, BLK), 1)).astype(F32)
    lane_head = lax.broadcasted_iota(jnp.int32, (c, BLK), 1) // HEAD_DIM

    def stack(x):
        return jnp.concatenate([jnp.where(lane_head == h, x, 0.0) for h in range(HEADS)], axis=0)

    def unstack(xw):
        out = xw[0:c]
        for h in range(1, HEADS):
            out = out + xw[h * c:(h + 1) * c]
        return out

    chunks = range(per_step)
    rows = [slice(s * c, (s + 1) * c) for s in chunks]
    lk, lr, rb, rk, vw, be_w, ke_w, w_c = [], [], [], [], [], [], [], []
    for s in chunks:
        logd = logd_ref[0, rows[s], :]
        logw = _dot(cum, logd, HIGHEST)
        logw_c = logw[c - 1:c, :]
        e_neg = jnp.exp(-logw)
        e_end = jnp.exp(logw_c - logw)
        b = b_ref[0, rows[s], :]
        k = k_ref[0, rows[s], :]
        lk.append(stack(kk_ref[0, rows[s], :] * jnp.exp(logw - logd)))
        lr.append(stack(r_ref[0, rows[s], :] * jnp.exp(logw)))
        rb.append(stack(b * e_neg))
        rk.append(stack(k * e_neg))
        be_w.append(stack(b * e_end))
        ke_w.append(stack(k * e_end))
        vw.append(stack(v_ref[0, rows[s], :]))
        w_c.append(jnp.exp(logw_c))

    lhs = [jnp.concatenate([lk[s], lr[s]], axis=0) for s in chunks]
    ab_mb = [_mm(lhs[s], rb[s], _NT, 1) for s in chunks]
    ak_mk = [_mm(lhs[s], rk[s], _NT, 1) for s in chunks]
    m_b = [jnp.where(incl, ab_mb[s][n:], 0.0) for s in chunks]
    am_k = [jnp.concatenate([jnp.where(strict, ak_mk[s][:n], 0.0), jnp.where(incl, ak_mk[s][n:], 0.0)], axis=0)
            for s in chunks]
    npow = [jnp.where(strict, -ab_mb[s][:n], 0.0) for s in chunks]
    tinv = [eye + npow[s] for s in chunks]
    for _ in range(int(math.log2(c)) - 1):
        npow = [_mm(npow[s], npow[s], _NN, 1) for s in chunks]
        tinv = [tinv[s] + _mm(tinv[s], npow[s], _NN, 1) for s in chunks]
    akv_mkv = [_mm(am_k[s], vw[s], _NN, 1) for s in chunks]
    pq = [_mm(tinv[s], jnp.concatenate([lk[s], akv_mkv[s][:n]], axis=1), _NN, 1) for s in chunks]
    mb_pq = [_mm(m_b[s], pq[s], _NN, 1) for s in chunks]
    for s in chunks:
        r2_ref[0, rows[s], :] = unstack(lr[s] - mb_pq[s][:, :BLK])
        y0_ref[0, rows[s], :] = unstack(akv_mkv[s][n:] - mb_pq[s][:, BLK:])
    for s in chunks:
        p_w, q_w = pq[s][:, :BLK], pq[s][:, BLK:]
        gt_ref[0, s] = eye_w * w_c[s] - _mm(be_w[s], p_w, _TN, 1)
        ht_ref[0, s] = _mm(ke_w[s], vw[s], _TN, 1) - _mm(be_w[s], q_w, _TN, 1)


def _chunks_per_step(nc):
    return math.gcd(nc, RWKV_CHUNKS_PER_STEP)


def _rwkv_chunks(r, k, v, logd, kk, b, chunk):
    bsz, t, _ = r.shape
    nc = t // chunk
    ps = _chunks_per_step(nc)
    tok = pl.BlockSpec((1, ps * chunk, BLK), lambda bi, c: (bi, c, 0))
    mat = pl.BlockSpec((1, ps, BLK, BLK), lambda bi, c: (bi, c, 0, 0))
    mat_shape = jax.ShapeDtypeStruct((bsz, nc, BLK, BLK), F32)
    return pl.pallas_call(
        functools.partial(_rwkv_chunk_kernel, chunk=chunk, per_step=ps),
        grid=(bsz, nc // ps),
        in_specs=[tok] * 6,
        out_specs=[tok, tok, mat, mat],
        out_shape=[jax.ShapeDtypeStruct((bsz, t, BLK), F32)] * 2 + [mat_shape] * 2,
        compiler_params=_cparams(("parallel", "parallel")),
        name="rwkv_chunks",
    )(r, k, v, logd, kk, b)


def _rwkv_scan_kernel(r2_ref, y0_ref, gt_ref, ht_ref, gate_ref, bonus_ref, lnw_ref, lnb_ref, s0_ref,
                      o_ref, st_ref, s_scr, *, chunk, per_step):
    step = pl.program_id(1)

    @pl.when(step == 0)
    def _():
        s_scr[...] = jnp.zeros_like(s_scr)
        for h, sl in enumerate(_head_slices()):
            s_scr[sl, sl] = s0_ref[0, h]

    lnw = lnw_ref[...]
    lnb = lnb_ref[...]
    st = s_scr[...]
    for i in range(per_step):
        rows = slice(i * chunk, (i + 1) * chunk)
        y = _mm(r2_ref[0, rows, :], st, _NN, 1) + y0_ref[0, rows, :]
        st = _mm(gt_ref[0, i], st, _NN, 1) + ht_ref[0, i]
        for sl in _head_slices():
            yh = y[:, sl]
            mu = jnp.mean(yh, axis=-1, keepdims=True)
            d = yh - mu
            var = jnp.mean(d * d, axis=-1, keepdims=True)
            yn = d * lax.rsqrt(var + RWKV_GN_EPS) * lnw[:, sl] + lnb[:, sl]
            o_ref[0, rows, sl] = (yn + bonus_ref[0, rows, sl]) * gate_ref[0, rows, sl]
    s_scr[...] = st

    @pl.when(step == pl.num_programs(1) - 1)
    def _():
        for h, sl in enumerate(_head_slices()):
            st_ref[0, h] = st[sl, sl]


def _rwkv_scan(r2, y0, gt, ht, gate, bonus, ln_w, ln_b, st0, chunk):
    bsz, t, _ = r2.shape
    nc = t // chunk
    ps = _chunks_per_step(nc)
    tok = pl.BlockSpec((1, ps * chunk, BLK), lambda bi, c: (bi, c, 0))
    mat = pl.BlockSpec((1, ps, BLK, BLK), lambda bi, c: (bi, c, 0, 0))
    state = pl.BlockSpec((1, HEADS, HEAD_DIM, HEAD_DIM), lambda bi, c: (bi, 0, 0, 0))
    row = pl.BlockSpec((1, BLK), lambda bi, c: (0, 0))
    return pl.pallas_call(
        functools.partial(_rwkv_scan_kernel, chunk=chunk, per_step=ps),
        grid=(bsz, nc // ps),
        in_specs=[tok, tok, mat, mat, tok, tok, row, row, state],
        out_specs=[tok, state],
        out_shape=[jax.ShapeDtypeStruct((bsz, t, BLK), F32),
                   jax.ShapeDtypeStruct((bsz, HEADS, HEAD_DIM, HEAD_DIM), F32)],
        scratch_shapes=[pltpu.VMEM((BLK, BLK), F32)],
        compiler_params=_cparams(("parallel", "arbitrary")),
        name="rwkv_scan",
    )(r2, y0, gt, ht, gate, bonus, ln_w.reshape(1, BLK), ln_b.reshape(1, BLK), st0)


def _rwkv(h3, shift0, s0, n_valid, p, chunk, tm):
    bsz, t, cols = h3.shape
    outs = _rwkv_prep(h3.reshape(bsz * t, cols), shift0, t, n_valid,
                      p["mu"], p["w0"], p["a0"], p["k_k"], p["k_a"], p["r_k"], p["w_up"], p["a_up"], p["g_up"], tm)
    r, k, v, logd, kk, b, gate, bonus = [o.reshape(bsz, t, BLK) for o in outs]
    r2, y0, gt, ht = _rwkv_chunks(r, k, v, logd, kk, b, chunk)
    out, st = _rwkv_scan(r2, y0, gt, ht, gate, bonus, p["ln_w"], p["ln_b"], jnp.swapaxes(s0, 2, 3), chunk)
    return out, jnp.swapaxes(st, 2, 3)


def _reorder_w_in(w):
    fox_end = 3 * BRANCH_W
    ff_end = fox_end + HEADS
    gate_start = w.shape[-1] - N_BRANCH * D_MODEL
    pad = jnp.zeros(w.shape[:-1] + (BRANCH_W - HEADS,), w.dtype)
    out = jnp.concatenate([w[..., gate_start:], w[..., :fox_end], w[..., ff_end:gate_start],
                           w[..., fox_end:ff_end], pad], axis=-1)
    assert out.shape[-1] == PROJ_COLS
    return out.astype(BF16)


def _heads4(x):
    return x.reshape(x.shape[:-1] + (HEADS, HEAD_DIM))


def kernel(x_prompt, x_sample, cache_fox_k, cache_fox_v, cache_fox_logf, cache_sb_k, cache_sb_v, state_ret, state_rwkv, state_rwkv_shift, page_table, w_in, fox_f_bias, ret_gn_w, rwkv_mu, rwkv_w0, rwkv_w_up, rwkv_a0, rwkv_a_up, rwkv_g_up, rwkv_k_k, rwkv_k_a, rwkv_r_k, rwkv_ln_w, rwkv_ln_b, w_branch, w_out, norm_ffn1, ffn1_w_in, ffn1_w_out, norm_mix, norm_ffn2, ffn2_w_in, ffn2_w_out, norm_final):
    depth = w_in.shape[0]
    bp, t, d = x_prompt.shape
    bs = x_sample.shape[0]
    n_pool = cache_fox_k.shape[1]
    past_len = page_table.shape[1] * PAGE_SIZE

    w_in_r = _reorder_w_in(w_in)
    f1i, f1o, f2i, f2o = (a.astype(BF16) for a in (ffn1_w_in, ffn1_w_out, ffn2_w_in, ffn2_w_out))
    wbr = w_branch.astype(BF16)
    wo = w_out.astype(BF16)
    w_up, a_up, g_up = rwkv_w_up.astype(BF16), rwkv_a_up.astype(BF16), rwkv_g_up.astype(BF16)

    pool = lambda c: jnp.transpose(c, (0, 1, 3, 4, 2)).reshape(depth * n_pool, BLK, PAGE_SIZE)
    ck_fox, cv_fox, ck_sb, cv_sb = pool(cache_fox_k), pool(cache_fox_v), pool(cache_sb_k), pool(cache_sb_v)
    lf_t = jnp.swapaxes(cache_fox_logf, 2, 3).reshape(depth * n_pool, HEADS, PAGE_SIZE)

    cos_p, sin_p = _rope_tables(jnp.arange(t))
    s_pad = 8
    cos_s, sin_s = _rope_tables(past_len + jnp.arange(s_pad))
    ret_tab_p = _ret_tables(RET_CHUNK, RET_CHUNK)
    ret_tab_s = _ret_tables(s_pad, 1)
    zeros_state = jnp.zeros((bp, HEADS, HEAD_DIM, HEAD_DIM), F32)

    xp = x_prompt.reshape(bp * t, d)
    xs = x_sample.reshape(bs, d)
    new_p = [[] for _ in range(8)]
    new_s = [[] for _ in range(8)]
    tm_p = 1024

    for l in range(depth):
        last = l == depth - 1
        rw = dict(mu=rwkv_mu[l], w0=rwkv_w0[l], a0=rwkv_a0[l], k_k=rwkv_k_k[l], k_a=rwkv_k_a[l], r_k=rwkv_r_k[l],
                  w_up=w_up[l], a_up=a_up[l], g_up=g_up[l], ln_w=rwkv_ln_w[l], ln_b=rwkv_ln_b[l])

        xp = _ffn(xp, norm_ffn1[l], f1i[l], f1o[l], norm_final, False, tm_p)
        h = _proj(xp, norm_mix[l], w_in_r[l], min(2 * tm_p, bp * t))
        h3 = h.reshape(bp, t, PROJ_COLS)
        ff_t = jnp.swapaxes(h3[:, :, COL_FORGET:COL_FORGET + HEADS], 1, 2).reshape(bp * HEADS, t)
        lf_rows, *cum_terms = _forget(ff_t, jnp.tile(fox_f_bias[l], bp)[:, None])
        by_token = lambda rows: jnp.swapaxes(rows.reshape(bp, HEADS, t), 1, 2)
        lf = by_token(lf_rows)
        terms = jnp.stack([by_token(c) for c in cum_terms], axis=-1)
        blk = lambda c0: h3[:, :, c0:c0 + BLK]
        o_a = _fox_prompt(*_fox_operands(blk(COL_FOX), blk(COL_FOX + BLK), blk(COL_FOX + 2 * BLK), terms))
        o_b, ret_p = _retention(h3, COL_RET, cos_p, sin_p, ret_tab_p, ret_gn_w[l], zeros_state, RET_CHUNK)
        o_c, rw_p = _rwkv(h3, jnp.zeros((bp, RWKV_IN), F32), zeros_state, t, rw, RWKV_CHUNK, tm_p)
        o_d = _sb_prompt(_pack_heads(blk(COL_SB) * QK_SCALE, None), _pack_heads(blk(COL_SB + BLK), None),
                         _pack_heads(blk(COL_SB + 2 * BLK), None))
        flat = lambda o: o.reshape(bp * t, BLK)
        xp = _merge(xp, [flat(o_a), flat(o_b), flat(o_c), flat(o_d)], h, wbr[l], wo[l], 512)
        xp = _ffn(xp, norm_ffn2[l], f2i[l], f2o[l], norm_final, last, tm_p)
        for i, a in enumerate((_heads4(h3[:, :, COL_FOX + BLK:COL_FOX + 2 * BLK]),
                               _heads4(h3[:, :, COL_FOX + 2 * BLK:COL_FOX + 3 * BLK]), lf,
                               _heads4(h3[:, :, COL_SB + BLK:COL_SB + 2 * BLK]),
                               _heads4(h3[:, :, COL_SB + 2 * BLK:COL_SB + 3 * BLK]), ret_p, rw_p,
                               h3[:, -1, COL_RWKV:COL_RWKV + RWKV_IN])):
            new_p[i].append(a)

        xs = _ffn(xs, norm_ffn1[l], f1i[l], f1o[l], norm_final, False, bs)
        hs = _proj(xs, norm_mix[l], w_in_r[l], bs)
        col = lambda c0, n=BLK: hs[:, c0:c0 + n]
        ff_s = jnp.swapaxes(col(COL_FORGET, HEADS), 0, 1)
        ff_s = jnp.concatenate([ff_s, jnp.zeros((8 - HEADS, bs), F32)], axis=0)
        bias_s = jnp.concatenate([fox_f_bias[l], jnp.zeros((8 - HEADS,), F32)])[:, None]
        lf_s_rows = _forget(jnp.pad(ff_s, ((0, 0), (0, 128 - bs))), bias_s)[0]
        lf_s = jnp.swapaxes(lf_s_rows[:HEADS, :bs], 0, 1)
        o_a = _fox_sample(col(COL_FOX), col(COL_FOX + BLK), col(COL_FOX + 2 * BLK), lf_s,
                          ck_fox, cv_fox, lf_t, page_table, l * n_pool)
        hs_pad = jnp.pad(hs[:, None, :], ((0, 0), (0, s_pad - 1), (0, 0)))
        o_b, ret_s = _retention(hs_pad, COL_RET, cos_s, sin_s, ret_tab_s, ret_gn_w[l], state_ret[l], s_pad)
        o_c, rw_s = _rwkv(hs_pad, state_rwkv_shift[l], state_rwkv[l], 1, rw, s_pad, s_pad)
        o_d = _sb_sample(col(COL_SB), ck_sb, cv_sb, page_table, l * n_pool)
        xs = _merge(xs, [o_a, o_b[:, 0], o_c[:, 0], o_d], hs, wbr[l], wo[l], bs)
        xs = _ffn(xs, norm_ffn2[l], f2i[l], f2o[l], norm_final, last, bs)
        for i, a in enumerate((_heads4(col(COL_FOX + BLK))[:, None], _heads4(col(COL_FOX + 2 * BLK))[:, None],
                               lf_s[:, None], _heads4(col(COL_SB + BLK))[:, None],
                               _heads4(col(COL_SB + 2 * BLK))[:, None], ret_s, rw_s, col(COL_RWKV, RWKV_IN))):
            new_s[i].append(a)

    sp = [jnp.stack(v) for v in new_p]
    ss = [jnp.stack(v) for v in new_s]
    return (xp.reshape(bp, t, d), xs.reshape(bs, 1, d), sp[0], sp[1], sp[2], sp[3], sp[4], sp[5], sp[6], sp[7],
            ss[0], ss[1], ss[2], ss[3], ss[4], ss[5], ss[6], ss[7])
```

```python
import functools
import math

import jax
import jax.numpy as jnp
import numpy as np
from jax import lax
from jax.experimental import pallas as pl
from jax.experimental.pallas import tpu as pltpu

F32 = jnp.float32
BF16 = jnp.bfloat16
HIGHEST = lax.Precision.HIGHEST

D_MODEL = 1024
HEADS = 4
HEAD_DIM = 64
BRANCH_W = HEADS * HEAD_DIM
N_BRANCH = 4
DECAY_LORA = 64
AAA_LORA = 64
GATE_LORA = 128
RWKV_IN = 3 * BRANCH_W + DECAY_LORA + AAA_LORA + GATE_LORA
PAGE_SIZE = 128
RET_CHUNK = 128
ROPE_BASE = 10000.0
RMS_EPS = 1e-6
GN_EPS = 1e-5
RWKV_GN_EPS = 64e-5
QK_SCALE = HEAD_DIM ** -0.5
NEG_BIG = -1e30

COL_GATE = 0
COL_FOX = N_BRANCH * D_MODEL
COL_RET = COL_FOX + 3 * BRANCH_W
COL_RWKV = COL_RET + 4 * BRANCH_W
COL_SB = COL_RWKV + RWKV_IN
COL_FORGET = COL_SB + 3 * BRANCH_W
PROJ_COLS = COL_FORGET + BRANCH_W
BLK = BRANCH_W

VMEM_LIMIT = 56 * 1024 * 1024
RWKV_CHUNK = 64
RWKV_CHUNKS_PER_STEP = 4


def _cparams(sem):
    return pltpu.CompilerParams(dimension_semantics=sem, vmem_limit_bytes=VMEM_LIMIT)


def _dot(a, b, precision=None):
    return jnp.dot(a, b, preferred_element_type=F32, precision=precision)


def _dot_nt(a, b, precision=None):
    return lax.dot_general(a, b, (((1,), (1,)), ((), ())), preferred_element_type=F32, precision=precision)


def _dot_tn(a, b, precision=None):
    return lax.dot_general(a, b, (((0,), (0,)), ((), ())), preferred_element_type=F32, precision=precision)


_NN = (((1,), (0,)), ((), ()))
_NT = (((1,), (1,)), ((), ()))
_TN = (((0,), (0,)), ((), ()))


def _mm(a, b, dims, passes):
    dg = lambda x, y: lax.dot_general(x, y, dims, preferred_element_type=F32)
    ah = a.astype(BF16)
    bh = b.astype(BF16)
    if passes == 1:
        return dg(ah, bh)
    al = (a - ah.astype(F32)).astype(BF16)
    bl = (b - bh.astype(F32)).astype(BF16)
    return dg(ah, bh) + (dg(ah, bl) + dg(al, bh))


PREC_PAIR = 1
PREC_INV = 1
PREC_APPLY = 1
PREC_SCAN = 1


def _rms(x, g):
    return x * lax.rsqrt(jnp.mean(x * x, axis=-1, keepdims=True) + RMS_EPS) * g


def _softplus_neg_abs(z):
    return jnp.log(1.0 + jnp.exp(-jnp.abs(z)))


def _head_slices():
    return [slice(h * HEAD_DIM, (h + 1) * HEAD_DIM) for h in range(HEADS)]


def _ffn_kernel(x_ref, g_ref, wa_ref, wb_ref, wo_ref, gf_ref, o_ref, xn_ref, acc_ref, *, final_norm):
    j = pl.program_id(1)

    @pl.when(j == 0)
    def _():
        xn_ref[...] = _rms(x_ref[...], g_ref[...]).astype(BF16)
        acc_ref[...] = jnp.zeros_like(acc_ref)

    xn = xn_ref[...]
    a = _dot(xn, wa_ref[...])
    b = _dot(xn, wb_ref[...])
    hmid = (a * jax.nn.sigmoid(a)) * b
    acc_ref[...] += _dot(hmid.astype(BF16), wo_ref[...])

    @pl.when(j == pl.num_programs(1) - 1)
    def _():
        y = x_ref[...] + 0.5 * acc_ref[...]
        if final_norm:
            o_ref[...] = _rms(y, gf_ref[...])
        else:
            o_ref[...] = y


def _ffn(x, g, wi, wo, gf, final_norm, tm, tf=256):
    m, d = x.shape
    f = wo.shape[0]
    nf = f // tf
    kern = functools.partial(_ffn_kernel, final_norm=final_norm)
    outs = pl.pallas_call(
        kern,
        grid=(m // tm, nf),
        in_specs=[
            pl.BlockSpec((tm, d), lambda i, j: (i, 0)),
            pl.BlockSpec((1, d), lambda i, j: (0, 0)),
            pl.BlockSpec((d, tf), lambda i, j: (0, j)),
            pl.BlockSpec((d, tf), lambda i, j: (0, j + nf)),
            pl.BlockSpec((tf, d), lambda i, j: (j, 0)),
            pl.BlockSpec((1, d), lambda i, j: (0, 0)),
        ],
        out_specs=pl.BlockSpec((tm, d), lambda i, j: (i, 0)),
        out_shape=jax.ShapeDtypeStruct((m, d), F32),
        scratch_shapes=[pltpu.VMEM((tm, d), BF16), pltpu.VMEM((tm, d), F32)],
        compiler_params=_cparams(("parallel", "arbitrary")),
        name="ffn",
    )(x, g.reshape(1, d), wi, wi, wo, gf.reshape(1, d))
    return outs


def _proj_kernel(x_ref, g_ref, w_ref, o_ref, xn_ref):
    @pl.when(pl.program_id(1) == 0)
    def _():
        xn_ref[...] = _rms(x_ref[...], g_ref[...]).astype(BF16)

    o_ref[...] = _dot(xn_ref[...], w_ref[...])


def _proj(x, g, w, tm, tn=256):
    m, d = x.shape
    n = w.shape[1]
    return pl.pallas_call(
        _proj_kernel,
        grid=(m // tm, n // tn),
        in_specs=[
            pl.BlockSpec((tm, d), lambda i, j: (i, 0)),
            pl.BlockSpec((1, d), lambda i, j: (0, 0)),
            pl.BlockSpec((d, tn), lambda i, j: (0, j)),
        ],
        out_specs=pl.BlockSpec((tm, tn), lambda i, j: (i, j)),
        out_shape=jax.ShapeDtypeStruct((m, n), F32),
        scratch_shapes=[pltpu.VMEM((tm, d), BF16)],
        compiler_params=_cparams(("parallel", "arbitrary")),
        name="proj",
    )(x, g.reshape(1, d), w)


def _merge_kernel(x_ref, oa_ref, ob_ref, oc_ref, od_ref, g0_ref, g1_ref, g2_ref, g3_ref, wb_ref, wo_ref, o_ref):
    merged = None
    for i, (o_r, g_r) in enumerate(((oa_ref, g0_ref), (ob_ref, g1_ref), (oc_ref, g2_ref), (od_ref, g3_ref))):
        t = jax.nn.sigmoid(g_r[...]) * _dot(o_r[...].astype(BF16), wb_ref[i])
        merged = t if merged is None else merged + t
    o_ref[...] = x_ref[...] + _dot(merged.astype(BF16), wo_ref[...])


def _merge(x, branches, h, wb, wo, tm):
    m, d = x.shape
    gate_specs = [pl.BlockSpec((tm, d), functools.partial(lambda i, k: (i, k), k=COL_GATE // d + k)) for k in range(N_BRANCH)]
    return pl.pallas_call(
        _merge_kernel,
        grid=(m // tm,),
        in_specs=[pl.BlockSpec((tm, d), lambda i: (i, 0))]
        + [pl.BlockSpec((tm, BRANCH_W), lambda i: (i, 0))] * N_BRANCH
        + gate_specs
        + [pl.BlockSpec((N_BRANCH, BRANCH_W, d), lambda i: (0, 0, 0)), pl.BlockSpec((d, d), lambda i: (0, 0))],
        out_specs=pl.BlockSpec((tm, d), lambda i: (i, 0)),
        out_shape=jax.ShapeDtypeStruct((m, d), F32),
        compiler_params=_cparams(("parallel",)),
        name="merge",
    )(x, *branches, h, h, h, h, wb, wo)


def _forget_kernel(ff_ref, bias_ref, lf_ref, hi_ref, mid_ref, lo_ref, *, t):
    x = ff_ref[...] + bias_ref[...]
    lf = jnp.minimum(x, 0.0) - _softplus_neg_abs(x)
    lf_ref[...] = lf
    w = 128
    tri = (lax.broadcasted_iota(jnp.int32, (w, w), 0) <= lax.broadcasted_iota(jnp.int32, (w, w), 1)).astype(F32)
    carry = jnp.zeros((x.shape[0], 1), F32)
    for c in range(t // w):
        cols = slice(c * w, (c + 1) * w)
        cs = _dot(lf[:, cols], tri, HIGHEST) + carry
        carry = cs[:, w - 1:w]
        hi = cs.astype(BF16)
        r1 = cs - hi.astype(F32)
        mid = r1.astype(BF16)
        hi_ref[:, cols] = hi
        mid_ref[:, cols] = mid
        lo_ref[:, cols] = (r1 - mid.astype(F32)).astype(BF16)


def _forget(ff_t, bias_rows):
    r, t = ff_t.shape
    term = jax.ShapeDtypeStruct((r, t), BF16)
    return pl.pallas_call(
        functools.partial(_forget_kernel, t=t),
        out_shape=(jax.ShapeDtypeStruct((r, t), F32), term, term, term),
        compiler_params=pltpu.CompilerParams(vmem_limit_bytes=VMEM_LIMIT),
        name="forget",
    )(ff_t, bias_rows)


PACK = 128


TERM_COLS = 16


def _pack_kernel(fq_ref, fk_ref, fv_ref, sq_ref, sk_ref, sv_ref, terms_ref,
                 fqa_ref, fka_ref, fva_ref, sqa_ref, ska_ref, sva_ref):
    w = HEADS * PACK
    r = lax.broadcasted_iota(jnp.int32, (BLK, w), 0)
    c = lax.broadcasted_iota(jnp.int32, (BLK, w), 1)
    place = ((c // PACK == r // HEAD_DIM) & (c % PACK == r % HEAD_DIM)).astype(BF16)
    e = lax.broadcasted_iota(jnp.int32, (TERM_COLS, w), 0)
    ce = lax.broadcasted_iota(jnp.int32, (TERM_COLS, w), 1)
    term_lane = (e % HEADS) * PACK + HEAD_DIM + e // HEADS
    real = e < 3 * HEADS
    place_q = ((ce == term_lane) & real).astype(BF16)
    place_k = -((ce == term_lane + 3) & real).astype(BF16)
    lane = lax.broadcasted_iota(jnp.int32, (1, w), 1) % PACK
    ones_q = ((lane >= HEAD_DIM + 3) & (lane < HEAD_DIM + 6)).astype(F32)
    ones_k = ((lane >= HEAD_DIM) & (lane < HEAD_DIM + 3)).astype(F32)
    ones_v = (lane == HEAD_DIM).astype(F32)
    terms = terms_ref[...]

    def put(x):
        return _dot(x.astype(BF16), place)

    fqa_ref[...] = (put(fq_ref[...] * QK_SCALE) + _dot(terms, place_q) + ones_q).astype(BF16)
    fka_ref[...] = (put(fk_ref[...]) + _dot(terms, place_k) + ones_k).astype(BF16)
    fva_ref[...] = (put(fv_ref[...]) + ones_v).astype(BF16)
    sqa_ref[...] = put(sq_ref[...] * QK_SCALE).astype(BF16)
    ska_ref[...] = put(sk_ref[...]).astype(BF16)
    sva_ref[...] = put(sv_ref[...]).astype(BF16)


def _pack_operands(h, terms, tm):
    m = h.shape[0]
    w = HEADS * PACK
    colblk = lambda c0: pl.BlockSpec((tm, BLK), functools.partial(lambda i, c: (i, c), c=c0 // BLK))
    cols = [COL_FOX, COL_FOX + BLK, COL_FOX + 2 * BLK, COL_SB, COL_SB + BLK, COL_SB + 2 * BLK]
    return pl.pallas_call(
        _pack_kernel,
        grid=(m // tm,),
        in_specs=[colblk(c0) for c0 in cols] + [pl.BlockSpec((tm, TERM_COLS), lambda i: (i, 0))],
        out_specs=[pl.BlockSpec((tm, w), lambda i: (i, 0))] * 6,
        out_shape=[jax.ShapeDtypeStruct((m, w), BF16)] * 6,
        compiler_params=_cparams(("parallel",)),
        name="pack",
    )(h, h, h, h, h, h, terms)


def _causal_pairs(nq, keys_descending):
    qs, ks = [], []
    for qi in range(nq):
        order = range(qi, -1, -1) if keys_descending else range(qi + 1)
        for ki in order:
            qs.append(qi)
            ks.append(ki)
    return jnp.asarray(np.array(qs, np.int32)), jnp.asarray(np.array(ks, np.int32))


def _fox_kernel(qtab_ref, ktab_ref, q_ref, k_ref, v_ref, o_ref, m_scr, acc_scr, *, tq):
    qi = qtab_ref[pl.program_id(1)]
    ki = ktab_ref[pl.program_id(1)]
    heads = range(HEADS)
    lanes = [slice(h * PACK, (h + 1) * PACK) for h in heads]

    @pl.when(ki == 0)
    def _():
        m_scr[...] = jnp.full_like(m_scr, NEG_BIG)
        acc_scr[...] = jnp.zeros_like(acc_scr)

    def block(diagonal):
        s = [_dot_nt(q_ref[0, :, lanes[h]], k_ref[0, :, lanes[h]]) for h in heads]
        if diagonal:
            mask = lax.broadcasted_iota(jnp.int32, (tq, tq), 1) <= lax.broadcasted_iota(jnp.int32, (tq, tq), 0)
            s = [jnp.where(mask, x, NEG_BIG) for x in s]
        m_old = [m_scr[h] for h in heads]
        m_new = [jnp.maximum(m_old[h], jnp.max(s[h], axis=-1, keepdims=True)) for h in heads]
        p = [jnp.exp(s[h] - m_new[h]).astype(BF16) for h in heads]
        pv = [_dot(p[h], v_ref[0, :, lanes[h]]) for h in heads]
        for h in heads:
            acc_scr[h] = jnp.exp(m_old[h] - m_new[h]) * acc_scr[h] + pv[h]
            m_scr[h] = m_new[h]

    pl.when(ki < qi)(functools.partial(block, False))

    @pl.when(ki == qi)
    def _():
        block(True)
        for h, sl in enumerate(_head_slices()):
            acc = acc_scr[h]
            o_ref[0, :, sl] = acc[:, :HEAD_DIM] / acc[:, HEAD_DIM:HEAD_DIM + 1]


def _pair_call(kern, name, qa, ka, va, tq, keys_descending, scratch):
    b, t, w = qa.shape
    qtab, ktab = _causal_pairs(t // tq, keys_descending)
    grid_spec = pltpu.PrefetchScalarGridSpec(
        num_scalar_prefetch=2,
        grid=(b, qtab.shape[0]),
        in_specs=[
            pl.BlockSpec((1, tq, w), lambda bi, p, qt, kt: (bi, qt[p], 0)),
            pl.BlockSpec((1, tq, w), lambda bi, p, qt, kt: (bi, kt[p], 0)),
            pl.BlockSpec((1, tq, w), lambda bi, p, qt, kt: (bi, kt[p], 0)),
        ],
        out_specs=pl.BlockSpec((1, tq, BLK), lambda bi, p, qt, kt: (bi, qt[p], 0)),
        scratch_shapes=scratch,
    )
    return pl.pallas_call(
        functools.partial(kern, tq=tq),
        grid_spec=grid_spec,
        out_shape=jax.ShapeDtypeStruct((b, t, BLK), F32),
        compiler_params=_cparams(("parallel", "arbitrary")),
        name=name,
    )(qtab, ktab, qa, ka, va)


def _fox_prompt(qa, ka, va, tq=512):
    scratch = [pltpu.VMEM((HEADS, tq, 1), F32), pltpu.VMEM((HEADS, tq, PACK), F32)]
    return _pair_call(_fox_kernel, "fox_prompt", qa, ka, va, tq, False, scratch)


def _strict_upper(n):
    return lax.broadcasted_iota(jnp.int32, (n, n), 0) > lax.broadcasted_iota(jnp.int32, (n, n), 1)


def _suffix_excl(x, u_bf16):
    hi = x.astype(BF16)
    lo = (x - hi.astype(F32)).astype(BF16)
    return _dot(hi, u_bf16) + _dot(lo, u_bf16)


def _sb_kernel(qtab_ref, ktab_ref, q_ref, k_ref, v_ref, o_ref, r_scr, acc_scr, *, tq):
    qi = qtab_ref[pl.program_id(1)]
    ki = ktab_ref[pl.program_id(1)]

    @pl.when(ki == qi)
    def _():
        r_scr[...] = jnp.zeros_like(r_scr)
        acc_scr[...] = jnp.zeros_like(acc_scr)

    heads = range(HEADS)
    lanes = [slice(h * PACK, (h + 1) * PACK) for h in heads]

    def block(diagonal):
        u = _strict_upper(tq).astype(BF16)
        z = [_dot_nt(q_ref[0, :, lanes[h]], k_ref[0, :, lanes[h]]) for h in heads]
        sp = [_softplus_neg_abs(x) for x in z]
        ls = [jnp.minimum(z[h], 0.0) - sp[h] for h in heads]
        l1m = [-jnp.maximum(z[h], 0.0) - sp[h] for h in heads]
        if diagonal:
            mask = lax.broadcasted_iota(jnp.int32, (tq, tq), 1) < lax.broadcasted_iota(jnp.int32, (tq, tq), 0)
            l1m = [jnp.where(mask, x, 0.0) for x in l1m]
        excl = [_suffix_excl(x, u) for x in l1m]
        a = [jnp.exp(ls[h] + excl[h] + r_scr[h]) for h in heads]
        if diagonal:
            a = [jnp.where(mask, x, 0.0) for x in a]
        av = [_dot(a[h].astype(BF16), v_ref[0, :, lanes[h]]) for h in heads]
        for h in heads:
            acc_scr[h] += av[h]
            r_scr[h] += jnp.sum(l1m[h], axis=-1, keepdims=True)

    pl.when(ki == qi)(functools.partial(block, True))
    pl.when(ki < qi)(functools.partial(block, False))

    @pl.when(ki == 0)
    def _():
        for h, sl in enumerate(_head_slices()):
            o_ref[0, :, sl] = acc_scr[h][:, :HEAD_DIM]


def _sb_prompt(qa, ka, va, tq=256):
    scratch = [pltpu.VMEM((HEADS, tq, 1), F32), pltpu.VMEM((HEADS, tq, PACK), F32)]
    return _pair_call(_sb_kernel, "sb_prompt", qa, ka, va, tq, True, scratch)


GROUP = 8
MAX_SAMPLE_PAGES_PER_STEP = 16


def _page_scores(k_refs, qx_scr, s_scr):
    for j, k_ref in enumerate(k_refs):
        for h, sl in enumerate(_head_slices()):
            s_scr[j * GROUP + h:j * GROUP + h + 1, :] = jnp.sum(k_ref[0, sl, :] * qx_scr[sl, :], axis=0, keepdims=True)


def _page_carries(r, tot, pp):
    carries = []
    for j in range(pp):
        carries.append(r)
        r = r + tot[j * GROUP:(j + 1) * GROUP]
    return jnp.concatenate(carries, axis=0), r


def _weighted_values(w_scr, v_refs, acc_scr, scale):
    for h, sl in enumerate(_head_slices()):
        acc = acc_scr[sl, :] if scale is None else acc_scr[sl, :] * scale[h:h + 1, :]
        for j, v_ref in enumerate(v_refs):
            acc = acc + w_scr[j * GROUP + h:j * GROUP + h + 1, :] * v_ref[0, sl, :]
        acc_scr[sl, :] = acc


def _fox_sample_kernel(pt_ref, q_ref, kn_ref, vn_ref, lfn_ref, *refs, pp):
    k_refs, v_refs, lf_refs = refs[:pp], refs[pp:2 * pp], refs[2 * pp:3 * pp]
    o_ref, qx_scr, s_scr, lf_scr, p_scr, m_scr, l_scr, r_scr, acc_scr = refs[3 * pp:]
    step = pl.program_id(1)

    @pl.when(step == 0)
    def _():
        qcol = q_ref[0] * QK_SCALE
        qx_scr[...] = jnp.broadcast_to(qcol, qx_scr.shape)
        s_scr[...] = jnp.zeros_like(s_scr)
        lf_scr[...] = jnp.zeros_like(lf_scr)
        m_scr[...] = jnp.zeros_like(m_scr)
        own = qcol * kn_ref[0]
        for h, sl in enumerate(_head_slices()):
            m_scr[h:h + 1, :] = jnp.sum(own[sl], axis=0, keepdims=True)
        l_scr[...] = jnp.ones_like(l_scr)
        r_scr[...] = lfn_ref[0]
        lane = lax.broadcasted_iota(jnp.int32, acc_scr.shape, 1)
        acc_scr[...] = jnp.where(lane == 0, vn_ref[0], 0.0)

    for j, lf_ref in enumerate(lf_refs):
        lf_scr[j * GROUP:j * GROUP + HEADS, :] = lf_ref[0]
    _page_scores(k_refs, qx_scr, s_scr)
    lf_all = lf_scr[...]
    excl = _dot(lf_all, _strict_upper(PAGE_SIZE).astype(F32), HIGHEST)
    carries, r_out = _page_carries(r_scr[...], jnp.sum(lf_all, axis=-1, keepdims=True), pp)
    r_scr[...] = r_out
    s_all = s_scr[...] + excl + carries
    row_max = jnp.max(s_all, axis=-1, keepdims=True)
    m_old = m_scr[...]
    m_new = m_old
    for j in range(pp):
        m_new = jnp.maximum(m_new, row_max[j * GROUP:(j + 1) * GROUP])
    alpha = jnp.exp(m_old - m_new)
    p_all = jnp.exp(s_all - jnp.concatenate([m_new] * pp, axis=0))
    p_scr[...] = p_all
    row_sum = jnp.sum(p_all, axis=-1, keepdims=True)
    l_new = alpha * l_scr[...]
    for j in range(pp):
        l_new = l_new + row_sum[j * GROUP:(j + 1) * GROUP]
    l_scr[...] = l_new
    m_scr[...] = m_new
    _weighted_values(p_scr, v_refs, acc_scr, alpha)

    @pl.when(step == pl.num_programs(1) - 1)
    def _():
        l = l_scr[...]
        for h, sl in enumerate(_head_slices()):
            o_ref[0, sl, :] = jnp.sum(acc_scr[sl, :], axis=-1, keepdims=True) / l[h:h + 1, :]


def _sb_sample_kernel(pt_ref, q_ref, *refs, pp):
    k_refs, v_refs = refs[:pp], refs[pp:2 * pp]
    o_ref, qx_scr, z_scr, a_scr, r_scr, acc_scr = refs[2 * pp:]
    step = pl.program_id(1)

    @pl.when(step == 0)
    def _():
        qx_scr[...] = jnp.broadcast_to(q_ref[0] * QK_SCALE, qx_scr.shape)
        z_scr[...] = jnp.zeros_like(z_scr)
        r_scr[...] = jnp.zeros_like(r_scr)
        acc_scr[...] = jnp.zeros_like(acc_scr)

    _page_scores(k_refs, qx_scr, z_scr)
    z = z_scr[...]
    sp = _softplus_neg_abs(z)
    ls = jnp.minimum(z, 0.0) - sp
    l1m = -jnp.maximum(z, 0.0) - sp
    excl = _suffix_excl(l1m, _strict_upper(PAGE_SIZE).astype(BF16))
    carries, r_out = _page_carries(r_scr[...], jnp.sum(l1m, axis=-1, keepdims=True), pp)
    r_scr[...] = r_out
    a_scr[...] = jnp.exp(ls + excl + carries)
    _weighted_values(a_scr, v_refs, acc_scr, None)

    @pl.when(step == pl.num_programs(1) - 1)
    def _():
        o_ref[...] = jnp.sum(acc_scr[...], axis=-1, keepdims=True)[None]


def _pages_per_step(n_pages):
    pp = math.gcd(n_pages, MAX_SAMPLE_PAGES_PER_STEP)
    return pp


def _page_specs(page_table, base, pp, block):
    n_pages = page_table.shape[1]

    def spec(j):
        return pl.BlockSpec(block, lambda bi, st, pt: (base + pt[bi, n_pages - 1 - (st * pp + j)], 0, 0))

    return [spec(j) for j in range(pp)]


def _per_sample(bi, st, pt):
    return (bi, 0, 0)


def _fox_sample(q, k_new, v_new, lf_new, cache_kt, cache_vt, cache_lft, page_table, base):
    b, n_pages = page_table.shape
    pp = _pages_per_step(n_pages)
    col = lambda x: x[:, :, None]
    lfn = col(jnp.concatenate([lf_new, jnp.zeros((b, GROUP - HEADS), F32)], axis=1))
    kv_block = (1, BLK, PAGE_SIZE)
    grid_spec = pltpu.PrefetchScalarGridSpec(
        num_scalar_prefetch=1,
        grid=(b, n_pages // pp),
        in_specs=[pl.BlockSpec((1, BLK, 1), _per_sample)] * 3 + [pl.BlockSpec((1, GROUP, 1), _per_sample)]
        + _page_specs(page_table, base, pp, kv_block) + _page_specs(page_table, base, pp, kv_block)
        + _page_specs(page_table, base, pp, (1, HEADS, PAGE_SIZE)),
        out_specs=pl.BlockSpec((1, BLK, 1), _per_sample),
        scratch_shapes=[pltpu.VMEM((BLK, PAGE_SIZE), F32)] + [pltpu.VMEM((pp * GROUP, PAGE_SIZE), F32)] * 3
        + [pltpu.VMEM((GROUP, 1), F32)] * 3 + [pltpu.VMEM((BLK, PAGE_SIZE), F32)],
    )
    out = pl.pallas_call(
        functools.partial(_fox_sample_kernel, pp=pp),
        grid_spec=grid_spec,
        out_shape=jax.ShapeDtypeStruct((b, BLK, 1), F32),
        compiler_params=_cparams(("parallel", "arbitrary")),
        name="fox_sample",
    )(page_table, col(q), col(k_new), col(v_new), lfn, *([cache_kt] * pp), *([cache_vt] * pp), *([cache_lft] * pp))
    return out[:, :, 0]


def _sb_sample(q, cache_kt, cache_vt, page_table, base):
    b, n_pages = page_table.shape
    pp = _pages_per_step(n_pages)
    kv_block = (1, BLK, PAGE_SIZE)
    grid_spec = pltpu.PrefetchScalarGridSpec(
        num_scalar_prefetch=1,
        grid=(b, n_pages // pp),
        in_specs=[pl.BlockSpec((1, BLK, 1), _per_sample)]
        + _page_specs(page_table, base, pp, kv_block) + _page_specs(page_table, base, pp, kv_block),
        out_specs=pl.BlockSpec((1, BLK, 1), _per_sample),
        scratch_shapes=[pltpu.VMEM((BLK, PAGE_SIZE), F32)] + [pltpu.VMEM((pp * GROUP, PAGE_SIZE), F32)] * 2
        + [pltpu.VMEM((GROUP, 1), F32), pltpu.VMEM((BLK, PAGE_SIZE), F32)],
    )
    out = pl.pallas_call(
        functools.partial(_sb_sample_kernel, pp=pp),
        grid_spec=grid_spec,
        out_shape=jax.ShapeDtypeStruct((b, BLK, 1), F32),
        compiler_params=_cparams(("parallel", "arbitrary")),
        name="sb_sample",
    )(page_table, q[:, :, None], *([cache_kt] * pp), *([cache_vt] * pp))
    return out[:, :, 0]


def _rot_half(x):
    half = HEAD_DIM // 2
    lane = lax.broadcasted_iota(jnp.int32, x.shape, 1) % HEAD_DIM
    n = x.shape[1]
    return jnp.where(lane < half, -pltpu.roll(x, n - half, 1), pltpu.roll(x, half, 1))


def _ret_kernel(q_ref, k_ref, v_ref, g_ref, cos_ref, sin_ref, dmask_ref, xi_ref, zeta_ref, cd_ref, gnw_ref, s0_ref,
                o_ref, st_ref, s_scr):
    c = pl.program_id(1)

    @pl.when(c == 0)
    def _():
        s_scr[...] = s0_ref[0]

    cos = cos_ref[...]
    sin = sin_ref[...]
    q = q_ref[0]
    k = k_ref[0]
    qr = (q * cos + _rot_half(q) * sin).astype(BF16)
    kr = (k * cos + _rot_half(k) * sin) * QK_SCALE
    v = v_ref[0].astype(BF16)
    g = g_ref[0]
    gate = g * jax.nn.sigmoid(g)
    gnw = gnw_ref[...]
    heads = range(HEADS)
    sls = _head_slices()
    qh = [qr[:, sl] for sl in sls]
    kh = [kr[:, sl] for sl in sls]
    vh = [v[:, sl] for sl in sls]
    s = [s_scr[h] for h in heads]
    att = [(_dot_nt(qh[h], kh[h].astype(BF16)) * dmask_ref[h]).astype(BF16) for h in heads]
    cross = [_dot(qh[h], s[h].astype(BF16)) * xi_ref[h] for h in heads]
    o = [_dot(att[h], vh[h]) + cross[h] for h in heads]
    kv = [_dot_tn((kh[h] * zeta_ref[h]).astype(BF16), vh[h]) for h in heads]
    for h in heads:
        s_scr[h] = cd_ref[h] * s[h] + kv[h]
    mu = [jnp.mean(x, axis=-1, keepdims=True) for x in o]
    d = [o[h] - mu[h] for h in heads]
    var = [jnp.mean(x * x, axis=-1, keepdims=True) for x in d]
    for h, sl in enumerate(sls):
        o_ref[0, :, sl] = gate[:, sl] * (d[h] * lax.rsqrt(var[h] + GN_EPS) * gnw[:, sl])

    @pl.when(c == pl.num_programs(1) - 1)
    def _():
        st_ref[0] = s_scr[...]


def _ret_tables(chunk, n_valid):
    lg = np.log(1.0 - 2.0 ** (-5.0 - np.arange(HEADS, dtype=np.float64)))
    n = np.arange(chunk, dtype=np.float64)
    diff = n[:, None] - n[None, :]
    valid = (n < n_valid).astype(np.float64)
    dmask = np.where(diff[None] >= 0, np.exp(np.maximum(diff, 0.0)[None] * lg[:, None, None]), 0.0)
    dmask = dmask * valid[None, :, None] * valid[None, None, :]
    xi = np.exp((n[None, :] + 1.0) * lg[:, None]) * valid[None, :]
    zeta = np.exp((n_valid - 1.0 - n)[None, :] * lg[:, None]) * valid[None, :]
    cd = np.exp(n_valid * lg)
    f = lambda a: jnp.asarray(a.astype(np.float32))
    return f(dmask), f(xi[:, :, None]), f(zeta[:, :, None]), f(cd[:, None, None])


def _rope_tables(pos):
    half = HEAD_DIM // 2
    inv = ROPE_BASE ** (-jnp.arange(half, dtype=F32) / half)
    ang = pos.astype(F32)[:, None] * inv[None, :]
    cos = jnp.tile(jnp.cos(ang), (1, 2 * HEADS))
    sin = jnp.tile(jnp.sin(ang), (1, 2 * HEADS))
    return cos, sin


def _retention(h3, col0, cos, sin, tables, gnw, s0, chunk):
    b, t, _ = h3.shape
    nc = t // chunk
    cb = col0 // BLK
    dmask, xi, zeta, cd = tables

    def tok(k):
        return pl.BlockSpec((1, chunk, BLK), functools.partial(lambda bi, c, k: (bi, c, k), k=cb + k))

    const3 = lambda bi, c: (0, 0, 0)
    return pl.pallas_call(
        _ret_kernel,
        grid=(b, nc),
        in_specs=[tok(0), tok(1), tok(2), tok(3),
                  pl.BlockSpec((chunk, BLK), lambda bi, c: (c, 0)),
                  pl.BlockSpec((chunk, BLK), lambda bi, c: (c, 0)),
                  pl.BlockSpec((HEADS, chunk, chunk), const3),
                  pl.BlockSpec((HEADS, chunk, 1), const3),
                  pl.BlockSpec((HEADS, chunk, 1), const3),
                  pl.BlockSpec((HEADS, 1, 1), const3),
                  pl.BlockSpec((1, BLK), lambda bi, c: (0, 0)),
                  pl.BlockSpec((1, HEADS, HEAD_DIM, HEAD_DIM), lambda bi, c: (bi, 0, 0, 0))],
        out_specs=[pl.BlockSpec((1, chunk, BLK), lambda bi, c: (bi, c, 0)),
                   pl.BlockSpec((1, HEADS, HEAD_DIM, HEAD_DIM), lambda bi, c: (bi, 0, 0, 0))],
        out_shape=[jax.ShapeDtypeStruct((b, t, BLK), F32),
                   jax.ShapeDtypeStruct((b, HEADS, HEAD_DIM, HEAD_DIM), F32)],
        scratch_shapes=[pltpu.VMEM((HEADS, HEAD_DIM, HEAD_DIM), F32)],
        compiler_params=_cparams(("parallel", "arbitrary")),
        name="retention",
    )(h3, h3, h3, h3, cos, sin, dmask, xi, zeta, cd, gnw.reshape(1, BLK), s0)


def _head_sum_matrix():
    i = lax.broadcasted_iota(jnp.int32, (BLK, BLK), 0) // HEAD_DIM
    j = lax.broadcasted_iota(jnp.int32, (BLK, BLK), 1) // HEAD_DIM
    return (i == j).astype(F32)


def _rwkv_prep_kernel(h0_ref, h1_ref, h2_ref, h3_ref, c0_ref, c1_ref, c2_ref, c3_ref, shift_ref,
                      mu_ref, w0_ref, a0_ref, kk_w_ref, ka_ref, rk_ref, wup_ref, aup_ref, gup_ref,
                      r_ref, k_ref, v_ref, logd_ref, kk_ref, b_ref, g_ref, bonus_ref, *, tm, seq_len, n_valid):
    hr = jnp.concatenate([h0_ref[...], h1_ref[...], h2_ref[...], h3_ref[...]], axis=1)
    before = jnp.concatenate([c0_ref[...], c1_ref[...], c2_ref[...], c3_ref[...]], axis=1)[7:8]
    row = lax.broadcasted_iota(jnp.int32, (tm, 1), 0)
    pos = (pl.program_id(0) * tm + row) & (seq_len - 1)
    prev = jnp.where(row == 0, before, pltpu.roll(hr, 1, 0))
    prev = jnp.where(pos == 0, shift_ref[0], prev)
    valid = (pos < n_valid).astype(F32)
    xm = hr + (prev - hr) * mu_ref[...]
    r = xm[:, 0:BLK]
    k = xm[:, BLK:2 * BLK]
    v = xm[:, 2 * BLK:3 * BLK]
    o = 3 * BLK
    wd = xm[:, o:o + DECAY_LORA]
    ad = xm[:, o + DECAY_LORA:o + DECAY_LORA + AAA_LORA]
    gd = xm[:, o + DECAY_LORA + AAA_LORA:]
    wpre = w0_ref[...] + _dot(jnp.tanh(wd).astype(BF16), wup_ref[...])
    w = jnp.minimum(wpre, 0.0) - _softplus_neg_abs(wpre) - 0.5
    a = jax.nn.sigmoid(a0_ref[...] + _dot(ad.astype(BF16), aup_ref[...]))
    g = _dot(jax.nn.sigmoid(gd).astype(BF16), gup_ref[...])
    hs = _head_sum_matrix()
    kk = k * kk_w_ref[...]
    kk = kk * lax.rsqrt(_dot(kk * kk, hs, HIGHEST) + 1e-12)
    kmod = k * (1.0 + (a - 1.0) * ka_ref[...])
    bonus = _dot(r * kmod * rk_ref[...], hs, HIGHEST) * v
    r_ref[...] = r * valid
    k_ref[...] = kmod * valid
    v_ref[...] = v * valid
    logd_ref[...] = -jnp.exp(w) * valid
    kk_ref[...] = kk * valid
    b_ref[...] = kk * a * valid
    g_ref[...] = g
    bonus_ref[...] = bonus


def _rwkv_prep(h, shift0, seq_len, n_valid, mu, w0, a0, k_k, k_a, r_k, w_up, a_up, g_up, tm):
    m = h.shape[0]
    assert seq_len & (seq_len - 1) == 0 and seq_len % tm == 0 and tm % 8 == 0
    cb = COL_RWKV // BLK
    row = lambda n: pl.BlockSpec((1, n), lambda i: (0, 0))
    full = lambda a: pl.BlockSpec(a.shape, lambda i: (0, 0))
    tokb = pl.BlockSpec((tm, BLK), lambda i: (i, 0))
    cur = [pl.BlockSpec((tm, BLK), functools.partial(lambda i, c: (i, c), c=cb + c)) for c in range(4)]
    before = [pl.BlockSpec((8, BLK), functools.partial(lambda i, c: (jnp.maximum(i * (tm // 8) - 1, 0), c), c=cb + c))
              for c in range(4)]
    return pl.pallas_call(
        functools.partial(_rwkv_prep_kernel, tm=tm, seq_len=seq_len, n_valid=n_valid),
        grid=(m // tm,),
        in_specs=cur + before + [pl.BlockSpec((1, 1, RWKV_IN), lambda i: (i * tm // seq_len, 0, 0)),
                                 row(RWKV_IN), row(BLK), row(BLK), row(BLK), row(BLK), row(BLK),
                                 full(w_up), full(a_up), full(g_up)],
        out_specs=[tokb] * 8,
        out_shape=[jax.ShapeDtypeStruct((m, BLK), F32)] * 8,
        compiler_params=_cparams(("parallel",)),
        name="rwkv_prep",
    )(h, h, h, h, h, h, h, h, shift0[:, None, :], mu.reshape(1, -1), w0.reshape(1, -1), a0.reshape(1, -1),
      k_k.reshape(1, -1), k_a.reshape(1, -1), r_k.reshape(1, -1), w_up, a_up, g_up)


def _rwkv_chunk_kernel(r_ref, k_ref, v_ref, logd_ref, kk_ref, b_ref, r2_ref, y0_ref, gt_ref, ht_ref, *, chunk, per_step):
    c = chunk
    n = HEADS * c
    ri = lax.broadcasted_iota(jnp.int32, (c, c), 0)
    ci = lax.broadcasted_iota(jnp.int32, (c, c), 1)
    cum = (ci <= ri).astype(F32)
    rr = lax.broadcasted_iota(jnp.int32, (n, n), 0)
    cc = lax.broadcasted_iota(jnp.int32, (n, n), 1)
    assert c & (c - 1) == 0
    strict = (cc & (c - 1)) < (rr & (c - 1))
    incl = (cc & (c - 1)) <= (rr & (c - 1))
    eye = (rr == cc).astype(F32)
    eye_w = (lax.broadcasted_iota(jnp.int32, (BLK, BLK), 0) == lax.broadcasted_iota(jnp.int32, (BLK, BLK), 1)).astype(F32)
    lane_head = lax.broadcasted_iota(jnp.int32, (c, BLK), 1) // HEAD_DIM

    def stack(x):
        return jnp.concatenate([jnp.where(lane_head == h, x, 0.0) for h in range(HEADS)], axis=0)

    def unstack(xw):
        out = xw[0:c]
        for h in range(1, HEADS):
            out = out + xw[h * c:(h + 1) * c]
        return out

    chunks = range(per_step)
    rows = [slice(s * c, (s + 1) * c) for s in chunks]
    lk, lr, rb, rk, vw, be_w, ke_w, w_c = [], [], [], [], [], [], [], []
    for s in chunks:
        logd = logd_ref[0, rows[s], :]
        logw = _dot(cum, logd, HIGHEST)
        logw_c = logw[c - 1:c, :]
        e_neg = jnp.exp(-logw)
        e_end = jnp.exp(logw_c - logw)
        b = b_ref[0, rows[s], :]
        k = k_ref[0, rows[s], :]
        lk.append(stack(kk_ref[0, rows[s], :] * jnp.exp(logw - logd)))
        lr.append(stack(r_ref[0, rows[s], :] * jnp.exp(logw)))
        rb.append(stack(b * e_neg))
        rk.append(stack(k * e_neg))
        be_w.append(stack(b * e_end))
        ke_w.append(stack(k * e_end))
        vw.append(stack(v_ref[0, rows[s], :]))
        w_c.append(jnp.exp(logw_c))

    lhs = [jnp.concatenate([lk[s], lr[s]], axis=0) for s in chunks]
    ab_mb = [_mm(lhs[s], rb[s], _NT, 1) for s in chunks]
    ak_mk = [_mm(lhs[s], rk[s], _NT, 1) for s in chunks]
    m_b = [jnp.where(incl, ab_mb[s][n:], 0.0) for s in chunks]
    am_k = [jnp.concatenate([jnp.where(strict, ak_mk[s][:n], 0.0), jnp.where(incl, ak_mk[s][n:], 0.0)], axis=0)
            for s in chunks]
    npow = [jnp.where(strict, -ab_mb[s][:n], 0.0) for s in chunks]
    tinv = [eye + npow[s] for s in chunks]
    for _ in range(int(math.log2(c)) - 1):
        npow = [_mm(npow[s], npow[s], _NN, 1) for s in chunks]
        tinv = [tinv[s] + _mm(tinv[s], npow[s], _NN, 1) for s in chunks]
    akv_mkv = [_mm(am_k[s], vw[s], _NN, 1) for s in chunks]
    pq = [_mm(tinv[s], jnp.concatenate([lk[s], akv_mkv[s][:n]], axis=1), _NN, 1) for s in chunks]
    mb_pq = [_mm(m_b[s], pq[s], _NN, 1) for s in chunks]
    for s in chunks:
        r2_ref[0, rows[s], :] = unstack(lr[s] - mb_pq[s][:, :BLK])
        y0_ref[0, rows[s], :] = unstack(akv_mkv[s][n:] - mb_pq[s][:, BLK:])
    for s in chunks:
        p_w, q_w = pq[s][:, :BLK], pq[s][:, BLK:]
        gt_ref[0, s] = eye_w * w_c[s] - _mm(be_w[s], p_w, _TN, 1)
        ht_ref[0, s] = _mm(ke_w[s], vw[s], _TN, 1) - _mm(be_w[s], q_w, _TN, 1)


def _chunks_per_step(nc):
    return math.gcd(nc, RWKV_CHUNKS_PER_STEP)


def _rwkv_chunks(r, k, v, logd, kk, b, chunk):
    bsz, t, _ = r.shape
    nc = t // chunk
    ps = _chunks_per_step(nc)
    tok = pl.BlockSpec((1, ps * chunk, BLK), lambda bi, c: (bi, c, 0))
    mat = pl.BlockSpec((1, ps, BLK, BLK), lambda bi, c: (bi, c, 0, 0))
    mat_shape = jax.ShapeDtypeStruct((bsz, nc, BLK, BLK), F32)
    return pl.pallas_call(
        functools.partial(_rwkv_chunk_kernel, chunk=chunk, per_step=ps),
        grid=(bsz, nc // ps),
        in_specs=[tok] * 6,
        out_specs=[tok, tok, mat, mat],
        out_shape=[jax.ShapeDtypeStruct((bsz, t, BLK), F32)] * 2 + [mat_shape] * 2,
        compiler_params=_cparams(("parallel", "parallel")),
        name="rwkv_chunks",
    )(r, k, v, logd, kk, b)


def _rwkv_scan_kernel(r2_ref, y0_ref, gt_ref, ht_ref, gate_ref, bonus_ref, lnw_ref, lnb_ref, s0_ref,
                      o_ref, st_ref, s_scr, *, chunk, per_step):
    step = pl.program_id(1)

    @pl.when(step == 0)
    def _():
        s_scr[...] = jnp.zeros_like(s_scr)
        for h, sl in enumerate(_head_slices()):
            s_scr[sl, sl] = s0_ref[0, h]

    lnw = lnw_ref[...]
    lnb = lnb_ref[...]
    sts = [s_scr[...]]
    for i in range(per_step):
        sts.append(_mm(gt_ref[0, i], sts[i], _NN, 1) + ht_ref[0, i])
    st = sts[per_step]
    s_scr[...] = st
    rows = [slice(i * chunk, (i + 1) * chunk) for i in range(per_step)]
    ys = [_mm(r2_ref[0, rows[i], :], sts[i], _NN, 1) + y0_ref[0, rows[i], :] for i in range(per_step)]
    pieces = [(i, sl) for i in range(per_step) for sl in _head_slices()]
    yh = [ys[i][:, sl] for i, sl in pieces]
    mu = [jnp.mean(x, axis=-1, keepdims=True) for x in yh]
    d = [yh[n] - mu[n] for n in range(len(pieces))]
    var = [jnp.mean(x * x, axis=-1, keepdims=True) for x in d]
    for n, (i, sl) in enumerate(pieces):
        yn = d[n] * lax.rsqrt(var[n] + RWKV_GN_EPS) * lnw[:, sl] + lnb[:, sl]
        o_ref[0, rows[i], sl] = (yn + bonus_ref[0, rows[i], sl]) * gate_ref[0, rows[i], sl]

    @pl.when(step == pl.num_programs(1) - 1)
    def _():
        for h, sl in enumerate(_head_slices()):
            st_ref[0, h] = st[sl, sl]


def _rwkv_scan(r2, y0, gt, ht, gate, bonus, ln_w, ln_b, st0, chunk):
    bsz, t, _ = r2.shape
    nc = t // chunk
    ps = _chunks_per_step(nc)
    tok = pl.BlockSpec((1, ps * chunk, BLK), lambda bi, c: (bi, c, 0))
    mat = pl.BlockSpec((1, ps, BLK, BLK), lambda bi, c: (bi, c, 0, 0))
    state = pl.BlockSpec((1, HEADS, HEAD_DIM, HEAD_DIM), lambda bi, c: (bi, 0, 0, 0))
    row = pl.BlockSpec((1, BLK), lambda bi, c: (0, 0))
    return pl.pallas_call(
        functools.partial(_rwkv_scan_kernel, chunk=chunk, per_step=ps),
        grid=(bsz, nc // ps),
        in_specs=[tok, tok, mat, mat, tok, tok, row, row, state],
        out_specs=[tok, state],
        out_shape=[jax.ShapeDtypeStruct((bsz, t, BLK), F32),
                   jax.ShapeDtypeStruct((bsz, HEADS, HEAD_DIM, HEAD_DIM), F32)],
        scratch_shapes=[pltpu.VMEM((BLK, BLK), F32)],
        compiler_params=_cparams(("parallel", "arbitrary")),
        name="rwkv_scan",
    )(r2, y0, gt, ht, gate, bonus, ln_w.reshape(1, BLK), ln_b.reshape(1, BLK), st0)


def _rwkv(h3, shift0, s0, n_valid, p, chunk, tm):
    bsz, t, cols = h3.shape
    outs = _rwkv_prep(h3.reshape(bsz * t, cols), shift0, t, n_valid,
                      p["mu"], p["w0"], p["a0"], p["k_k"], p["k_a"], p["r_k"], p["w_up"], p["a_up"], p["g_up"], tm)
    r, k, v, logd, kk, b, gate, bonus = [o.reshape(bsz, t, BLK) for o in outs]
    r2, y0, gt, ht = _rwkv_chunks(r, k, v, logd, kk, b, chunk)
    out, st = _rwkv_scan(r2, y0, gt, ht, gate, bonus, p["ln_w"], p["ln_b"], jnp.swapaxes(s0, 2, 3), chunk)
    return out, jnp.swapaxes(st, 2, 3)


def _reorder_w_in(w):
    fox_end = 3 * BRANCH_W
    ff_end = fox_end + HEADS
    gate_start = w.shape[-1] - N_BRANCH * D_MODEL
    pad = jnp.zeros(w.shape[:-1] + (BRANCH_W - HEADS,), w.dtype)
    out = jnp.concatenate([w[..., gate_start:], w[..., :fox_end], w[..., ff_end:gate_start],
                           w[..., fox_end:ff_end], pad], axis=-1)
    assert out.shape[-1] == PROJ_COLS
    return out.astype(BF16)


def _heads4(x):
    return x.reshape(x.shape[:-1] + (HEADS, HEAD_DIM))


def kernel(x_prompt, x_sample, cache_fox_k, cache_fox_v, cache_fox_logf, cache_sb_k, cache_sb_v, state_ret, state_rwkv, state_rwkv_shift, page_table, w_in, fox_f_bias, ret_gn_w, rwkv_mu, rwkv_w0, rwkv_w_up, rwkv_a0, rwkv_a_up, rwkv_g_up, rwkv_k_k, rwkv_k_a, rwkv_r_k, rwkv_ln_w, rwkv_ln_b, w_branch, w_out, norm_ffn1, ffn1_w_in, ffn1_w_out, norm_mix, norm_ffn2, ffn2_w_in, ffn2_w_out, norm_final):
    depth = w_in.shape[0]
    bp, t, d = x_prompt.shape
    bs = x_sample.shape[0]
    n_pool = cache_fox_k.shape[1]
    past_len = page_table.shape[1] * PAGE_SIZE

    w_in_r = _reorder_w_in(w_in)
    f1i, f1o, f2i, f2o = (a.astype(BF16) for a in (ffn1_w_in, ffn1_w_out, ffn2_w_in, ffn2_w_out))
    wbr = w_branch.astype(BF16)
    wo = w_out.astype(BF16)
    w_up, a_up, g_up = rwkv_w_up.astype(BF16), rwkv_a_up.astype(BF16), rwkv_g_up.astype(BF16)

    pool = lambda c: jnp.transpose(c, (0, 1, 3, 4, 2)).reshape(depth * n_pool, BLK, PAGE_SIZE)
    ck_fox, cv_fox, ck_sb, cv_sb = pool(cache_fox_k), pool(cache_fox_v), pool(cache_sb_k), pool(cache_sb_v)
    lf_t = jnp.swapaxes(cache_fox_logf, 2, 3).reshape(depth * n_pool, HEADS, PAGE_SIZE)

    cos_p, sin_p = _rope_tables(jnp.arange(t))
    s_pad = 8
    cos_s, sin_s = _rope_tables(past_len + jnp.arange(s_pad))
    ret_tab_p = _ret_tables(RET_CHUNK, RET_CHUNK)
    ret_tab_s = _ret_tables(s_pad, 1)
    zeros_state = jnp.zeros((bp, HEADS, HEAD_DIM, HEAD_DIM), F32)

    xp = x_prompt.reshape(bp * t, d)
    xs = x_sample.reshape(bs, d)
    new_p = [[] for _ in range(8)]
    new_s = [[] for _ in range(8)]
    tm_p = 1024

    for l in range(depth):
        last = l == depth - 1
        rw = dict(mu=rwkv_mu[l], w0=rwkv_w0[l], a0=rwkv_a0[l], k_k=rwkv_k_k[l], k_a=rwkv_k_a[l], r_k=rwkv_r_k[l],
                  w_up=w_up[l], a_up=a_up[l], g_up=g_up[l], ln_w=rwkv_ln_w[l], ln_b=rwkv_ln_b[l])

        xp = _ffn(xp, norm_ffn1[l], f1i[l], f1o[l], norm_final, False, tm_p)
        h = _proj(xp, norm_mix[l], w_in_r[l], min(2 * tm_p, bp * t))
        h3 = h.reshape(bp, t, PROJ_COLS)
        ff_t = jnp.swapaxes(h3[:, :, COL_FORGET:COL_FORGET + HEADS], 1, 2).reshape(bp * HEADS, t)
        lf_rows, *cum_terms = _forget(ff_t, jnp.tile(fox_f_bias[l], bp)[:, None])
        by_token = lambda rows: jnp.swapaxes(rows.reshape(bp, HEADS, t), 1, 2)
        lf = by_token(lf_rows)
        terms = jnp.concatenate([by_token(c) for c in cum_terms]
                                + [jnp.zeros((bp, t, TERM_COLS - 3 * HEADS), BF16)], axis=-1)
        packed = _pack_operands(h, terms.reshape(bp * t, TERM_COLS), tm_p)
        fqa, fka, fva, sqa, ska, sva = [a.reshape(bp, t, HEADS * PACK) for a in packed]
        o_a = _fox_prompt(fqa, fka, fva)
        o_b, ret_p = _retention(h3, COL_RET, cos_p, sin_p, ret_tab_p, ret_gn_w[l], zeros_state, RET_CHUNK)
        o_c, rw_p = _rwkv(h3, jnp.zeros((bp, RWKV_IN), F32), zeros_state, t, rw, RWKV_CHUNK, tm_p)
        o_d = _sb_prompt(sqa, ska, sva)
        flat = lambda o: o.reshape(bp * t, BLK)
        xp = _merge(xp, [flat(o_a), flat(o_b), flat(o_c), flat(o_d)], h, wbr[l], wo[l], 512)
        xp = _ffn(xp, norm_ffn2[l], f2i[l], f2o[l], norm_final, last, tm_p)
        for i, a in enumerate((_heads4(h3[:, :, COL_FOX + BLK:COL_FOX + 2 * BLK]),
                               _heads4(h3[:, :, COL_FOX + 2 * BLK:COL_FOX + 3 * BLK]), lf,
                               _heads4(h3[:, :, COL_SB + BLK:COL_SB + 2 * BLK]),
                               _heads4(h3[:, :, COL_SB + 2 * BLK:COL_SB + 3 * BLK]), ret_p, rw_p,
                               h3[:, -1, COL_RWKV:COL_RWKV + RWKV_IN])):
            new_p[i].append(a)

        xs = _ffn(xs, norm_ffn1[l], f1i[l], f1o[l], norm_final, False, bs)
        hs = _proj(xs, norm_mix[l], w_in_r[l], bs)
        col = lambda c0, n=BLK: hs[:, c0:c0 + n]
        ff_s = jnp.swapaxes(col(COL_FORGET, HEADS), 0, 1)
        ff_s = jnp.concatenate([ff_s, jnp.zeros((8 - HEADS, bs), F32)], axis=0)
        bias_s = jnp.concatenate([fox_f_bias[l], jnp.zeros((8 - HEADS,), F32)])[:, None]
        lf_s_rows = _forget(jnp.pad(ff_s, ((0, 0), (0, 128 - bs))), bias_s)[0]
        lf_s = jnp.swapaxes(lf_s_rows[:HEADS, :bs], 0, 1)
        o_a = _fox_sample(col(COL_FOX), col(COL_FOX + BLK), col(COL_FOX + 2 * BLK), lf_s,
                          ck_fox, cv_fox, lf_t, page_table, l * n_pool)
        hs_pad = jnp.pad(hs[:, None, :], ((0, 0), (0, s_pad - 1), (0, 0)))
        o_b, ret_s = _retention(hs_pad, COL_RET, cos_s, sin_s, ret_tab_s, ret_gn_w[l], state_ret[l], s_pad)
        o_c, rw_s = _rwkv(hs_pad, state_rwkv_shift[l], state_rwkv[l], 1, rw, s_pad, s_pad)
        o_d = _sb_sample(col(COL_SB), ck_sb, cv_sb, page_table, l * n_pool)
        xs = _merge(xs, [o_a, o_b[:, 0], o_c[:, 0], o_d], hs, wbr[l], wo[l], bs)
        xs = _ffn(xs, norm_ffn2[l], f2i[l], f2o[l], norm_final, last, bs)
        for i, a in enumerate((_heads4(col(COL_FOX + BLK))[:, None], _heads4(col(COL_FOX + 2 * BLK))[:, None],
                               lf_s[:, None], _heads4(col(COL_SB + BLK))[:, None],
                               _heads4(col(COL_SB + 2 * BLK))[:, None], ret_s, rw_s, col(COL_RWKV, RWKV_IN))):
            new_s[i].append(a)

    sp = [jnp.stack(v) for v in new_p]
    ss = [jnp.stack(v) for v in new_s]
    return (xp.reshape(bp, t, d), xs.reshape(bs, 1, d), sp[0], sp[1], sp[2], sp[3], sp[4], sp[5], sp[6], sp[7],
            ss[0], ss[1], ss[2], ss[3], ss[4], ss[5], ss[6], ss[7])
```

```python
import functools
import math

import jax
import jax.numpy as jnp
import numpy as np
from jax import lax
from jax.experimental import pallas as pl
from jax.experimental.pallas import tpu as pltpu

F32 = jnp.float32
BF16 = jnp.bfloat16
HIGHEST = lax.Precision.HIGHEST

D_MODEL = 1024
HEADS = 4
HEAD_DIM = 64
BRANCH_W = HEADS * HEAD_DIM
N_BRANCH = 4
DECAY_LORA = 64
AAA_LORA = 64
GATE_LORA = 128
RWKV_IN = 3 * BRANCH_W + DECAY_LORA + AAA_LORA + GATE_LORA
PAGE_SIZE = 128
RET_CHUNK = 128
ROPE_BASE = 10000.0
RMS_EPS = 1e-6
GN_EPS = 1e-5
RWKV_GN_EPS = 64e-5
QK_SCALE = HEAD_DIM ** -0.5
NEG_BIG = -1e30

COL_GATE = 0
COL_FOX = N_BRANCH * D_MODEL
COL_RET = COL_FOX + 3 * BRANCH_W
COL_RWKV = COL_RET + 4 * BRANCH_W
COL_SB = COL_RWKV + RWKV_IN
COL_FORGET = COL_SB + 3 * BRANCH_W
PROJ_COLS = COL_FORGET + BRANCH_W
BLK = BRANCH_W

VMEM_LIMIT = 56 * 1024 * 1024
RWKV_CHUNK = 64
RWKV_CHUNKS_PER_STEP = 4


def _cparams(sem):
    return pltpu.CompilerParams(dimension_semantics=sem, vmem_limit_bytes=VMEM_LIMIT)


def _dot(a, b, precision=None):
    return jnp.dot(a, b, preferred_element_type=F32, precision=precision)


def _dot_nt(a, b, precision=None):
    return lax.dot_general(a, b, (((1,), (1,)), ((), ())), preferred_element_type=F32, precision=precision)


def _dot_tn(a, b, precision=None):
    return lax.dot_general(a, b, (((0,), (0,)), ((), ())), preferred_element_type=F32, precision=precision)


_NN = (((1,), (0,)), ((), ()))
_NT = (((1,), (1,)), ((), ()))
_TN = (((0,), (0,)), ((), ()))


def _mm(a, b, dims, passes):
    dg = lambda x, y: lax.dot_general(x, y, dims, preferred_element_type=F32)
    ah = a.astype(BF16)
    bh = b.astype(BF16)
    if passes == 1:
        return dg(ah, bh)
    al = (a - ah.astype(F32)).astype(BF16)
    bl = (b - bh.astype(F32)).astype(BF16)
    return dg(ah, bh) + (dg(ah, bl) + dg(al, bh))


PREC_PAIR = 1
PREC_INV = 1
PREC_APPLY = 1
PREC_SCAN = 1


def _rms(x, g):
    return x * lax.rsqrt(jnp.mean(x * x, axis=-1, keepdims=True) + RMS_EPS) * g


def _softplus_neg_abs(z):
    return jnp.log(1.0 + jnp.exp(-jnp.abs(z)))


def _head_slices():
    return [slice(h * HEAD_DIM, (h + 1) * HEAD_DIM) for h in range(HEADS)]


def _ffn_kernel(x_ref, g_ref, wa_ref, wb_ref, wo_ref, gf_ref, o_ref, xn_ref, *, final_norm):
    j = pl.program_id(1)

    @pl.when(j == 0)
    def _():
        xn_ref[...] = _rms(x_ref[...], g_ref[...]).astype(BF16)
        o_ref[...] = jnp.zeros_like(o_ref)

    xn = xn_ref[...]
    a = _dot(xn, wa_ref[...])
    b = _dot(xn, wb_ref[...])
    hmid = (a * (0.5 * jnp.tanh(0.5 * a) + 0.5)) * b
    o_ref[...] += _dot(hmid.astype(BF16), wo_ref[...])

    @pl.when(j == pl.num_programs(1) - 1)
    def _():
        y = x_ref[...] + 0.5 * o_ref[...]
        if final_norm:
            o_ref[...] = _rms(y, gf_ref[...])
        else:
            o_ref[...] = y


def _ffn(x, g, wi, wo, gf, final_norm, tm, tf=256):
    m, d = x.shape
    f = wo.shape[0]
    nf = f // tf
    kern = functools.partial(_ffn_kernel, final_norm=final_norm)
    outs = pl.pallas_call(
        kern,
        grid=(m // tm, nf),
        in_specs=[
            pl.BlockSpec((tm, d), lambda i, j: (i, 0)),
            pl.BlockSpec((1, d), lambda i, j: (0, 0)),
            pl.BlockSpec((d, tf), lambda i, j: (0, j)),
            pl.BlockSpec((d, tf), lambda i, j: (0, j + nf)),
            pl.BlockSpec((tf, d), lambda i, j: (j, 0)),
            pl.BlockSpec((1, d), lambda i, j: (0, 0)),
        ],
        out_specs=pl.BlockSpec((tm, d), lambda i, j: (i, 0)),
        out_shape=jax.ShapeDtypeStruct((m, d), F32),
        scratch_shapes=[pltpu.VMEM((tm, d), BF16)],
        compiler_params=_cparams(("parallel", "arbitrary")),
        name="ffn",
    )(x, g.reshape(1, d), wi, wi, wo, gf.reshape(1, d))
    return outs


def _proj_kernel(x_ref, g_ref, w_ref, o_ref, xn_ref):
    @pl.when(pl.program_id(1) == 0)
    def _():
        xn_ref[...] = _rms(x_ref[...], g_ref[...]).astype(BF16)

    o_ref[...] = _dot(xn_ref[...], w_ref[...])


def _proj(x, g, w, tm, tn=256):
    m, d = x.shape
    n = w.shape[1]
    return pl.pallas_call(
        _proj_kernel,
        grid=(m // tm, n // tn),
        in_specs=[
            pl.BlockSpec((tm, d), lambda i, j: (i, 0)),
            pl.BlockSpec((1, d), lambda i, j: (0, 0)),
            pl.BlockSpec((d, tn), lambda i, j: (0, j)),
        ],
        out_specs=pl.BlockSpec((tm, tn), lambda i, j: (i, j)),
        out_shape=jax.ShapeDtypeStruct((m, n), F32),
        scratch_shapes=[pltpu.VMEM((tm, d), BF16)],
        compiler_params=_cparams(("parallel", "arbitrary")),
        name="proj",
    )(x, g.reshape(1, d), w)


def _merge_kernel(x_ref, oa_ref, ob_ref, oc_ref, od_ref, g0_ref, g1_ref, g2_ref, g3_ref, wb_ref, wo_ref, o_ref):
    merged = None
    for i, (o_r, g_r) in enumerate(((oa_ref, g0_ref), (ob_ref, g1_ref), (oc_ref, g2_ref), (od_ref, g3_ref))):
        gate = 0.5 * jnp.tanh(0.5 * g_r[...]) + 0.5
        t = gate * _dot(o_r[...].astype(BF16), wb_ref[i])
        merged = t if merged is None else merged + t
    o_ref[...] = x_ref[...] + _dot(merged.astype(BF16), wo_ref[...])


def _merge(x, branches, h, wb, wo, tm):
    m, d = x.shape
    gate_specs = [pl.BlockSpec((tm, d), functools.partial(lambda i, k: (i, k), k=COL_GATE // d + k)) for k in range(N_BRANCH)]
    return pl.pallas_call(
        _merge_kernel,
        grid=(m // tm,),
        in_specs=[pl.BlockSpec((tm, d), lambda i: (i, 0))]
        + [pl.BlockSpec((tm, BRANCH_W), lambda i: (i, 0))] * N_BRANCH
        + gate_specs
        + [pl.BlockSpec((N_BRANCH, BRANCH_W, d), lambda i: (0, 0, 0)), pl.BlockSpec((d, d), lambda i: (0, 0))],
        out_specs=pl.BlockSpec((tm, d), lambda i: (i, 0)),
        out_shape=jax.ShapeDtypeStruct((m, d), F32),
        compiler_params=_cparams(("parallel",)),
        name="merge",
    )(x, *branches, h, h, h, h, wb, wo)


def _forget_kernel(ff_ref, bias_ref, lf_ref, hi_ref, mid_ref, lo_ref, *, t):
    x = ff_ref[...] + bias_ref[...]
    lf = jnp.minimum(x, 0.0) - _softplus_neg_abs(x)
    lf_ref[...] = lf
    w = 128
    tri = (lax.broadcasted_iota(jnp.int32, (w, w), 0) <= lax.broadcasted_iota(jnp.int32, (w, w), 1)).astype(F32)
    carry = jnp.zeros((x.shape[0], 1), F32)
    for c in range(t // w):
        cols = slice(c * w, (c + 1) * w)
        cs = _dot(lf[:, cols], tri, HIGHEST) + carry
        carry = cs[:, w - 1:w]
        hi = cs.astype(BF16)
        r1 = cs - hi.astype(F32)
        mid = r1.astype(BF16)
        hi_ref[:, cols] = hi
        mid_ref[:, cols] = mid
        lo_ref[:, cols] = (r1 - mid.astype(F32)).astype(BF16)


def _forget(ff_t, bias_rows):
    r, t = ff_t.shape
    term = jax.ShapeDtypeStruct((r, t), BF16)
    return pl.pallas_call(
        functools.partial(_forget_kernel, t=t),
        out_shape=(jax.ShapeDtypeStruct((r, t), F32), term, term, term),
        compiler_params=pltpu.CompilerParams(vmem_limit_bytes=VMEM_LIMIT),
        name="forget",
    )(ff_t, bias_rows)


PACK = 128


TERM_COLS = 16


def _pack_kernel(fq_ref, fk_ref, fv_ref, sq_ref, sk_ref, sv_ref, terms_ref,
                 fqa_ref, fka_ref, fva_ref, sqa_ref, ska_ref, sva_ref):
    w = HEADS * PACK
    r = lax.broadcasted_iota(jnp.int32, (BLK, w), 0)
    c = lax.broadcasted_iota(jnp.int32, (BLK, w), 1)
    place = ((c // PACK == r // HEAD_DIM) & (c % PACK == r % HEAD_DIM)).astype(BF16)
    e = lax.broadcasted_iota(jnp.int32, (TERM_COLS, w), 0)
    ce = lax.broadcasted_iota(jnp.int32, (TERM_COLS, w), 1)
    term_lane = (e % HEADS) * PACK + HEAD_DIM + e // HEADS
    real = e < 3 * HEADS
    place_q = ((ce == term_lane) & real).astype(BF16)
    place_k = -((ce == term_lane + 3) & real).astype(BF16)
    lane = lax.broadcasted_iota(jnp.int32, (1, w), 1) % PACK
    ones_q = ((lane >= HEAD_DIM + 3) & (lane < HEAD_DIM + 6)).astype(F32)
    ones_k = ((lane >= HEAD_DIM) & (lane < HEAD_DIM + 3)).astype(F32)
    ones_v = (lane == HEAD_DIM).astype(F32)
    terms = terms_ref[...]

    def put(x):
        return _dot(x.astype(BF16), place)

    fqa_ref[...] = (put(fq_ref[...] * QK_SCALE) + _dot(terms, place_q) + ones_q).astype(BF16)
    fka_ref[...] = (put(fk_ref[...]) + _dot(terms, place_k) + ones_k).astype(BF16)
    fva_ref[...] = (put(fv_ref[...]) + ones_v).astype(BF16)
    sqa_ref[...] = put(sq_ref[...] * QK_SCALE).astype(BF16)
    ska_ref[...] = put(sk_ref[...]).astype(BF16)
    sva_ref[...] = put(sv_ref[...]).astype(BF16)


def _pack_operands(h, terms, tm):
    m = h.shape[0]
    w = HEADS * PACK
    colblk = lambda c0: pl.BlockSpec((tm, BLK), functools.partial(lambda i, c: (i, c), c=c0 // BLK))
    cols = [COL_FOX, COL_FOX + BLK, COL_FOX + 2 * BLK, COL_SB, COL_SB + BLK, COL_SB + 2 * BLK]
    return pl.pallas_call(
        _pack_kernel,
        grid=(m // tm,),
        in_specs=[colblk(c0) for c0 in cols] + [pl.BlockSpec((tm, TERM_COLS), lambda i: (i, 0))],
        out_specs=[pl.BlockSpec((tm, w), lambda i: (i, 0))] * 6,
        out_shape=[jax.ShapeDtypeStruct((m, w), BF16)] * 6,
        compiler_params=_cparams(("parallel",)),
        name="pack",
    )(h, h, h, h, h, h, terms)


def _causal_pairs(nq, keys_descending):
    qs, ks = [], []
    for qi in range(nq):
        order = range(qi, -1, -1) if keys_descending else range(qi + 1)
        for ki in order:
            qs.append(qi)
            ks.append(ki)
    return jnp.asarray(np.array(qs, np.int32)), jnp.asarray(np.array(ks, np.int32))


def _fox_kernel(qtab_ref, ktab_ref, q_ref, k_ref, v_ref, o_ref, m_scr, acc_scr, *, tq):
    qi = qtab_ref[pl.program_id(1)]
    ki = ktab_ref[pl.program_id(1)]
    heads = range(HEADS)
    lanes = [slice(h * PACK, (h + 1) * PACK) for h in heads]

    @pl.when(ki == 0)
    def _():
        m_scr[...] = jnp.full_like(m_scr, NEG_BIG)
        acc_scr[...] = jnp.zeros_like(acc_scr)

    def block(diagonal):
        s = [_dot_nt(q_ref[0, :, lanes[h]], k_ref[0, :, lanes[h]]) for h in heads]
        if diagonal:
            mask = lax.broadcasted_iota(jnp.int32, (tq, tq), 1) <= lax.broadcasted_iota(jnp.int32, (tq, tq), 0)
            s = [jnp.where(mask, x, NEG_BIG) for x in s]
        m_old = [m_scr[h] for h in heads]
        m_new = [jnp.maximum(m_old[h], jnp.max(s[h], axis=-1, keepdims=True)) for h in heads]
        p = [jnp.exp(s[h] - m_new[h]).astype(BF16) for h in heads]
        pv = [_dot(p[h], v_ref[0, :, lanes[h]]) for h in heads]
        for h in heads:
            acc_scr[h] = jnp.exp(m_old[h] - m_new[h]) * acc_scr[h] + pv[h]
            m_scr[h] = m_new[h]

    pl.when(ki < qi)(functools.partial(block, False))

    @pl.when(ki == qi)
    def _():
        block(True)
        for h, sl in enumerate(_head_slices()):
            acc = acc_scr[h]
            o_ref[0, :, sl] = acc[:, :HEAD_DIM] / acc[:, HEAD_DIM:HEAD_DIM + 1]


def _pair_call(kern, name, qa, ka, va, tq, keys_descending, scratch):
    b, t, w = qa.shape
    qtab, ktab = _causal_pairs(t // tq, keys_descending)
    grid_spec = pltpu.PrefetchScalarGridSpec(
        num_scalar_prefetch=2,
        grid=(b, qtab.shape[0]),
        in_specs=[
            pl.BlockSpec((1, tq, w), lambda bi, p, qt, kt: (bi, qt[p], 0)),
            pl.BlockSpec((1, tq, w), lambda bi, p, qt, kt: (bi, kt[p], 0)),
            pl.BlockSpec((1, tq, w), lambda bi, p, qt, kt: (bi, kt[p], 0)),
        ],
        out_specs=pl.BlockSpec((1, tq, BLK), lambda bi, p, qt, kt: (bi, qt[p], 0)),
        scratch_shapes=scratch,
    )
    return pl.pallas_call(
        functools.partial(kern, tq=tq),
        grid_spec=grid_spec,
        out_shape=jax.ShapeDtypeStruct((b, t, BLK), F32),
        compiler_params=_cparams(("parallel", "arbitrary")),
        name=name,
    )(qtab, ktab, qa, ka, va)


def _fox_prompt(qa, ka, va, tq=512):
    scratch = [pltpu.VMEM((HEADS, tq, 1), F32), pltpu.VMEM((HEADS, tq, PACK), F32)]
    return _pair_call(_fox_kernel, "fox_prompt", qa, ka, va, tq, False, scratch)


def _strict_upper(n):
    return lax.broadcasted_iota(jnp.int32, (n, n), 0) > lax.broadcasted_iota(jnp.int32, (n, n), 1)


def _suffix_excl(x, u_bf16):
    hi = x.astype(BF16)
    lo = (x - hi.astype(F32)).astype(BF16)
    return _dot(hi, u_bf16) + _dot(lo, u_bf16)


def _sb_kernel(qtab_ref, ktab_ref, q_ref, k_ref, v_ref, o_ref, r_scr, acc_scr, *, tq):
    qi = qtab_ref[pl.program_id(1)]
    ki = ktab_ref[pl.program_id(1)]

    @pl.when(ki == qi)
    def _():
        r_scr[...] = jnp.zeros_like(r_scr)
        acc_scr[...] = jnp.zeros_like(acc_scr)

    heads = range(HEADS)
    lanes = [slice(h * PACK, (h + 1) * PACK) for h in heads]

    def block(diagonal):
        u = _strict_upper(tq).astype(BF16)
        z = [_dot_nt(q_ref[0, :, lanes[h]], k_ref[0, :, lanes[h]]) for h in heads]
        sp = [_softplus_neg_abs(x) for x in z]
        ls = [jnp.minimum(z[h], 0.0) - sp[h] for h in heads]
        l1m = [-jnp.maximum(z[h], 0.0) - sp[h] for h in heads]
        if diagonal:
            mask = lax.broadcasted_iota(jnp.int32, (tq, tq), 1) < lax.broadcasted_iota(jnp.int32, (tq, tq), 0)
            l1m = [jnp.where(mask, x, 0.0) for x in l1m]
        excl = [_suffix_excl(x, u) for x in l1m]
        a = [jnp.exp(ls[h] + excl[h] + r_scr[h]) for h in heads]
        if diagonal:
            a = [jnp.where(mask, x, 0.0) for x in a]
        av = [_dot(a[h].astype(BF16), v_ref[0, :, lanes[h]]) for h in heads]
        for h in heads:
            acc_scr[h] += av[h]
            r_scr[h] += jnp.sum(l1m[h], axis=-1, keepdims=True)

    pl.when(ki == qi)(functools.partial(block, True))
    pl.when(ki < qi)(functools.partial(block, False))

    @pl.when(ki == 0)
    def _():
        for h, sl in enumerate(_head_slices()):
            o_ref[0, :, sl] = acc_scr[h][:, :HEAD_DIM]


def _sb_prompt(qa, ka, va, tq=256):
    scratch = [pltpu.VMEM((HEADS, tq, 1), F32), pltpu.VMEM((HEADS, tq, PACK), F32)]
    return _pair_call(_sb_kernel, "sb_prompt", qa, ka, va, tq, True, scratch)


GROUP = 8
MAX_SAMPLE_PAGES_PER_STEP = 32


def _page_scores(k_refs, qx_scr, s_scr):
    for j, k_ref in enumerate(k_refs):
        for h, sl in enumerate(_head_slices()):
            s_scr[j * GROUP + h:j * GROUP + h + 1, :] = jnp.sum(k_ref[0, sl, :] * qx_scr[sl, :], axis=0, keepdims=True)


def _page_carries(r, tot, pp):
    carries = []
    for j in range(pp):
        carries.append(r)
        r = r + tot[j * GROUP:(j + 1) * GROUP]
    return jnp.concatenate(carries, axis=0), r


def _weighted_values(w_scr, v_refs, acc_scr, scale):
    for h, sl in enumerate(_head_slices()):
        acc = acc_scr[sl, :] if scale is None else acc_scr[sl, :] * scale[h:h + 1, :]
        for j, v_ref in enumerate(v_refs):
            acc = acc + w_scr[j * GROUP + h:j * GROUP + h + 1, :] * v_ref[0, sl, :]
        acc_scr[sl, :] = acc


def _fox_sample_kernel(pt_ref, q_ref, kn_ref, vn_ref, lfn_ref, *refs, pp):
    k_refs, v_refs, lf_refs = refs[:pp], refs[pp:2 * pp], refs[2 * pp:3 * pp]
    o_ref, qx_scr, s_scr, lf_scr, p_scr, m_scr, l_scr, r_scr, acc_scr = refs[3 * pp:]
    step = pl.program_id(1)

    @pl.when(step == 0)
    def _():
        qcol = q_ref[0] * QK_SCALE
        qx_scr[...] = jnp.broadcast_to(qcol, qx_scr.shape)
        s_scr[...] = jnp.zeros_like(s_scr)
        lf_scr[...] = jnp.zeros_like(lf_scr)
        m_scr[...] = jnp.zeros_like(m_scr)
        own = qcol * kn_ref[0]
        for h, sl in enumerate(_head_slices()):
            m_scr[h:h + 1, :] = jnp.sum(own[sl], axis=0, keepdims=True)
        l_scr[...] = jnp.ones_like(l_scr)
        r_scr[...] = lfn_ref[0]
        lane = lax.broadcasted_iota(jnp.int32, acc_scr.shape, 1)
        acc_scr[...] = jnp.where(lane == 0, vn_ref[0], 0.0)

    for j, lf_ref in enumerate(lf_refs):
        lf_scr[j * GROUP:j * GROUP + HEADS, :] = lf_ref[0]
    _page_scores(k_refs, qx_scr, s_scr)
    lf_all = lf_scr[...]
    excl = _dot(lf_all, _strict_upper(PAGE_SIZE).astype(F32), HIGHEST)
    carries, r_out = _page_carries(r_scr[...], jnp.sum(lf_all, axis=-1, keepdims=True), pp)
    r_scr[...] = r_out
    s_all = s_scr[...] + excl + carries
    row_max = jnp.max(s_all, axis=-1, keepdims=True)
    m_old = m_scr[...]
    m_new = m_old
    for j in range(pp):
        m_new = jnp.maximum(m_new, row_max[j * GROUP:(j + 1) * GROUP])
    alpha = jnp.exp(m_old - m_new)
    p_all = jnp.exp(s_all - jnp.concatenate([m_new] * pp, axis=0))
    p_scr[...] = p_all
    row_sum = jnp.sum(p_all, axis=-1, keepdims=True)
    l_new = alpha * l_scr[...]
    for j in range(pp):
        l_new = l_new + row_sum[j * GROUP:(j + 1) * GROUP]
    l_scr[...] = l_new
    m_scr[...] = m_new
    _weighted_values(p_scr, v_refs, acc_scr, alpha)

    @pl.when(step == pl.num_programs(1) - 1)
    def _():
        l = l_scr[...]
        for h, sl in enumerate(_head_slices()):
            o_ref[0, sl, :] = jnp.sum(acc_scr[sl, :], axis=-1, keepdims=True) / l[h:h + 1, :]


def _sb_sample_kernel(pt_ref, q_ref, *refs, pp):
    k_refs, v_refs = refs[:pp], refs[pp:2 * pp]
    o_ref, qx_scr, z_scr, a_scr, r_scr, acc_scr = refs[2 * pp:]
    step = pl.program_id(1)

    @pl.when(step == 0)
    def _():
        qx_scr[...] = jnp.broadcast_to(q_ref[0] * QK_SCALE, qx_scr.shape)
        z_scr[...] = jnp.zeros_like(z_scr)
        r_scr[...] = jnp.zeros_like(r_scr)
        acc_scr[...] = jnp.zeros_like(acc_scr)

    _page_scores(k_refs, qx_scr, z_scr)
    z = z_scr[...]
    sp = _softplus_neg_abs(z)
    ls = jnp.minimum(z, 0.0) - sp
    l1m = -jnp.maximum(z, 0.0) - sp
    excl = _suffix_excl(l1m, _strict_upper(PAGE_SIZE).astype(BF16))
    carries, r_out = _page_carries(r_scr[...], jnp.sum(l1m, axis=-1, keepdims=True), pp)
    r_scr[...] = r_out
    a_scr[...] = jnp.exp(ls + excl + carries)
    _weighted_values(a_scr, v_refs, acc_scr, None)

    @pl.when(step == pl.num_programs(1) - 1)
    def _():
        o_ref[...] = jnp.sum(acc_scr[...], axis=-1, keepdims=True)[None]


def _pages_per_step(n_pages):
    pp = math.gcd(n_pages, MAX_SAMPLE_PAGES_PER_STEP)
    return pp


def _page_specs(page_table, base, pp, block):
    n_pages = page_table.shape[1]

    def spec(j):
        return pl.BlockSpec(block, lambda bi, st, pt: (base + pt[bi, n_pages - 1 - (st * pp + j)], 0, 0))

    return [spec(j) for j in range(pp)]


def _per_sample(bi, st, pt):
    return (bi, 0, 0)


def _fox_sample(q, k_new, v_new, lf_new, cache_kt, cache_vt, cache_lft, page_table, base):
    b, n_pages = page_table.shape
    pp = _pages_per_step(n_pages)
    col = lambda x: x[:, :, None]
    lfn = col(jnp.concatenate([lf_new, jnp.zeros((b, GROUP - HEADS), F32)], axis=1))
    kv_block = (1, BLK, PAGE_SIZE)
    grid_spec = pltpu.PrefetchScalarGridSpec(
        num_scalar_prefetch=1,
        grid=(b, n_pages // pp),
        in_specs=[pl.BlockSpec((1, BLK, 1), _per_sample)] * 3 + [pl.BlockSpec((1, GROUP, 1), _per_sample)]
        + _page_specs(page_table, base, pp, kv_block) + _page_specs(page_table, base, pp, kv_block)
        + _page_specs(page_table, base, pp, (1, HEADS, PAGE_SIZE)),
        out_specs=pl.BlockSpec((1, BLK, 1), _per_sample),
        scratch_shapes=[pltpu.VMEM((BLK, PAGE_SIZE), F32)] + [pltpu.VMEM((pp * GROUP, PAGE_SIZE), F32)] * 3
        + [pltpu.VMEM((GROUP, 1), F32)] * 3 + [pltpu.VMEM((BLK, PAGE_SIZE), F32)],
    )
    out = pl.pallas_call(
        functools.partial(_fox_sample_kernel, pp=pp),
        grid_spec=grid_spec,
        out_shape=jax.ShapeDtypeStruct((b, BLK, 1), F32),
        compiler_params=_cparams(("parallel", "arbitrary")),
        name="fox_sample",
    )(page_table, col(q), col(k_new), col(v_new), lfn, *([cache_kt] * pp), *([cache_vt] * pp), *([cache_lft] * pp))
    return out[:, :, 0]


def _sb_sample(q, cache_kt, cache_vt, page_table, base):
    b, n_pages = page_table.shape
    pp = _pages_per_step(n_pages)
    kv_block = (1, BLK, PAGE_SIZE)
    grid_spec = pltpu.PrefetchScalarGridSpec(
        num_scalar_prefetch=1,
        grid=(b, n_pages // pp),
        in_specs=[pl.BlockSpec((1, BLK, 1), _per_sample)]
        + _page_specs(page_table, base, pp, kv_block) + _page_specs(page_table, base, pp, kv_block),
        out_specs=pl.BlockSpec((1, BLK, 1), _per_sample),
        scratch_shapes=[pltpu.VMEM((BLK, PAGE_SIZE), F32)] + [pltpu.VMEM((pp * GROUP, PAGE_SIZE), F32)] * 2
        + [pltpu.VMEM((GROUP, 1), F32), pltpu.VMEM((BLK, PAGE_SIZE), F32)],
    )
    out = pl.pallas_call(
        functools.partial(_sb_sample_kernel, pp=pp),
        grid_spec=grid_spec,
        out_shape=jax.ShapeDtypeStruct((b, BLK, 1), F32),
        compiler_params=_cparams(("parallel", "arbitrary")),
        name="sb_sample",
    )(page_table, q[:, :, None], *([cache_kt] * pp), *([cache_vt] * pp))
    return out[:, :, 0]


def _rot_half(x):
    half = HEAD_DIM // 2
    lane = lax.broadcasted_iota(jnp.int32, x.shape, 1) % HEAD_DIM
    n = x.shape[1]
    return jnp.where(lane < half, -pltpu.roll(x, n - half, 1), pltpu.roll(x, half, 1))


def _ret_kernel(q_ref, k_ref, v_ref, g_ref, cos_ref, sin_ref, dmask_ref, xi_ref, zeta_ref, cd_ref, gnw_ref, s0_ref,
                o_ref, st_ref, s_scr):
    c = pl.program_id(1)

    @pl.when(c == 0)
    def _():
        s_scr[...] = s0_ref[0]

    cos = cos_ref[...]
    sin = sin_ref[...]
    q = q_ref[0]
    k = k_ref[0]
    qr = (q * cos + _rot_half(q) * sin).astype(BF16)
    kr = (k * cos + _rot_half(k) * sin) * QK_SCALE
    v = v_ref[0].astype(BF16)
    g = g_ref[0]
    gate = g * jax.nn.sigmoid(g)
    gnw = gnw_ref[...]
    heads = range(HEADS)
    sls = _head_slices()
    qh = [qr[:, sl] for sl in sls]
    kh = [kr[:, sl] for sl in sls]
    vh = [v[:, sl] for sl in sls]
    s = [s_scr[h] for h in heads]
    att = [(_dot_nt(qh[h], kh[h].astype(BF16)) * dmask_ref[h]).astype(BF16) for h in heads]
    cross = [_dot(qh[h], s[h].astype(BF16)) * xi_ref[h] for h in heads]
    o = [_dot(att[h], vh[h]) + cross[h] for h in heads]
    kv = [_dot_tn((kh[h] * zeta_ref[h]).astype(BF16), vh[h]) for h in heads]
    for h in heads:
        s_scr[h] = cd_ref[h] * s[h] + kv[h]
    mu = [jnp.mean(x, axis=-1, keepdims=True) for x in o]
    d = [o[h] - mu[h] for h in heads]
    var = [jnp.mean(x * x, axis=-1, keepdims=True) for x in d]
    for h, sl in enumerate(sls):
        o_ref[0, :, sl] = gate[:, sl] * (d[h] * lax.rsqrt(var[h] + GN_EPS) * gnw[:, sl])

    @pl.when(c == pl.num_programs(1) - 1)
    def _():
        st_ref[0] = s_scr[...]


def _ret_tables(chunk, n_valid):
    lg = np.log(1.0 - 2.0 ** (-5.0 - np.arange(HEADS, dtype=np.float64)))
    n = np.arange(chunk, dtype=np.float64)
    diff = n[:, None] - n[None, :]
    valid = (n < n_valid).astype(np.float64)
    dmask = np.where(diff[None] >= 0, np.exp(np.maximum(diff, 0.0)[None] * lg[:, None, None]), 0.0)
    dmask = dmask * valid[None, :, None] * valid[None, None, :]
    xi = np.exp((n[None, :] + 1.0) * lg[:, None]) * valid[None, :]
    zeta = np.exp((n_valid - 1.0 - n)[None, :] * lg[:, None]) * valid[None, :]
    cd = np.exp(n_valid * lg)
    f = lambda a: jnp.asarray(a.astype(np.float32))
    return f(dmask), f(xi[:, :, None]), f(zeta[:, :, None]), f(cd[:, None, None])


def _rope_tables(pos):
    half = HEAD_DIM // 2
    inv = ROPE_BASE ** (-jnp.arange(half, dtype=F32) / half)
    ang = pos.astype(F32)[:, None] * inv[None, :]
    cos = jnp.tile(jnp.cos(ang), (1, 2 * HEADS))
    sin = jnp.tile(jnp.sin(ang), (1, 2 * HEADS))
    return cos, sin


def _retention(h3, col0, cos, sin, tables, gnw, s0, chunk):
    b, t, _ = h3.shape
    nc = t // chunk
    cb = col0 // BLK
    dmask, xi, zeta, cd = tables

    def tok(k):
        return pl.BlockSpec((1, chunk, BLK), functools.partial(lambda bi, c, k: (bi, c, k), k=cb + k))

    const3 = lambda bi, c: (0, 0, 0)
    return pl.pallas_call(
        _ret_kernel,
        grid=(b, nc),
        in_specs=[tok(0), tok(1), tok(2), tok(3),
                  pl.BlockSpec((chunk, BLK), lambda bi, c: (c, 0)),
                  pl.BlockSpec((chunk, BLK), lambda bi, c: (c, 0)),
                  pl.BlockSpec((HEADS, chunk, chunk), const3),
                  pl.BlockSpec((HEADS, chunk, 1), const3),
                  pl.BlockSpec((HEADS, chunk, 1), const3),
                  pl.BlockSpec((HEADS, 1, 1), const3),
                  pl.BlockSpec((1, BLK), lambda bi, c: (0, 0)),
                  pl.BlockSpec((1, HEADS, HEAD_DIM, HEAD_DIM), lambda bi, c: (bi, 0, 0, 0))],
        out_specs=[pl.BlockSpec((1, chunk, BLK), lambda bi, c: (bi, c, 0)),
                   pl.BlockSpec((1, HEADS, HEAD_DIM, HEAD_DIM), lambda bi, c: (bi, 0, 0, 0))],
        out_shape=[jax.ShapeDtypeStruct((b, t, BLK), F32),
                   jax.ShapeDtypeStruct((b, HEADS, HEAD_DIM, HEAD_DIM), F32)],
        scratch_shapes=[pltpu.VMEM((HEADS, HEAD_DIM, HEAD_DIM), F32)],
        compiler_params=_cparams(("parallel", "arbitrary")),
        name="retention",
    )(h3, h3, h3, h3, cos, sin, dmask, xi, zeta, cd, gnw.reshape(1, BLK), s0)


def _head_sum_matrix():
    i = lax.broadcasted_iota(jnp.int32, (BLK, BLK), 0) // HEAD_DIM
    j = lax.broadcasted_iota(jnp.int32, (BLK, BLK), 1) // HEAD_DIM
    return (i == j).astype(F32)


def _rwkv_prep_kernel(h0_ref, h1_ref, h2_ref, h3_ref, c0_ref, c1_ref, c2_ref, c3_ref, shift_ref,
                      mu_ref, w0_ref, a0_ref, kk_w_ref, ka_ref, rk_ref, wup_ref, aup_ref, gup_ref,
                      r_ref, k_ref, v_ref, logd_ref, kk_ref, b_ref, g_ref, bonus_ref, *, tm, seq_len, n_valid):
    hr = jnp.concatenate([h0_ref[...], h1_ref[...], h2_ref[...], h3_ref[...]], axis=1)
    before = jnp.concatenate([c0_ref[...], c1_ref[...], c2_ref[...], c3_ref[...]], axis=1)[7:8]
    row = lax.broadcasted_iota(jnp.int32, (tm, 1), 0)
    pos = (pl.program_id(0) * tm + row) & (seq_len - 1)
    prev = jnp.where(row == 0, before, pltpu.roll(hr, 1, 0))
    prev = jnp.where(pos == 0, shift_ref[0], prev)
    valid = (pos < n_valid).astype(F32)
    xm = hr + (prev - hr) * mu_ref[...]
    r = xm[:, 0:BLK]
    k = xm[:, BLK:2 * BLK]
    v = xm[:, 2 * BLK:3 * BLK]
    o = 3 * BLK
    wd = xm[:, o:o + DECAY_LORA]
    ad = xm[:, o + DECAY_LORA:o + DECAY_LORA + AAA_LORA]
    gd = xm[:, o + DECAY_LORA + AAA_LORA:]
    wpre = w0_ref[...] + _dot(jnp.tanh(wd).astype(BF16), wup_ref[...])
    w = jnp.minimum(wpre, 0.0) - _softplus_neg_abs(wpre) - 0.5
    a = jax.nn.sigmoid(a0_ref[...] + _dot(ad.astype(BF16), aup_ref[...]))
    g = _dot(jax.nn.sigmoid(gd).astype(BF16), gup_ref[...])
    hs = _head_sum_matrix()
    kk = k * kk_w_ref[...]
    kk = kk * lax.rsqrt(_dot(kk * kk, hs, HIGHEST) + 1e-12)
    kmod = k * (1.0 + (a - 1.0) * ka_ref[...])
    bonus = _dot(r * kmod * rk_ref[...], hs, HIGHEST) * v
    r_ref[...] = r * valid
    k_ref[...] = kmod * valid
    v_ref[...] = v * valid
    logd_ref[...] = -jnp.exp(w) * valid
    kk_ref[...] = kk * valid
    b_ref[...] = kk * a * valid
    g_ref[...] = g
    bonus_ref[...] = bonus


def _rwkv_prep(h, shift0, seq_len, n_valid, mu, w0, a0, k_k, k_a, r_k, w_up, a_up, g_up, tm):
    m = h.shape[0]
    assert seq_len & (seq_len - 1) == 0 and seq_len % tm == 0 and tm % 8 == 0
    cb = COL_RWKV // BLK
    row = lambda n: pl.BlockSpec((1, n), lambda i: (0, 0))
    full = lambda a: pl.BlockSpec(a.shape, lambda i: (0, 0))
    tokb = pl.BlockSpec((tm, BLK), lambda i: (i, 0))
    cur = [pl.BlockSpec((tm, BLK), functools.partial(lambda i, c: (i, c), c=cb + c)) for c in range(4)]
    before = [pl.BlockSpec((8, BLK), functools.partial(lambda i, c: (jnp.maximum(i * (tm // 8) - 1, 0), c), c=cb + c))
              for c in range(4)]
    return pl.pallas_call(
        functools.partial(_rwkv_prep_kernel, tm=tm, seq_len=seq_len, n_valid=n_valid),
        grid=(m // tm,),
        in_specs=cur + before + [pl.BlockSpec((1, 1, RWKV_IN), lambda i: (i * tm // seq_len, 0, 0)),
                                 row(RWKV_IN), row(BLK), row(BLK), row(BLK), row(BLK), row(BLK),
                                 full(w_up), full(a_up), full(g_up)],
        out_specs=[tokb] * 8,
        out_shape=[jax.ShapeDtypeStruct((m, BLK), F32)] * 8,
        compiler_params=_cparams(("parallel",)),
        name="rwkv_prep",
    )(h, h, h, h, h, h, h, h, shift0[:, None, :], mu.reshape(1, -1), w0.reshape(1, -1), a0.reshape(1, -1),
      k_k.reshape(1, -1), k_a.reshape(1, -1), r_k.reshape(1, -1), w_up, a_up, g_up)


def _rwkv_chunk_kernel(r_ref, k_ref, v_ref, logd_ref, kk_ref, b_ref, r2_ref, y0_ref, gt_ref, ht_ref, *, chunk, per_step):
    c = chunk
    n = HEADS * c
    ri = lax.broadcasted_iota(jnp.int32, (c, c), 0)
    ci = lax.broadcasted_iota(jnp.int32, (c, c), 1)
    cum = (ci <= ri).astype(F32)
    rr = lax.broadcasted_iota(jnp.int32, (n, n), 0)
    cc = lax.broadcasted_iota(jnp.int32, (n, n), 1)
    assert c & (c - 1) == 0
    strict = (cc & (c - 1)) < (rr & (c - 1))
    incl = (cc & (c - 1)) <= (rr & (c - 1))
    eye = (rr == cc).astype(F32)
    eye_w = (lax.broadcasted_iota(jnp.int32, (BLK, BLK), 0) == lax.broadcasted_iota(jnp.int32, (BLK, BLK), 1)).astype(F32)
    lane_head = lax.broadcasted_iota(jnp.int32, (c, BLK), 1) // HEAD_DIM

    def stack(x):
        return jnp.concatenate([jnp.where(lane_head == h, x, 0.0) for h in range(HEADS)], axis=0)

    def unstack(xw):
        out = xw[0:c]
        for h in range(1, HEADS):
            out = out + xw[h * c:(h + 1) * c]
        return out

    chunks = range(per_step)
    rows = [slice(s * c, (s + 1) * c) for s in chunks]
    lk, lr, rb, rk, vw, be_w, ke_w, w_c = [], [], [], [], [], [], [], []
    for s in chunks:
        logd = logd_ref[0, rows[s], :]
        logw = _dot(cum, logd, HIGHEST)
        logw_c = logw[c - 1:c, :]
        e_neg = jnp.exp(-logw)
        e_end = jnp.exp(logw_c - logw)
        b = b_ref[0, rows[s], :]
        k = k_ref[0, rows[s], :]
        lk.append(stack(kk_ref[0, rows[s], :] * jnp.exp(logw - logd)))
        lr.append(stack(r_ref[0, rows[s], :] * jnp.exp(logw)))
        rb.append(stack(b * e_neg))
        rk.append(stack(k * e_neg))
        be_w.append(stack(b * e_end))
        ke_w.append(stack(k * e_end))
        vw.append(stack(v_ref[0, rows[s], :]))
        w_c.append(jnp.exp(logw_c))

    lhs = [jnp.concatenate([lk[s], lr[s]], axis=0) for s in chunks]
    ab_mb = [_mm(lhs[s], rb[s], _NT, 1) for s in chunks]
    ak_mk = [_mm(lhs[s], rk[s], _NT, 1) for s in chunks]
    m_b = [jnp.where(incl, ab_mb[s][n:], 0.0) for s in chunks]
    am_k = [jnp.concatenate([jnp.where(strict, ak_mk[s][:n], 0.0), jnp.where(incl, ak_mk[s][n:], 0.0)], axis=0)
            for s in chunks]
    npow = [jnp.where(strict, -ab_mb[s][:n], 0.0) for s in chunks]
    tinv = [eye + npow[s] for s in chunks]
    for _ in range(int(math.log2(c)) - 1):
        npow = [_mm(npow[s], npow[s], _NN, 1) for s in chunks]
        tinv = [tinv[s] + _mm(tinv[s], npow[s], _NN, 1) for s in chunks]
    akv_mkv = [_mm(am_k[s], vw[s], _NN, 1) for s in chunks]
    pq = [_mm(tinv[s], jnp.concatenate([lk[s], akv_mkv[s][:n]], axis=1), _NN, 1) for s in chunks]
    mb_pq = [_mm(m_b[s], pq[s], _NN, 1) for s in chunks]
    for s in chunks:
        r2_ref[0, rows[s], :] = unstack(lr[s] - mb_pq[s][:, :BLK])
        y0_ref[0, rows[s], :] = unstack(akv_mkv[s][n:] - mb_pq[s][:, BLK:])
    for s in chunks:
        p_w, q_w = pq[s][:, :BLK], pq[s][:, BLK:]
        gt_ref[0, s] = eye_w * w_c[s] - _mm(be_w[s], p_w, _TN, 1)
        ht_ref[0, s] = _mm(ke_w[s], vw[s], _TN, 1) - _mm(be_w[s], q_w, _TN, 1)


def _chunks_per_step(nc):
    return math.gcd(nc, RWKV_CHUNKS_PER_STEP)


def _rwkv_chunks(r, k, v, logd, kk, b, chunk):
    bsz, t, _ = r.shape
    nc = t // chunk
    ps = _chunks_per_step(nc)
    tok = pl.BlockSpec((1, ps * chunk, BLK), lambda bi, c: (bi, c, 0))
    mat = pl.BlockSpec((1, ps, BLK, BLK), lambda bi, c: (bi, c, 0, 0))
    mat_shape = jax.ShapeDtypeStruct((bsz, nc, BLK, BLK), F32)
    return pl.pallas_call(
        functools.partial(_rwkv_chunk_kernel, chunk=chunk, per_step=ps),
        grid=(bsz, nc // ps),
        in_specs=[tok] * 6,
        out_specs=[tok, tok, mat, mat],
        out_shape=[jax.ShapeDtypeStruct((bsz, t, BLK), F32)] * 2 + [mat_shape] * 2,
        compiler_params=_cparams(("parallel", "parallel")),
        name="rwkv_chunks",
    )(r, k, v, logd, kk, b)


def _rwkv_scan_kernel(r2_ref, y0_ref, gt_ref, ht_ref, gate_ref, bonus_ref, lnw_ref, lnb_ref, s0_ref,
                      o_ref, st_ref, s_scr, *, chunk, per_step):
    step = pl.program_id(1)

    @pl.when(step == 0)
    def _():
        s_scr[...] = jnp.zeros_like(s_scr)
        for h, sl in enumerate(_head_slices()):
            s_scr[sl, sl] = s0_ref[0, h]

    lnw = lnw_ref[...]
    lnb = lnb_ref[...]
    sts = [s_scr[...]]
    for i in range(per_step):
        sts.append(_mm(gt_ref[0, i], sts[i], _NN, 1) + ht_ref[0, i])
    st = sts[per_step]
    s_scr[...] = st
    rows = [slice(i * chunk, (i + 1) * chunk) for i in range(per_step)]
    ys = [_mm(r2_ref[0, rows[i], :], sts[i], _NN, 1) + y0_ref[0, rows[i], :] for i in range(per_step)]
    pieces = [(i, sl) for i in range(per_step) for sl in _head_slices()]
    yh = [ys[i][:, sl] for i, sl in pieces]
    mu = [jnp.mean(x, axis=-1, keepdims=True) for x in yh]
    d = [yh[n] - mu[n] for n in range(len(pieces))]
    var = [jnp.mean(x * x, axis=-1, keepdims=True) for x in d]
    for n, (i, sl) in enumerate(pieces):
        yn = d[n] * lax.rsqrt(var[n] + RWKV_GN_EPS) * lnw[:, sl] + lnb[:, sl]
        o_ref[0, rows[i], sl] = (yn + bonus_ref[0, rows[i], sl]) * gate_ref[0, rows[i], sl]

    @pl.when(step == pl.num_programs(1) - 1)
    def _():
        for h, sl in enumerate(_head_slices()):
            st_ref[0, h] = st[sl, sl]


def _rwkv_scan(r2, y0, gt, ht, gate, bonus, ln_w, ln_b, st0, chunk):
    bsz, t, _ = r2.shape
    nc = t // chunk
    ps = _chunks_per_step(nc)
    tok = pl.BlockSpec((1, ps * chunk, BLK), lambda bi, c: (bi, c, 0))
    mat = pl.BlockSpec((1, ps, BLK, BLK), lambda bi, c: (bi, c, 0, 0))
    state = pl.BlockSpec((1, HEADS, HEAD_DIM, HEAD_DIM), lambda bi, c: (bi, 0, 0, 0))
    row = pl.BlockSpec((1, BLK), lambda bi, c: (0, 0))
    return pl.pallas_call(
        functools.partial(_rwkv_scan_kernel, chunk=chunk, per_step=ps),
        grid=(bsz, nc // ps),
        in_specs=[tok, tok, mat, mat, tok, tok, row, row, state],
        out_specs=[tok, state],
        out_shape=[jax.ShapeDtypeStruct((bsz, t, BLK), F32),
                   jax.ShapeDtypeStruct((bsz, HEADS, HEAD_DIM, HEAD_DIM), F32)],
        scratch_shapes=[pltpu.VMEM((BLK, BLK), F32)],
        compiler_params=_cparams(("parallel", "arbitrary")),
        name="rwkv_scan",
    )(r2, y0, gt, ht, gate, bonus, ln_w.reshape(1, BLK), ln_b.reshape(1, BLK), st0)


def _rwkv(h3, shift0, s0, n_valid, p, chunk, tm):
    bsz, t, cols = h3.shape
    outs = _rwkv_prep(h3.reshape(bsz * t, cols), shift0, t, n_valid,
                      p["mu"], p["w0"], p["a0"], p["k_k"], p["k_a"], p["r_k"], p["w_up"], p["a_up"], p["g_up"], tm)
    r, k, v, logd, kk, b, gate, bonus = [o.reshape(bsz, t, BLK) for o in outs]
    r2, y0, gt, ht = _rwkv_chunks(r, k, v, logd, kk, b, chunk)
    out, st = _rwkv_scan(r2, y0, gt, ht, gate, bonus, p["ln_w"], p["ln_b"], jnp.swapaxes(s0, 2, 3), chunk)
    return out, jnp.swapaxes(st, 2, 3)


def _reorder_w_in(w):
    fox_end = 3 * BRANCH_W
    ff_end = fox_end + HEADS
    gate_start = w.shape[-1] - N_BRANCH * D_MODEL
    pad = jnp.zeros(w.shape[:-1] + (BRANCH_W - HEADS,), w.dtype)
    out = jnp.concatenate([w[..., gate_start:], w[..., :fox_end], w[..., ff_end:gate_start],
                           w[..., fox_end:ff_end], pad], axis=-1)
    assert out.shape[-1] == PROJ_COLS
    return out.astype(BF16)


def _heads4(x):
    return x.reshape(x.shape[:-1] + (HEADS, HEAD_DIM))


def kernel(x_prompt, x_sample, cache_fox_k, cache_fox_v, cache_fox_logf, cache_sb_k, cache_sb_v, state_ret, state_rwkv, state_rwkv_shift, page_table, w_in, fox_f_bias, ret_gn_w, rwkv_mu, rwkv_w0, rwkv_w_up, rwkv_a0, rwkv_a_up, rwkv_g_up, rwkv_k_k, rwkv_k_a, rwkv_r_k, rwkv_ln_w, rwkv_ln_b, w_branch, w_out, norm_ffn1, ffn1_w_in, ffn1_w_out, norm_mix, norm_ffn2, ffn2_w_in, ffn2_w_out, norm_final):
    depth = w_in.shape[0]
    bp, t, d = x_prompt.shape
    bs = x_sample.shape[0]
    n_pool = cache_fox_k.shape[1]
    past_len = page_table.shape[1] * PAGE_SIZE

    w_in_r = _reorder_w_in(w_in)
    f1i, f1o, f2i, f2o = (a.astype(BF16) for a in (ffn1_w_in, ffn1_w_out, ffn2_w_in, ffn2_w_out))
    wbr = w_branch.astype(BF16)
    wo = w_out.astype(BF16)
    w_up, a_up, g_up = rwkv_w_up.astype(BF16), rwkv_a_up.astype(BF16), rwkv_g_up.astype(BF16)

    pool = lambda c: jnp.transpose(c, (0, 1, 3, 4, 2)).reshape(depth * n_pool, BLK, PAGE_SIZE)
    ck_fox, cv_fox, ck_sb, cv_sb = pool(cache_fox_k), pool(cache_fox_v), pool(cache_sb_k), pool(cache_sb_v)
    lf_t = jnp.swapaxes(cache_fox_logf, 2, 3).reshape(depth * n_pool, HEADS, PAGE_SIZE)

    cos_p, sin_p = _rope_tables(jnp.arange(t))
    s_pad = 8
    cos_s, sin_s = _rope_tables(past_len + jnp.arange(s_pad))
    ret_tab_p = _ret_tables(RET_CHUNK, RET_CHUNK)
    ret_tab_s = _ret_tables(s_pad, 1)
    zeros_state = jnp.zeros((bp, HEADS, HEAD_DIM, HEAD_DIM), F32)

    xp = x_prompt.reshape(bp * t, d)
    xs = x_sample.reshape(bs, d)
    new_p = [[] for _ in range(8)]
    new_s = [[] for _ in range(8)]
    tm_p = 1024
    tm_wide = min(2 * tm_p, bp * t)

    for l in range(depth):
        last = l == depth - 1
        rw = dict(mu=rwkv_mu[l], w0=rwkv_w0[l], a0=rwkv_a0[l], k_k=rwkv_k_k[l], k_a=rwkv_k_a[l], r_k=rwkv_r_k[l],
                  w_up=w_up[l], a_up=a_up[l], g_up=g_up[l], ln_w=rwkv_ln_w[l], ln_b=rwkv_ln_b[l])

        xp = _ffn(xp, norm_ffn1[l], f1i[l], f1o[l], norm_final, False, tm_wide)
        h = _proj(xp, norm_mix[l], w_in_r[l], tm_wide)
        h3 = h.reshape(bp, t, PROJ_COLS)
        ff_t = jnp.swapaxes(h3[:, :, COL_FORGET:COL_FORGET + HEADS], 1, 2).reshape(bp * HEADS, t)
        lf_rows, *cum_terms = _forget(ff_t, jnp.tile(fox_f_bias[l], bp)[:, None])
        by_token = lambda rows: jnp.swapaxes(rows.reshape(bp, HEADS, t), 1, 2)
        lf = by_token(lf_rows)
        terms = jnp.concatenate([by_token(c) for c in cum_terms]
                                + [jnp.zeros((bp, t, TERM_COLS - 3 * HEADS), BF16)], axis=-1)
        packed = _pack_operands(h, terms.reshape(bp * t, TERM_COLS), tm_p)
        fqa, fka, fva, sqa, ska, sva = [a.reshape(bp, t, HEADS * PACK) for a in packed]
        o_a = _fox_prompt(fqa, fka, fva)
        o_b, ret_p = _retention(h3, COL_RET, cos_p, sin_p, ret_tab_p, ret_gn_w[l], zeros_state, RET_CHUNK)
        o_c, rw_p = _rwkv(h3, jnp.zeros((bp, RWKV_IN), F32), zeros_state, t, rw, RWKV_CHUNK, tm_p)
        o_d = _sb_prompt(sqa, ska, sva)
        flat = lambda o: o.reshape(bp * t, BLK)
        xp = _merge(xp, [flat(o_a), flat(o_b), flat(o_c), flat(o_d)], h, wbr[l], wo[l], 512)
        xp = _ffn(xp, norm_ffn2[l], f2i[l], f2o[l], norm_final, last, tm_wide)
        for i, a in enumerate((_heads4(h3[:, :, COL_FOX + BLK:COL_FOX + 2 * BLK]),
                               _heads4(h3[:, :, COL_FOX + 2 * BLK:COL_FOX + 3 * BLK]), lf,
                               _heads4(h3[:, :, COL_SB + BLK:COL_SB + 2 * BLK]),
                               _heads4(h3[:, :, COL_SB + 2 * BLK:COL_SB + 3 * BLK]), ret_p, rw_p,
                               h3[:, -1, COL_RWKV:COL_RWKV + RWKV_IN])):
            new_p[i].append(a)

        xs = _ffn(xs, norm_ffn1[l], f1i[l], f1o[l], norm_final, False, bs)
        hs = _proj(xs, norm_mix[l], w_in_r[l], bs)
        col = lambda c0, n=BLK: hs[:, c0:c0 + n]
        ff_s = jnp.swapaxes(col(COL_FORGET, HEADS), 0, 1)
        ff_s = jnp.concatenate([ff_s, jnp.zeros((8 - HEADS, bs), F32)], axis=0)
        bias_s = jnp.concatenate([fox_f_bias[l], jnp.zeros((8 - HEADS,), F32)])[:, None]
        lf_s_rows = _forget(jnp.pad(ff_s, ((0, 0), (0, 128 - bs))), bias_s)[0]
        lf_s = jnp.swapaxes(lf_s_rows[:HEADS, :bs], 0, 1)
        o_a = _fox_sample(col(COL_FOX), col(COL_FOX + BLK), col(COL_FOX + 2 * BLK), lf_s,
                          ck_fox, cv_fox, lf_t, page_table, l * n_pool)
        hs_pad = jnp.pad(hs[:, None, :], ((0, 0), (0, s_pad - 1), (0, 0)))
        o_b, ret_s = _retention(hs_pad, COL_RET, cos_s, sin_s, ret_tab_s, ret_gn_w[l], state_ret[l], s_pad)
        o_c, rw_s = _rwkv(hs_pad, state_rwkv_shift[l], state_rwkv[l], 1, rw, s_pad, s_pad)
        o_d = _sb_sample(col(COL_SB), ck_sb, cv_sb, page_table, l * n_pool)
        xs = _merge(xs, [o_a, o_b[:, 0], o_c[:, 0], o_d], hs, wbr[l], wo[l], bs)
        xs = _ffn(xs, norm_ffn2[l], f2i[l], f2o[l], norm_final, last, bs)
        for i, a in enumerate((_heads4(col(COL_FOX + BLK))[:, None], _heads4(col(COL_FOX + 2 * BLK))[:, None],
                               lf_s[:, None], _heads4(col(COL_SB + BLK))[:, None],
                               _heads4(col(COL_SB + 2 * BLK))[:, None], ret_s, rw_s, col(COL_RWKV, RWKV_IN))):
            new_s[i].append(a)

    sp = [jnp.stack(v) for v in new_p]
    ss = [jnp.stack(v) for v in new_s]
    return (xp.reshape(bp, t, d), xs.reshape(bs, 1, d), sp[0], sp[1], sp[2], sp[3], sp[4], sp[5], sp[6], sp[7],
            ss[0], ss[1], ss[2], ss[3], ss[4], ss[5], ss[6], ss[7])
```

```python
import functools
import math

import jax
import jax.numpy as jnp
import numpy as np
from jax import lax
from jax.experimental import pallas as pl
from jax.experimental.pallas import tpu as pltpu

F32 = jnp.float32
BF16 = jnp.bfloat16
HIGHEST = lax.Precision.HIGHEST

D_MODEL = 1024
HEADS = 4
HEAD_DIM = 64
BRANCH_W = HEADS * HEAD_DIM
N_BRANCH = 4
DECAY_LORA = 64
AAA_LORA = 64
GATE_LORA = 128
RWKV_IN = 3 * BRANCH_W + DECAY_LORA + AAA_LORA + GATE_LORA
PAGE_SIZE = 128
RET_CHUNK = 128
ROPE_BASE = 10000.0
RMS_EPS = 1e-6
GN_EPS = 1e-5
RWKV_GN_EPS = 64e-5
QK_SCALE = HEAD_DIM ** -0.5
NEG_BIG = -1e30

COL_GATE = 0
COL_FOX = N_BRANCH * D_MODEL
COL_RET = COL_FOX + 3 * BRANCH_W
COL_RWKV = COL_RET + 4 * BRANCH_W
COL_SB = COL_RWKV + RWKV_IN
COL_FORGET = COL_SB + 3 * BRANCH_W
PROJ_COLS = COL_FORGET + BRANCH_W
BLK = BRANCH_W

VMEM_LIMIT = 56 * 1024 * 1024
RWKV_CHUNK = 64
RWKV_CHUNKS_PER_STEP = 4


def _cparams(sem):
    return pltpu.CompilerParams(dimension_semantics=sem, vmem_limit_bytes=VMEM_LIMIT)


def _dot(a, b, precision=None):
    return jnp.dot(a, b, preferred_element_type=F32, precision=precision)


def _dot_nt(a, b, precision=None):
    return lax.dot_general(a, b, (((1,), (1,)), ((), ())), preferred_element_type=F32, precision=precision)


def _dot_tn(a, b, precision=None):
    return lax.dot_general(a, b, (((0,), (0,)), ((), ())), preferred_element_type=F32, precision=precision)


_NN = (((1,), (0,)), ((), ()))
_NT = (((1,), (1,)), ((), ()))
_TN = (((0,), (0,)), ((), ()))


def _mm(a, b, dims):
    return lax.dot_general(a.astype(BF16), b.astype(BF16), dims, preferred_element_type=F32)


def _rms(x, g):
    return x * lax.rsqrt(jnp.mean(x * x, axis=-1, keepdims=True) + RMS_EPS) * g


def _softplus_neg_abs(z):
    return jnp.log(1.0 + jnp.exp(-jnp.abs(z)))


def _head_slices():
    return [slice(h * HEAD_DIM, (h + 1) * HEAD_DIM) for h in range(HEADS)]


def _ffn_kernel(x_ref, g_ref, wa_ref, wb_ref, wo_ref, gf_ref, o_ref, xn_ref, *, final_norm):
    j = pl.program_id(1)

    @pl.when(j == 0)
    def _():
        xn_ref[...] = _rms(x_ref[...], g_ref[...]).astype(BF16)
        o_ref[...] = jnp.zeros_like(o_ref)

    xn = xn_ref[...]
    a = _dot(xn, wa_ref[...])
    b = _dot(xn, wb_ref[...])
    hmid = (a * (0.5 * jnp.tanh(0.5 * a) + 0.5)) * b
    o_ref[...] += _dot(hmid.astype(BF16), wo_ref[...])

    @pl.when(j == pl.num_programs(1) - 1)
    def _():
        y = x_ref[...] + 0.5 * o_ref[...]
        if final_norm:
            o_ref[...] = _rms(y, gf_ref[...])
        else:
            o_ref[...] = y


def _ffn(x, g, wi, wo, gf, final_norm, tm, tf=256):
    m, d = x.shape
    f = wo.shape[0]
    nf = f // tf
    kern = functools.partial(_ffn_kernel, final_norm=final_norm)
    outs = pl.pallas_call(
        kern,
        grid=(m // tm, nf),
        in_specs=[
            pl.BlockSpec((tm, d), lambda i, j: (i, 0)),
            pl.BlockSpec((1, d), lambda i, j: (0, 0)),
            pl.BlockSpec((d, tf), lambda i, j: (0, j)),
            pl.BlockSpec((d, tf), lambda i, j: (0, j + nf)),
            pl.BlockSpec((tf, d), lambda i, j: (j, 0)),
            pl.BlockSpec((1, d), lambda i, j: (0, 0)),
        ],
        out_specs=pl.BlockSpec((tm, d), lambda i, j: (i, 0)),
        out_shape=jax.ShapeDtypeStruct((m, d), F32),
        scratch_shapes=[pltpu.VMEM((tm, d), BF16)],
        compiler_params=_cparams(("parallel", "arbitrary")),
        name="ffn",
    )(x, g.reshape(1, d), wi, wi, wo, gf.reshape(1, d))
    return outs


def _proj_kernel(x_ref, g_ref, w_ref, o_ref, xn_ref):
    @pl.when(pl.program_id(1) == 0)
    def _():
        xn_ref[...] = _rms(x_ref[...], g_ref[...]).astype(BF16)

    o_ref[...] = _dot(xn_ref[...], w_ref[...])


def _proj(x, g, w, tm, tn=256):
    m, d = x.shape
    n = w.shape[1]
    return pl.pallas_call(
        _proj_kernel,
        grid=(m // tm, n // tn),
        in_specs=[
            pl.BlockSpec((tm, d), lambda i, j: (i, 0)),
            pl.BlockSpec((1, d), lambda i, j: (0, 0)),
            pl.BlockSpec((d, tn), lambda i, j: (0, j)),
        ],
        out_specs=pl.BlockSpec((tm, tn), lambda i, j: (i, j)),
        out_shape=jax.ShapeDtypeStruct((m, n), F32),
        scratch_shapes=[pltpu.VMEM((tm, d), BF16)],
        compiler_params=_cparams(("parallel", "arbitrary")),
        name="proj",
    )(x, g.reshape(1, d), w)


def _merge_kernel(x_ref, oa_ref, ob_ref, oc_ref, od_ref, g0_ref, g1_ref, g2_ref, g3_ref, wb_ref, wo_ref, o_ref):
    merged = None
    for i, (o_r, g_r) in enumerate(((oa_ref, g0_ref), (ob_ref, g1_ref), (oc_ref, g2_ref), (od_ref, g3_ref))):
        gate = 0.5 * jnp.tanh(0.5 * g_r[...]) + 0.5
        t = gate * _dot(o_r[...].astype(BF16), wb_ref[i])
        merged = t if merged is None else merged + t
    o_ref[...] = x_ref[...] + _dot(merged.astype(BF16), wo_ref[...])


def _merge(x, branches, h, wb, wo, tm):
    m, d = x.shape
    gate_specs = [pl.BlockSpec((tm, d), functools.partial(lambda i, k: (i, k), k=COL_GATE // d + k)) for k in range(N_BRANCH)]
    return pl.pallas_call(
        _merge_kernel,
        grid=(m // tm,),
        in_specs=[pl.BlockSpec((tm, d), lambda i: (i, 0))]
        + [pl.BlockSpec((tm, BRANCH_W), lambda i: (i, 0))] * N_BRANCH
        + gate_specs
        + [pl.BlockSpec((N_BRANCH, BRANCH_W, d), lambda i: (0, 0, 0)), pl.BlockSpec((d, d), lambda i: (0, 0))],
        out_specs=pl.BlockSpec((tm, d), lambda i: (i, 0)),
        out_shape=jax.ShapeDtypeStruct((m, d), F32),
        compiler_params=_cparams(("parallel",)),
        name="merge",
    )(x, *branches, h, h, h, h, wb, wo)


def _forget_kernel(ff_ref, bias_ref, lf_ref, hi_ref, mid_ref, lo_ref, *, t):
    x = ff_ref[...] + bias_ref[...]
    lf = jnp.minimum(x, 0.0) - _softplus_neg_abs(x)
    lf_ref[...] = lf
    w = 128
    tri = (lax.broadcasted_iota(jnp.int32, (w, w), 0) <= lax.broadcasted_iota(jnp.int32, (w, w), 1)).astype(F32)
    carry = jnp.zeros((x.shape[0], 1), F32)
    for c in range(t // w):
        cols = slice(c * w, (c + 1) * w)
        cs = _dot(lf[:, cols], tri, HIGHEST) + carry
        carry = cs[:, w - 1:w]
        hi = cs.astype(BF16)
        r1 = cs - hi.astype(F32)
        mid = r1.astype(BF16)
        hi_ref[:, cols] = hi
        mid_ref[:, cols] = mid
        lo_ref[:, cols] = (r1 - mid.astype(F32)).astype(BF16)


def _forget(ff_t, bias_rows):
    r, t = ff_t.shape
    term = jax.ShapeDtypeStruct((r, t), BF16)
    return pl.pallas_call(
        functools.partial(_forget_kernel, t=t),
        out_shape=(jax.ShapeDtypeStruct((r, t), F32), term, term, term),
        compiler_params=pltpu.CompilerParams(vmem_limit_bytes=VMEM_LIMIT),
        name="forget",
    )(ff_t, bias_rows)


PACK = 128


TERM_COLS = 16


def _pack_kernel(fq_ref, fk_ref, fv_ref, sq_ref, sk_ref, sv_ref, terms_ref,
                 fqa_ref, fka_ref, fva_ref, sqa_ref, ska_ref, sva_ref):
    w = HEADS * PACK
    r = lax.broadcasted_iota(jnp.int32, (BLK, w), 0)
    c = lax.broadcasted_iota(jnp.int32, (BLK, w), 1)
    place = ((c // PACK == r // HEAD_DIM) & (c % PACK == r % HEAD_DIM)).astype(BF16)
    e = lax.broadcasted_iota(jnp.int32, (TERM_COLS, w), 0)
    ce = lax.broadcasted_iota(jnp.int32, (TERM_COLS, w), 1)
    term_lane = (e % HEADS) * PACK + HEAD_DIM + e // HEADS
    real = e < 3 * HEADS
    place_q = ((ce == term_lane) & real).astype(BF16)
    place_k = -((ce == term_lane + 3) & real).astype(BF16)
    lane = lax.broadcasted_iota(jnp.int32, (1, w), 1) % PACK
    ones_q = ((lane >= HEAD_DIM + 3) & (lane < HEAD_DIM + 6)).astype(F32)
    ones_k = ((lane >= HEAD_DIM) & (lane < HEAD_DIM + 3)).astype(F32)
    ones_v = (lane == HEAD_DIM).astype(F32)
    terms = terms_ref[...]

    def put(x):
        return _dot(x.astype(BF16), place)

    fqa_ref[...] = (put(fq_ref[...] * QK_SCALE) + _dot(terms, place_q) + ones_q).astype(BF16)
    fka_ref[...] = (put(fk_ref[...]) + _dot(terms, place_k) + ones_k).astype(BF16)
    fva_ref[...] = (put(fv_ref[...]) + ones_v).astype(BF16)
    sqa_ref[...] = put(sq_ref[...] * QK_SCALE).astype(BF16)
    ska_ref[...] = put(sk_ref[...]).astype(BF16)
    sva_ref[...] = put(sv_ref[...]).astype(BF16)


def _pack_operands(h, terms, tm):
    m = h.shape[0]
    w = HEADS * PACK
    colblk = lambda c0: pl.BlockSpec((tm, BLK), functools.partial(lambda i, c: (i, c), c=c0 // BLK))
    cols = [COL_FOX, COL_FOX + BLK, COL_FOX + 2 * BLK, COL_SB, COL_SB + BLK, COL_SB + 2 * BLK]
    return pl.pallas_call(
        _pack_kernel,
        grid=(m // tm,),
        in_specs=[colblk(c0) for c0 in cols] + [pl.BlockSpec((tm, TERM_COLS), lambda i: (i, 0))],
        out_specs=[pl.BlockSpec((tm, w), lambda i: (i, 0))] * 6,
        out_shape=[jax.ShapeDtypeStruct((m, w), BF16)] * 6,
        compiler_params=_cparams(("parallel",)),
        name="pack",
    )(h, h, h, h, h, h, terms)


def _causal_pairs(nq, keys_descending):
    qs, ks = [], []
    for qi in range(nq):
        order = range(qi, -1, -1) if keys_descending else range(qi + 1)
        for ki in order:
            qs.append(qi)
            ks.append(ki)
    return jnp.asarray(np.array(qs, np.int32)), jnp.asarray(np.array(ks, np.int32))


def _fox_kernel(qtab_ref, ktab_ref, q_ref, k_ref, v_ref, o_ref, m_scr, acc_scr, *, tq):
    qi = qtab_ref[pl.program_id(1)]
    ki = ktab_ref[pl.program_id(1)]
    heads = range(HEADS)
    lanes = [slice(h * PACK, (h + 1) * PACK) for h in heads]

    @pl.when(ki == 0)
    def _():
        m_scr[...] = jnp.full_like(m_scr, NEG_BIG)
        acc_scr[...] = jnp.zeros_like(acc_scr)

    def block(diagonal):
        s = [_dot_nt(q_ref[0, :, lanes[h]], k_ref[0, :, lanes[h]]) for h in heads]
        if diagonal:
            mask = lax.broadcasted_iota(jnp.int32, (tq, tq), 1) <= lax.broadcasted_iota(jnp.int32, (tq, tq), 0)
            s = [jnp.where(mask, x, NEG_BIG) for x in s]
        m_old = [m_scr[h] for h in heads]
        m_new = [jnp.maximum(m_old[h], jnp.max(s[h], axis=-1, keepdims=True)) for h in heads]
        p = [jnp.exp(s[h] - m_new[h]).astype(BF16) for h in heads]
        pv = [_dot(p[h], v_ref[0, :, lanes[h]]) for h in heads]
        for h in heads:
            acc_scr[h] = jnp.exp(m_old[h] - m_new[h]) * acc_scr[h] + pv[h]
            m_scr[h] = m_new[h]

    pl.when(ki < qi)(functools.partial(block, False))

    @pl.when(ki == qi)
    def _():
        block(True)
        for h, sl in enumerate(_head_slices()):
            acc = acc_scr[h]
            o_ref[0, :, sl] = acc[:, :HEAD_DIM] / acc[:, HEAD_DIM:HEAD_DIM + 1]


def _pair_call(kern, name, qa, ka, va, tq, keys_descending, scratch):
    b, t, w = qa.shape
    qtab, ktab = _causal_pairs(t // tq, keys_descending)
    grid_spec = pltpu.PrefetchScalarGridSpec(
        num_scalar_prefetch=2,
        grid=(b, qtab.shape[0]),
        in_specs=[
            pl.BlockSpec((1, tq, w), lambda bi, p, qt, kt: (bi, qt[p], 0)),
            pl.BlockSpec((1, tq, w), lambda bi, p, qt, kt: (bi, kt[p], 0)),
            pl.BlockSpec((1, tq, w), lambda bi, p, qt, kt: (bi, kt[p], 0)),
        ],
        out_specs=pl.BlockSpec((1, tq, BLK), lambda bi, p, qt, kt: (bi, qt[p], 0)),
        scratch_shapes=scratch,
    )
    return pl.pallas_call(
        functools.partial(kern, tq=tq),
        grid_spec=grid_spec,
        out_shape=jax.ShapeDtypeStruct((b, t, BLK), F32),
        compiler_params=_cparams(("parallel", "arbitrary")),
        name=name,
    )(qtab, ktab, qa, ka, va)


def _fox_prompt(qa, ka, va, tq=512):
    scratch = [pltpu.VMEM((HEADS, tq, 1), F32), pltpu.VMEM((HEADS, tq, PACK), F32)]
    return _pair_call(_fox_kernel, "fox_prompt", qa, ka, va, tq, False, scratch)


def _strict_upper(n):
    return lax.broadcasted_iota(jnp.int32, (n, n), 0) > lax.broadcasted_iota(jnp.int32, (n, n), 1)


def _suffix_excl(x, u_bf16):
    hi = x.astype(BF16)
    lo = (x - hi.astype(F32)).astype(BF16)
    return _dot(hi, u_bf16) + _dot(lo, u_bf16)


def _sb_kernel(qtab_ref, ktab_ref, q_ref, k_ref, v_ref, o_ref, r_scr, acc_scr, *, tq):
    qi = qtab_ref[pl.program_id(1)]
    ki = ktab_ref[pl.program_id(1)]

    @pl.when(ki == qi)
    def _():
        r_scr[...] = jnp.zeros_like(r_scr)
        acc_scr[...] = jnp.zeros_like(acc_scr)

    heads = range(HEADS)
    lanes = [slice(h * PACK, (h + 1) * PACK) for h in heads]

    def block(diagonal):
        u = _strict_upper(tq).astype(BF16)
        z = [_dot_nt(q_ref[0, :, lanes[h]], k_ref[0, :, lanes[h]]) for h in heads]
        sp = [_softplus_neg_abs(x) for x in z]
        ls = [jnp.minimum(z[h], 0.0) - sp[h] for h in heads]
        l1m = [-jnp.maximum(z[h], 0.0) - sp[h] for h in heads]
        if diagonal:
            mask = lax.broadcasted_iota(jnp.int32, (tq, tq), 1) < lax.broadcasted_iota(jnp.int32, (tq, tq), 0)
            l1m = [jnp.where(mask, x, 0.0) for x in l1m]
        excl = [_suffix_excl(x, u) for x in l1m]
        a = [jnp.exp(ls[h] + excl[h] + r_scr[h]) for h in heads]
        if diagonal:
            a = [jnp.where(mask, x, 0.0) for x in a]
        av = [_dot(a[h].astype(BF16), v_ref[0, :, lanes[h]]) for h in heads]
        for h in heads:
            acc_scr[h] += av[h]
            r_scr[h] += jnp.sum(l1m[h], axis=-1, keepdims=True)

    pl.when(ki == qi)(functools.partial(block, True))
    pl.when(ki < qi)(functools.partial(block, False))

    @pl.when(ki == 0)
    def _():
        for h, sl in enumerate(_head_slices()):
            o_ref[0, :, sl] = acc_scr[h][:, :HEAD_DIM]


def _sb_prompt(qa, ka, va, tq=256):
    scratch = [pltpu.VMEM((HEADS, tq, 1), F32), pltpu.VMEM((HEADS, tq, PACK), F32)]
    return _pair_call(_sb_kernel, "sb_prompt", qa, ka, va, tq, True, scratch)


GROUP = 8
MAX_SAMPLE_PAGES_PER_STEP = 32


def _page_scores(k_refs, qx_scr, s_scr):
    for j, k_ref in enumerate(k_refs):
        for h, sl in enumerate(_head_slices()):
            s_scr[j * GROUP + h:j * GROUP + h + 1, :] = jnp.sum(k_ref[0, sl, :] * qx_scr[sl, :], axis=0, keepdims=True)


def _page_carries(r, tot, pp):
    carries = []
    for j in range(pp):
        carries.append(r)
        r = r + tot[j * GROUP:(j + 1) * GROUP]
    return jnp.concatenate(carries, axis=0), r


def _weighted_values(w_scr, v_refs, acc_scr, scale):
    for h, sl in enumerate(_head_slices()):
        acc = acc_scr[sl, :] if scale is None else acc_scr[sl, :] * scale[h:h + 1, :]
        for j, v_ref in enumerate(v_refs):
            acc = acc + w_scr[j * GROUP + h:j * GROUP + h + 1, :] * v_ref[0, sl, :]
        acc_scr[sl, :] = acc


def _fox_sample_kernel(pt_ref, q_ref, kn_ref, vn_ref, lfn_ref, *refs, pp):
    k_refs, v_refs, lf_refs = refs[:pp], refs[pp:2 * pp], refs[2 * pp:3 * pp]
    o_ref, qx_scr, s_scr, lf_scr, p_scr, m_scr, l_scr, r_scr, acc_scr = refs[3 * pp:]
    step = pl.program_id(1)

    @pl.when(step == 0)
    def _():
        qcol = q_ref[0] * QK_SCALE
        qx_scr[...] = jnp.broadcast_to(qcol, qx_scr.shape)
        s_scr[...] = jnp.zeros_like(s_scr)
        lf_scr[...] = jnp.zeros_like(lf_scr)
        m_scr[...] = jnp.zeros_like(m_scr)
        own = qcol * kn_ref[0]
        for h, sl in enumerate(_head_slices()):
            m_scr[h:h + 1, :] = jnp.sum(own[sl], axis=0, keepdims=True)
        l_scr[...] = jnp.ones_like(l_scr)
        r_scr[...] = lfn_ref[0]
        lane = lax.broadcasted_iota(jnp.int32, acc_scr.shape, 1)
        acc_scr[...] = jnp.where(lane == 0, vn_ref[0], 0.0)

    for j, lf_ref in enumerate(lf_refs):
        lf_scr[j * GROUP:j * GROUP + HEADS, :] = lf_ref[0]
    _page_scores(k_refs, qx_scr, s_scr)
    lf_all = lf_scr[...]
    excl = _dot(lf_all, _strict_upper(PAGE_SIZE).astype(F32), HIGHEST)
    carries, r_out = _page_carries(r_scr[...], jnp.sum(lf_all, axis=-1, keepdims=True), pp)
    r_scr[...] = r_out
    s_all = s_scr[...] + excl + carries
    row_max = jnp.max(s_all, axis=-1, keepdims=True)
    m_old = m_scr[...]
    m_new = m_old
    for j in range(pp):
        m_new = jnp.maximum(m_new, row_max[j * GROUP:(j + 1) * GROUP])
    alpha = jnp.exp(m_old - m_new)
    p_all = jnp.exp(s_all - jnp.concatenate([m_new] * pp, axis=0))
    p_scr[...] = p_all
    row_sum = jnp.sum(p_all, axis=-1, keepdims=True)
    l_new = alpha * l_scr[...]
    for j in range(pp):
        l_new = l_new + row_sum[j * GROUP:(j + 1) * GROUP]
    l_scr[...] = l_new
    m_scr[...] = m_new
    _weighted_values(p_scr, v_refs, acc_scr, alpha)

    @pl.when(step == pl.num_programs(1) - 1)
    def _():
        l = l_scr[...]
        for h, sl in enumerate(_head_slices()):
            o_ref[0, sl, :] = jnp.sum(acc_scr[sl, :], axis=-1, keepdims=True) / l[h:h + 1, :]


def _sb_sample_kernel(pt_ref, q_ref, *refs, pp):
    k_refs, v_refs = refs[:pp], refs[pp:2 * pp]
    o_ref, qx_scr, z_scr, a_scr, r_scr, acc_scr = refs[2 * pp:]
    step = pl.program_id(1)

    @pl.when(step == 0)
    def _():
        qx_scr[...] = jnp.broadcast_to(q_ref[0] * QK_SCALE, qx_scr.shape)
        z_scr[...] = jnp.zeros_like(z_scr)
        r_scr[...] = jnp.zeros_like(r_scr)
        acc_scr[...] = jnp.zeros_like(acc_scr)

    _page_scores(k_refs, qx_scr, z_scr)
    z = z_scr[...]
    sp = _softplus_neg_abs(z)
    ls = jnp.minimum(z, 0.0) - sp
    l1m = -jnp.maximum(z, 0.0) - sp
    excl = _suffix_excl(l1m, _strict_upper(PAGE_SIZE).astype(BF16))
    carries, r_out = _page_carries(r_scr[...], jnp.sum(l1m, axis=-1, keepdims=True), pp)
    r_scr[...] = r_out
    a_scr[...] = jnp.exp(ls + excl + carries)
    _weighted_values(a_scr, v_refs, acc_scr, None)

    @pl.when(step == pl.num_programs(1) - 1)
    def _():
        o_ref[...] = jnp.sum(acc_scr[...], axis=-1, keepdims=True)[None]


def _pages_per_step(n_pages):
    pp = math.gcd(n_pages, MAX_SAMPLE_PAGES_PER_STEP)
    return pp


def _page_specs(page_table, base, pp, block):
    n_pages = page_table.shape[1]

    def spec(j):
        return pl.BlockSpec(block, lambda bi, st, pt: (base + pt[bi, n_pages - 1 - (st * pp + j)], 0, 0))

    return [spec(j) for j in range(pp)]


def _per_sample(bi, st, pt):
    return (bi, 0, 0)


def _fox_sample(q, k_new, v_new, lf_new, cache_kt, cache_vt, cache_lft, page_table, base):
    b, n_pages = page_table.shape
    pp = _pages_per_step(n_pages)
    col = lambda x: x[:, :, None]
    lfn = col(jnp.concatenate([lf_new, jnp.zeros((b, GROUP - HEADS), F32)], axis=1))
    kv_block = (1, BLK, PAGE_SIZE)
    grid_spec = pltpu.PrefetchScalarGridSpec(
        num_scalar_prefetch=1,
        grid=(b, n_pages // pp),
        in_specs=[pl.BlockSpec((1, BLK, 1), _per_sample)] * 3 + [pl.BlockSpec((1, GROUP, 1), _per_sample)]
        + _page_specs(page_table, base, pp, kv_block) + _page_specs(page_table, base, pp, kv_block)
        + _page_specs(page_table, base, pp, (1, HEADS, PAGE_SIZE)),
        out_specs=pl.BlockSpec((1, BLK, 1), _per_sample),
        scratch_shapes=[pltpu.VMEM((BLK, PAGE_SIZE), F32)] + [pltpu.VMEM((pp * GROUP, PAGE_SIZE), F32)] * 3
        + [pltpu.VMEM((GROUP, 1), F32)] * 3 + [pltpu.VMEM((BLK, PAGE_SIZE), F32)],
    )
    out = pl.pallas_call(
        functools.partial(_fox_sample_kernel, pp=pp),
        grid_spec=grid_spec,
        out_shape=jax.ShapeDtypeStruct((b, BLK, 1), F32),
        compiler_params=_cparams(("parallel", "arbitrary")),
        name="fox_sample",
    )(page_table, col(q), col(k_new), col(v_new), lfn, *([cache_kt] * pp), *([cache_vt] * pp), *([cache_lft] * pp))
    return out[:, :, 0]


def _sb_sample(q, cache_kt, cache_vt, page_table, base):
    b, n_pages = page_table.shape
    pp = _pages_per_step(n_pages)
    kv_block = (1, BLK, PAGE_SIZE)
    grid_spec = pltpu.PrefetchScalarGridSpec(
        num_scalar_prefetch=1,
        grid=(b, n_pages // pp),
        in_specs=[pl.BlockSpec((1, BLK, 1), _per_sample)]
        + _page_specs(page_table, base, pp, kv_block) + _page_specs(page_table, base, pp, kv_block),
        out_specs=pl.BlockSpec((1, BLK, 1), _per_sample),
        scratch_shapes=[pltpu.VMEM((BLK, PAGE_SIZE), F32)] + [pltpu.VMEM((pp * GROUP, PAGE_SIZE), F32)] * 2
        + [pltpu.VMEM((GROUP, 1), F32), pltpu.VMEM((BLK, PAGE_SIZE), F32)],
    )
    out = pl.pallas_call(
        functools.partial(_sb_sample_kernel, pp=pp),
        grid_spec=grid_spec,
        out_shape=jax.ShapeDtypeStruct((b, BLK, 1), F32),
        compiler_params=_cparams(("parallel", "arbitrary")),
        name="sb_sample",
    )(page_table, q[:, :, None], *([cache_kt] * pp), *([cache_vt] * pp))
    return out[:, :, 0]


def _rot_half(x):
    half = HEAD_DIM // 2
    lane = lax.broadcasted_iota(jnp.int32, x.shape, 1) % HEAD_DIM
    n = x.shape[1]
    return jnp.where(lane < half, -pltpu.roll(x, n - half, 1), pltpu.roll(x, half, 1))


def _ret_kernel(q_ref, k_ref, v_ref, g_ref, cos_ref, sin_ref, dmask_ref, xi_ref, zeta_ref, cd_ref, gnw_ref, s0_ref,
                o_ref, st_ref, s_scr, *, seqs):
    c = pl.program_id(1)

    @pl.when(c == 0)
    def _():
        s_scr[...] = s0_ref[...]

    cos = cos_ref[...]
    sin = sin_ref[...]
    gnw = gnw_ref[...]
    qr, kr, v, gate = [], [], [], []
    for n in range(seqs):
        q = q_ref[n]
        k = k_ref[n]
        g = g_ref[n]
        qr.append((q * cos + _rot_half(q) * sin).astype(BF16))
        kr.append((k * cos + _rot_half(k) * sin) * QK_SCALE)
        v.append(v_ref[n].astype(BF16))
        gate.append(g * jax.nn.sigmoid(g))
    pieces = [(n, h, sl) for n in range(seqs) for h, sl in enumerate(_head_slices())]
    ids = range(len(pieces))
    qh = [qr[n][:, sl] for n, h, sl in pieces]
    kh = [kr[n][:, sl] for n, h, sl in pieces]
    vh = [v[n][:, sl] for n, h, sl in pieces]
    s = [s_scr[n, h] for n, h, sl in pieces]
    att = [(_dot_nt(qh[i], kh[i].astype(BF16)) * dmask_ref[pieces[i][1]]).astype(BF16) for i in ids]
    cross = [_dot(qh[i], s[i].astype(BF16)) * xi_ref[pieces[i][1]] for i in ids]
    o = [_dot(att[i], vh[i]) + cross[i] for i in ids]
    kv = [_dot_tn((kh[i] * zeta_ref[pieces[i][1]]).astype(BF16), vh[i]) for i in ids]
    for i, (n, h, sl) in enumerate(pieces):
        s_scr[n, h] = cd_ref[h] * s[i] + kv[i]
    mu = [jnp.mean(x, axis=-1, keepdims=True) for x in o]
    d = [o[i] - mu[i] for i in ids]
    var = [jnp.mean(x * x, axis=-1, keepdims=True) for x in d]
    for i, (n, h, sl) in enumerate(pieces):
        o_ref[n, :, sl] = gate[n][:, sl] * (d[i] * lax.rsqrt(var[i] + GN_EPS) * gnw[:, sl])

    @pl.when(c == pl.num_programs(1) - 1)
    def _():
        st_ref[...] = s_scr[...]


def _ret_tables(chunk, n_valid):
    lg = np.log(1.0 - 2.0 ** (-5.0 - np.arange(HEADS, dtype=np.float64)))
    n = np.arange(chunk, dtype=np.float64)
    diff = n[:, None] - n[None, :]
    valid = (n < n_valid).astype(np.float64)
    dmask = np.where(diff[None] >= 0, np.exp(np.maximum(diff, 0.0)[None] * lg[:, None, None]), 0.0)
    dmask = dmask * valid[None, :, None] * valid[None, None, :]
    xi = np.exp((n[None, :] + 1.0) * lg[:, None]) * valid[None, :]
    zeta = np.exp((n_valid - 1.0 - n)[None, :] * lg[:, None]) * valid[None, :]
    cd = np.exp(n_valid * lg)
    f = lambda a: jnp.asarray(a.astype(np.float32))
    return f(dmask), f(xi[:, :, None]), f(zeta[:, :, None]), f(cd[:, None, None])


def _rope_tables(pos):
    half = HEAD_DIM // 2
    inv = ROPE_BASE ** (-jnp.arange(half, dtype=F32) / half)
    ang = pos.astype(F32)[:, None] * inv[None, :]
    cos = jnp.tile(jnp.cos(ang), (1, 2 * HEADS))
    sin = jnp.tile(jnp.sin(ang), (1, 2 * HEADS))
    return cos, sin


def _retention(h3, col0, cos, sin, tables, gnw, s0, chunk):
    b, t, _ = h3.shape
    nc = t // chunk
    cb = col0 // BLK
    dmask, xi, zeta, cd = tables
    nq = math.gcd(b, SCAN_SEQS_PER_STEP)

    def tok(k):
        return pl.BlockSpec((nq, chunk, BLK), functools.partial(lambda bi, c, k: (bi, c, k), k=cb + k))

    const3 = lambda bi, c: (0, 0, 0)
    return pl.pallas_call(
        functools.partial(_ret_kernel, seqs=nq),
        grid=(b // nq, nc),
        in_specs=[tok(0), tok(1), tok(2), tok(3),
                  pl.BlockSpec((chunk, BLK), lambda bi, c: (c, 0)),
                  pl.BlockSpec((chunk, BLK), lambda bi, c: (c, 0)),
                  pl.BlockSpec((HEADS, chunk, chunk), const3),
                  pl.BlockSpec((HEADS, chunk, 1), const3),
                  pl.BlockSpec((HEADS, chunk, 1), const3),
                  pl.BlockSpec((HEADS, 1, 1), const3),
                  pl.BlockSpec((1, BLK), lambda bi, c: (0, 0)),
                  pl.BlockSpec((nq, HEADS, HEAD_DIM, HEAD_DIM), lambda bi, c: (bi, 0, 0, 0))],
        out_specs=[pl.BlockSpec((nq, chunk, BLK), lambda bi, c: (bi, c, 0)),
                   pl.BlockSpec((nq, HEADS, HEAD_DIM, HEAD_DIM), lambda bi, c: (bi, 0, 0, 0))],
        out_shape=[jax.ShapeDtypeStruct((b, t, BLK), F32),
                   jax.ShapeDtypeStruct((b, HEADS, HEAD_DIM, HEAD_DIM), F32)],
        scratch_shapes=[pltpu.VMEM((nq, HEADS, HEAD_DIM, HEAD_DIM), F32)],
        compiler_params=_cparams(("parallel", "arbitrary")),
        name="retention",
    )(h3, h3, h3, h3, cos, sin, dmask, xi, zeta, cd, gnw.reshape(1, BLK), s0)


def _head_sum_matrix():
    i = lax.broadcasted_iota(jnp.int32, (BLK, BLK), 0) // HEAD_DIM
    j = lax.broadcasted_iota(jnp.int32, (BLK, BLK), 1) // HEAD_DIM
    return (i == j).astype(F32)


def _rwkv_prep_kernel(h0_ref, h1_ref, h2_ref, h3_ref, c0_ref, c1_ref, c2_ref, c3_ref, shift_ref,
                      mu_ref, w0_ref, a0_ref, kk_w_ref, ka_ref, rk_ref, wup_ref, aup_ref, gup_ref,
                      r_ref, k_ref, v_ref, logd_ref, kk_ref, b_ref, g_ref, bonus_ref, *, tm, seq_len, n_valid):
    hr = jnp.concatenate([h0_ref[...], h1_ref[...], h2_ref[...], h3_ref[...]], axis=1)
    before = jnp.concatenate([c0_ref[...], c1_ref[...], c2_ref[...], c3_ref[...]], axis=1)[7:8]
    row = lax.broadcasted_iota(jnp.int32, (tm, 1), 0)
    pos = (pl.program_id(0) * tm + row) & (seq_len - 1)
    prev = jnp.where(row == 0, before, pltpu.roll(hr, 1, 0))
    prev = jnp.where(pos == 0, shift_ref[0], prev)
    valid = (pos < n_valid).astype(F32)
    xm = hr + (prev - hr) * mu_ref[...]
    r = xm[:, 0:BLK]
    k = xm[:, BLK:2 * BLK]
    v = xm[:, 2 * BLK:3 * BLK]
    o = 3 * BLK
    wd = xm[:, o:o + DECAY_LORA]
    ad = xm[:, o + DECAY_LORA:o + DECAY_LORA + AAA_LORA]
    gd = xm[:, o + DECAY_LORA + AAA_LORA:]
    wpre = w0_ref[...] + _dot(jnp.tanh(wd).astype(BF16), wup_ref[...])
    w = jnp.minimum(wpre, 0.0) - _softplus_neg_abs(wpre) - 0.5
    a = jax.nn.sigmoid(a0_ref[...] + _dot(ad.astype(BF16), aup_ref[...]))
    g = _dot(jax.nn.sigmoid(gd).astype(BF16), gup_ref[...])
    hs = _head_sum_matrix()
    kk = k * kk_w_ref[...]
    kk = kk * lax.rsqrt(_dot(kk * kk, hs, HIGHEST) + 1e-12)
    kmod = k * (1.0 + (a - 1.0) * ka_ref[...])
    bonus = _dot(r * kmod * rk_ref[...], hs, HIGHEST) * v
    r_ref[...] = r * valid
    k_ref[...] = kmod * valid
    v_ref[...] = v * valid
    logd_ref[...] = -jnp.exp(w) * valid
    kk_ref[...] = kk * valid
    b_ref[...] = kk * a * valid
    g_ref[...] = g
    bonus_ref[...] = bonus


def _rwkv_prep(h, shift0, seq_len, n_valid, mu, w0, a0, k_k, k_a, r_k, w_up, a_up, g_up, tm):
    m = h.shape[0]
    assert seq_len & (seq_len - 1) == 0 and seq_len % tm == 0 and tm % 8 == 0
    cb = COL_RWKV // BLK
    row = lambda n: pl.BlockSpec((1, n), lambda i: (0, 0))
    full = lambda a: pl.BlockSpec(a.shape, lambda i: (0, 0))
    tokb = pl.BlockSpec((tm, BLK), lambda i: (i, 0))
    cur = [pl.BlockSpec((tm, BLK), functools.partial(lambda i, c: (i, c), c=cb + c)) for c in range(4)]
    before = [pl.BlockSpec((8, BLK), functools.partial(lambda i, c: (jnp.maximum(i * (tm // 8) - 1, 0), c), c=cb + c))
              for c in range(4)]
    return pl.pallas_call(
        functools.partial(_rwkv_prep_kernel, tm=tm, seq_len=seq_len, n_valid=n_valid),
        grid=(m // tm,),
        in_specs=cur + before + [pl.BlockSpec((1, 1, RWKV_IN), lambda i: (i * tm // seq_len, 0, 0)),
                                 row(RWKV_IN), row(BLK), row(BLK), row(BLK), row(BLK), row(BLK),
                                 full(w_up), full(a_up), full(g_up)],
        out_specs=[tokb] * 8,
        out_shape=[jax.ShapeDtypeStruct((m, BLK), F32)] * 8,
        compiler_params=_cparams(("parallel",)),
        name="rwkv_prep",
    )(h, h, h, h, h, h, h, h, shift0[:, None, :], mu.reshape(1, -1), w0.reshape(1, -1), a0.reshape(1, -1),
      k_k.reshape(1, -1), k_a.reshape(1, -1), r_k.reshape(1, -1), w_up, a_up, g_up)


def _rwkv_chunk_kernel(r_ref, k_ref, v_ref, logd_ref, kk_ref, b_ref, r2_ref, y0_ref, gt_ref, ht_ref, *, chunk, per_step):
    c = chunk
    n = HEADS * c
    ri = lax.broadcasted_iota(jnp.int32, (c, c), 0)
    ci = lax.broadcasted_iota(jnp.int32, (c, c), 1)
    cum = (ci <= ri).astype(F32)
    rr = lax.broadcasted_iota(jnp.int32, (n, n), 0)
    cc = lax.broadcasted_iota(jnp.int32, (n, n), 1)
    assert c & (c - 1) == 0
    strict = (cc & (c - 1)) < (rr & (c - 1))
    incl = (cc & (c - 1)) <= (rr & (c - 1))
    eye = (rr == cc).astype(F32)
    eye_w = (lax.broadcasted_iota(jnp.int32, (BLK, BLK), 0) == lax.broadcasted_iota(jnp.int32, (BLK, BLK), 1)).astype(F32)
    lane_head = lax.broadcasted_iota(jnp.int32, (c, BLK), 1) // HEAD_DIM

    def stack(x):
        return jnp.concatenate([jnp.where(lane_head == h, x, 0.0) for h in range(HEADS)], axis=0)

    def unstack(xw):
        out = xw[0:c]
        for h in range(1, HEADS):
            out = out + xw[h * c:(h + 1) * c]
        return out

    chunks = range(per_step)
    rows = [slice(s * c, (s + 1) * c) for s in chunks]
    lk, lr, rb, rk, vw, be_w, ke_w, w_c = [], [], [], [], [], [], [], []
    for s in chunks:
        logd = logd_ref[0, rows[s], :]
        logw = _dot(cum, logd, HIGHEST)
        logw_c = logw[c - 1:c, :]
        e_neg = jnp.exp(-logw)
        e_end = jnp.exp(logw_c - logw)
        b = b_ref[0, rows[s], :]
        k = k_ref[0, rows[s], :]
        lk.append(stack(kk_ref[0, rows[s], :] * jnp.exp(logw - logd)))
        lr.append(stack(r_ref[0, rows[s], :] * jnp.exp(logw)))
        rb.append(stack(b * e_neg))
        rk.append(stack(k * e_neg))
        be_w.append(stack(b * e_end))
        ke_w.append(stack(k * e_end))
        vw.append(stack(v_ref[0, rows[s], :]))
        w_c.append(jnp.exp(logw_c))

    lhs = [jnp.concatenate([lk[s], lr[s]], axis=0) for s in chunks]
    ab_mb = [_mm(lhs[s], rb[s], _NT) for s in chunks]
    ak_mk = [_mm(lhs[s], rk[s], _NT) for s in chunks]
    m_b = [jnp.where(incl, ab_mb[s][n:], 0.0) for s in chunks]
    am_k = [jnp.concatenate([jnp.where(strict, ak_mk[s][:n], 0.0), jnp.where(incl, ak_mk[s][n:], 0.0)], axis=0)
            for s in chunks]
    npow = [jnp.where(strict, -ab_mb[s][:n], 0.0) for s in chunks]
    tinv = [eye + npow[s] for s in chunks]
    for _ in range(int(math.log2(c)) - 1):
        npow = [_mm(npow[s], npow[s], _NN) for s in chunks]
        tinv = [tinv[s] + _mm(tinv[s], npow[s], _NN) for s in chunks]
    akv_mkv = [_mm(am_k[s], vw[s], _NN) for s in chunks]
    pq = [_mm(tinv[s], jnp.concatenate([lk[s], akv_mkv[s][:n]], axis=1), _NN) for s in chunks]
    mb_pq = [_mm(m_b[s], pq[s], _NN) for s in chunks]
    for s in chunks:
        r2_ref[0, rows[s], :] = unstack(lr[s] - mb_pq[s][:, :BLK])
        y0_ref[0, rows[s], :] = unstack(akv_mkv[s][n:] - mb_pq[s][:, BLK:])
    for s in chunks:
        p_w, q_w = pq[s][:, :BLK], pq[s][:, BLK:]
        gt_ref[0, s] = eye_w * w_c[s] - _mm(be_w[s], p_w, _TN)
        ht_ref[0, s] = _mm(ke_w[s], vw[s], _TN) - _mm(be_w[s], q_w, _TN)


def _chunks_per_step(nc):
    return math.gcd(nc, RWKV_CHUNKS_PER_STEP)


def _rwkv_chunks(r, k, v, logd, kk, b, chunk):
    bsz, t, _ = r.shape
    nc = t // chunk
    ps = _chunks_per_step(nc)
    tok = pl.BlockSpec((1, ps * chunk, BLK), lambda bi, c: (bi, c, 0))
    mat = pl.BlockSpec((1, ps, BLK, BLK), lambda bi, c: (bi, c, 0, 0))
    mat_shape = jax.ShapeDtypeStruct((bsz, nc, BLK, BLK), F32)
    return pl.pallas_call(
        functools.partial(_rwkv_chunk_kernel, chunk=chunk, per_step=ps),
        grid=(bsz, nc // ps),
        in_specs=[tok] * 6,
        out_specs=[tok, tok, mat, mat],
        out_shape=[jax.ShapeDtypeStruct((bsz, t, BLK), F32)] * 2 + [mat_shape] * 2,
        compiler_params=_cparams(("parallel", "parallel")),
        name="rwkv_chunks",
    )(r, k, v, logd, kk, b)


def _rwkv_scan_kernel(r2_ref, y0_ref, gt_ref, ht_ref, gate_ref, bonus_ref, lnw_ref, lnb_ref, s0_ref,
                      o_ref, st_ref, s_scr, *, chunk, per_step, seqs):
    step = pl.program_id(1)

    @pl.when(step == 0)
    def _():
        s_scr[...] = jnp.zeros_like(s_scr)
        for q in range(seqs):
            for h, sl in enumerate(_head_slices()):
                s_scr[q, sl, sl] = s0_ref[q, h]

    lnw = lnw_ref[...]
    lnb = lnb_ref[...]
    sts = [[s_scr[q] for q in range(seqs)]]
    for i in range(per_step):
        sts.append([_mm(gt_ref[q, i], sts[i][q], _NN) + ht_ref[q, i] for q in range(seqs)])
    for q in range(seqs):
        s_scr[q] = sts[per_step][q]
    rows = [slice(i * chunk, (i + 1) * chunk) for i in range(per_step)]
    pairs = [(q, i) for q in range(seqs) for i in range(per_step)]
    ys = [_mm(r2_ref[q, rows[i], :], sts[i][q], _NN) + y0_ref[q, rows[i], :] for q, i in pairs]
    pieces = [(n, q, i, sl) for n, (q, i) in enumerate(pairs) for sl in _head_slices()]
    yh = [ys[n][:, sl] for n, q, i, sl in pieces]
    mu = [jnp.mean(x, axis=-1, keepdims=True) for x in yh]
    d = [yh[m] - mu[m] for m in range(len(pieces))]
    var = [jnp.mean(x * x, axis=-1, keepdims=True) for x in d]
    for m, (n, q, i, sl) in enumerate(pieces):
        yn = d[m] * lax.rsqrt(var[m] + RWKV_GN_EPS) * lnw[:, sl] + lnb[:, sl]
        o_ref[q, rows[i], sl] = (yn + bonus_ref[q, rows[i], sl]) * gate_ref[q, rows[i], sl]

    @pl.when(step == pl.num_programs(1) - 1)
    def _():
        for q in range(seqs):
            for h, sl in enumerate(_head_slices()):
                st_ref[q, h] = sts[per_step][q][sl, sl]


SCAN_SEQS_PER_STEP = 2


def _rwkv_scan(r2, y0, gt, ht, gate, bonus, ln_w, ln_b, st0, chunk):
    bsz, t, _ = r2.shape
    nc = t // chunk
    ps = _chunks_per_step(nc)
    nq = math.gcd(bsz, SCAN_SEQS_PER_STEP)
    tok = pl.BlockSpec((nq, ps * chunk, BLK), lambda bi, c: (bi, c, 0))
    mat = pl.BlockSpec((nq, ps, BLK, BLK), lambda bi, c: (bi, c, 0, 0))
    state = pl.BlockSpec((nq, HEADS, HEAD_DIM, HEAD_DIM), lambda bi, c: (bi, 0, 0, 0))
    row = pl.BlockSpec((1, BLK), lambda bi, c: (0, 0))
    return pl.pallas_call(
        functools.partial(_rwkv_scan_kernel, chunk=chunk, per_step=ps, seqs=nq),
        grid=(bsz // nq, nc // ps),
        in_specs=[tok, tok, mat, mat, tok, tok, row, row, state],
        out_specs=[tok, state],
        out_shape=[jax.ShapeDtypeStruct((bsz, t, BLK), F32),
                   jax.ShapeDtypeStruct((bsz, HEADS, HEAD_DIM, HEAD_DIM), F32)],
        scratch_shapes=[pltpu.VMEM((nq, BLK, BLK), F32)],
        compiler_params=_cparams(("parallel", "arbitrary")),
        name="rwkv_scan",
    )(r2, y0, gt, ht, gate, bonus, ln_w.reshape(1, BLK), ln_b.reshape(1, BLK), st0)


def _rwkv(h3, shift0, s0, n_valid, p, chunk, tm):
    bsz, t, cols = h3.shape
    outs = _rwkv_prep(h3.reshape(bsz * t, cols), shift0, t, n_valid,
                      p["mu"], p["w0"], p["a0"], p["k_k"], p["k_a"], p["r_k"], p["w_up"], p["a_up"], p["g_up"], tm)
    r, k, v, logd, kk, b, gate, bonus = [o.reshape(bsz, t, BLK) for o in outs]
    r2, y0, gt, ht = _rwkv_chunks(r, k, v, logd, kk, b, chunk)
    out, st = _rwkv_scan(r2, y0, gt, ht, gate, bonus, p["ln_w"], p["ln_b"], jnp.swapaxes(s0, 2, 3), chunk)
    return out, jnp.swapaxes(st, 2, 3)


def _reorder_w_in(w):
    fox_end = 3 * BRANCH_W
    ff_end = fox_end + HEADS
    gate_start = w.shape[-1] - N_BRANCH * D_MODEL
    pad = jnp.zeros(w.shape[:-1] + (BRANCH_W - HEADS,), w.dtype)
    out = jnp.concatenate([w[..., gate_start:], w[..., :fox_end], w[..., ff_end:gate_start],
                           w[..., fox_end:ff_end], pad], axis=-1)
    assert out.shape[-1] == PROJ_COLS
    return out.astype(BF16)


def _heads4(x):
    return x.reshape(x.shape[:-1] + (HEADS, HEAD_DIM))


def kernel(x_prompt, x_sample, cache_fox_k, cache_fox_v, cache_fox_logf, cache_sb_k, cache_sb_v, state_ret, state_rwkv, state_rwkv_shift, page_table, w_in, fox_f_bias, ret_gn_w, rwkv_mu, rwkv_w0, rwkv_w_up, rwkv_a0, rwkv_a_up, rwkv_g_up, rwkv_k_k, rwkv_k_a, rwkv_r_k, rwkv_ln_w, rwkv_ln_b, w_branch, w_out, norm_ffn1, ffn1_w_in, ffn1_w_out, norm_mix, norm_ffn2, ffn2_w_in, ffn2_w_out, norm_final):
    depth = w_in.shape[0]
    bp, t, d = x_prompt.shape
    bs = x_sample.shape[0]
    n_pool = cache_fox_k.shape[1]
    past_len = page_table.shape[1] * PAGE_SIZE

    w_in_r = _reorder_w_in(w_in)
    f1i, f1o, f2i, f2o = (a.astype(BF16) for a in (ffn1_w_in, ffn1_w_out, ffn2_w_in, ffn2_w_out))
    wbr = w_branch.astype(BF16)
    wo = w_out.astype(BF16)
    w_up, a_up, g_up = rwkv_w_up.astype(BF16), rwkv_a_up.astype(BF16), rwkv_g_up.astype(BF16)

    pool = lambda c: jnp.transpose(c, (0, 1, 3, 4, 2)).reshape(depth * n_pool, BLK, PAGE_SIZE)
    ck_fox, cv_fox, ck_sb, cv_sb = pool(cache_fox_k), pool(cache_fox_v), pool(cache_sb_k), pool(cache_sb_v)
    lf_t = jnp.swapaxes(cache_fox_logf, 2, 3).reshape(depth * n_pool, HEADS, PAGE_SIZE)

    cos_p, sin_p = _rope_tables(jnp.arange(t))
    s_pad = 8
    cos_s, sin_s = _rope_tables(past_len + jnp.arange(s_pad))
    ret_tab_p = _ret_tables(RET_CHUNK, RET_CHUNK)
    ret_tab_s = _ret_tables(s_pad, 1)
    zeros_state = jnp.zeros((bp, HEADS, HEAD_DIM, HEAD_DIM), F32)

    xp = x_prompt.reshape(bp * t, d)
    xs = x_sample.reshape(bs, d)
    new_p = [[] for _ in range(8)]
    new_s = [[] for _ in range(8)]
    tm_p = 1024
    tm_wide = min(2 * tm_p, bp * t)

    for l in range(depth):
        last = l == depth - 1
        rw = dict(mu=rwkv_mu[l], w0=rwkv_w0[l], a0=rwkv_a0[l], k_k=rwkv_k_k[l], k_a=rwkv_k_a[l], r_k=rwkv_r_k[l],
                  w_up=w_up[l], a_up=a_up[l], g_up=g_up[l], ln_w=rwkv_ln_w[l], ln_b=rwkv_ln_b[l])

        xp = _ffn(xp, norm_ffn1[l], f1i[l], f1o[l], norm_final, False, tm_wide)
        h = _proj(xp, norm_mix[l], w_in_r[l], tm_wide)
        h3 = h.reshape(bp, t, PROJ_COLS)
        ff_t = jnp.swapaxes(h3[:, :, COL_FORGET:COL_FORGET + HEADS], 1, 2).reshape(bp * HEADS, t)
        lf_rows, *cum_terms = _forget(ff_t, jnp.tile(fox_f_bias[l], bp)[:, None])
        by_token = lambda rows: jnp.swapaxes(rows.reshape(bp, HEADS, t), 1, 2)
        lf = by_token(lf_rows)
        terms = jnp.concatenate([by_token(c) for c in cum_terms]
                                + [jnp.zeros((bp, t, TERM_COLS - 3 * HEADS), BF16)], axis=-1)
        packed = _pack_operands(h, terms.reshape(bp * t, TERM_COLS), tm_p)
        fqa, fka, fva, sqa, ska, sva = [a.reshape(bp, t, HEADS * PACK) for a in packed]
        o_a = _fox_prompt(fqa, fka, fva)
        o_b, ret_p = _retention(h3, COL_RET, cos_p, sin_p, ret_tab_p, ret_gn_w[l], zeros_state, RET_CHUNK)
        o_c, rw_p = _rwkv(h3, jnp.zeros((bp, RWKV_IN), F32), zeros_state, t, rw, RWKV_CHUNK, tm_p)
        o_d = _sb_prompt(sqa, ska, sva)
        flat = lambda o: o.reshape(bp * t, BLK)
        xp = _merge(xp, [flat(o_a), flat(o_b), flat(o_c), flat(o_d)], h, wbr[l], wo[l], 512)
        xp = _ffn(xp, norm_ffn2[l], f2i[l], f2o[l], norm_final, last, tm_wide)
        for i, a in enumerate((_heads4(h3[:, :, COL_FOX + BLK:COL_FOX + 2 * BLK]),
                               _heads4(h3[:, :, COL_FOX + 2 * BLK:COL_FOX + 3 * BLK]), lf,
                               _heads4(h3[:, :, COL_SB + BLK:COL_SB + 2 * BLK]),
                               _heads4(h3[:, :, COL_SB + 2 * BLK:COL_SB + 3 * BLK]), ret_p, rw_p,
                               h3[:, -1, COL_RWKV:COL_RWKV + RWKV_IN])):
            new_p[i].append(a)

        xs = _ffn(xs, norm_ffn1[l], f1i[l], f1o[l], norm_final, False, bs)
        hs = _proj(xs, norm_mix[l], w_in_r[l], bs)
        col = lambda c0, n=BLK: hs[:, c0:c0 + n]
        ff_s = jnp.swapaxes(col(COL_FORGET, HEADS), 0, 1)
        ff_s = jnp.concatenate([ff_s, jnp.zeros((8 - HEADS, bs), F32)], axis=0)
        bias_s = jnp.concatenate([fox_f_bias[l], jnp.zeros((8 - HEADS,), F32)])[:, None]
        lf_s_rows = _forget(jnp.pad(ff_s, ((0, 0), (0, 128 - bs))), bias_s)[0]
        lf_s = jnp.swapaxes(lf_s_rows[:HEADS, :bs], 0, 1)
        o_a = _fox_sample(col(COL_FOX), col(COL_FOX + BLK), col(COL_FOX + 2 * BLK), lf_s,
                          ck_fox, cv_fox, lf_t, page_table, l * n_pool)
        hs_pad = jnp.pad(hs[:, None, :], ((0, 0), (0, s_pad - 1), (0, 0)))
        o_b, ret_s = _retention(hs_pad, COL_RET, cos_s, sin_s, ret_tab_s, ret_gn_w[l], state_ret[l], s_pad)
        o_c, rw_s = _rwkv(hs_pad, state_rwkv_shift[l], state_rwkv[l], 1, rw, s_pad, s_pad)
        o_d = _sb_sample(col(COL_SB), ck_sb, cv_sb, page_table, l * n_pool)
        xs = _merge(xs, [o_a, o_b[:, 0], o_c[:, 0], o_d], hs, wbr[l], wo[l], bs)
        xs = _ffn(xs, norm_ffn2[l], f2i[l], f2o[l], norm_final, last, bs)
        for i, a in enumerate((_heads4(col(COL_FOX + BLK))[:, None], _heads4(col(COL_FOX + 2 * BLK))[:, None],
                               lf_s[:, None], _heads4(col(COL_SB + BLK))[:, None],
                               _heads4(col(COL_SB + 2 * BLK))[:, None], ret_s, rw_s, col(COL_RWKV, RWKV_IN))):
            new_s[i].append(a)

    sp = [jnp.stack(v) for v in new_p]
    ss = [jnp.stack(v) for v in new_s]
    return (xp.reshape(bp, t, d), xs.reshape(bs, 1, d), sp[0], sp[1], sp[2], sp[3], sp[4], sp[5], sp[6], sp[7],
            ss[0], ss[1], ss[2], ss[3], ss[4], ss[5], ss[6], ss[7])
```

```python
import functools
import math

import jax
import jax.numpy as jnp
import numpy as np
from jax import lax
from jax.experimental import pallas as pl
from jax.experimental.pallas import tpu as pltpu

F32 = jnp.float32
BF16 = jnp.bfloat16
HIGHEST = lax.Precision.HIGHEST

D_MODEL = 1024
HEADS = 4
HEAD_DIM = 64
BRANCH_W = HEADS * HEAD_DIM
N_BRANCH = 4
DECAY_LORA = 64
AAA_LORA = 64
GATE_LORA = 128
RWKV_IN = 3 * BRANCH_W + DECAY_LORA + AAA_LORA + GATE_LORA
PAGE_SIZE = 128
RET_CHUNK = 128
ROPE_BASE = 10000.0
RMS_EPS = 1e-6
GN_EPS = 1e-5
RWKV_GN_EPS = 64e-5
QK_SCALE = HEAD_DIM ** -0.5
NEG_BIG = -1e30

COL_GATE = 0
COL_FOX = N_BRANCH * D_MODEL
COL_RET = COL_FOX + 3 * BRANCH_W
COL_RWKV = COL_RET + 4 * BRANCH_W
COL_SB = COL_RWKV + RWKV_IN
COL_FORGET = COL_SB + 3 * BRANCH_W
PROJ_COLS = COL_FORGET + BRANCH_W
BLK = BRANCH_W

VMEM_LIMIT = 56 * 1024 * 1024
RWKV_CHUNK = 64
RWKV_CHUNKS_PER_STEP = 4


def _cparams(sem):
    return pltpu.CompilerParams(dimension_semantics=sem, vmem_limit_bytes=VMEM_LIMIT)


def _dot(a, b, precision=None):
    return jnp.dot(a, b, preferred_element_type=F32, precision=precision)


def _dot_nt(a, b, precision=None):
    return lax.dot_general(a, b, (((1,), (1,)), ((), ())), preferred_element_type=F32, precision=precision)


def _dot_tn(a, b, precision=None):
    return lax.dot_general(a, b, (((0,), (0,)), ((), ())), preferred_element_type=F32, precision=precision)


_NN = (((1,), (0,)), ((), ()))
_NT = (((1,), (1,)), ((), ()))
_TN = (((0,), (0,)), ((), ()))


def _mm(a, b, dims):
    return lax.dot_general(a.astype(BF16), b.astype(BF16), dims, preferred_element_type=F32)


def _rms(x, g):
    return x * lax.rsqrt(jnp.mean(x * x, axis=-1, keepdims=True) + RMS_EPS) * g


def _softplus_neg_abs(z):
    return jnp.log(1.0 + jnp.exp(-jnp.abs(z)))


def _head_slices():
    return [slice(h * HEAD_DIM, (h + 1) * HEAD_DIM) for h in range(HEADS)]


def _ffn_kernel(x_ref, g_ref, wa_ref, wb_ref, wo_ref, gf_ref, o_ref, xn_ref, *, final_norm):
    j = pl.program_id(1)

    @pl.when(j == 0)
    def _():
        xn_ref[...] = _rms(x_ref[...], g_ref[...]).astype(BF16)
        o_ref[...] = jnp.zeros_like(o_ref)

    xn = xn_ref[...]
    a = _dot(xn, wa_ref[...])
    b = _dot(xn, wb_ref[...])
    hmid = (a * (0.5 * jnp.tanh(0.5 * a) + 0.5)) * b
    o_ref[...] += _dot(hmid.astype(BF16), wo_ref[...])

    @pl.when(j == pl.num_programs(1) - 1)
    def _():
        y = x_ref[...] + 0.5 * o_ref[...]
        if final_norm:
            o_ref[...] = _rms(y, gf_ref[...])
        else:
            o_ref[...] = y


def _ffn(x, g, wi, wo, gf, final_norm, tm, tf=256):
    m, d = x.shape
    f = wo.shape[0]
    nf = f // tf
    kern = functools.partial(_ffn_kernel, final_norm=final_norm)
    outs = pl.pallas_call(
        kern,
        grid=(m // tm, nf),
        in_specs=[
            pl.BlockSpec((tm, d), lambda i, j: (i, 0)),
            pl.BlockSpec((1, d), lambda i, j: (0, 0)),
            pl.BlockSpec((d, tf), lambda i, j: (0, j)),
            pl.BlockSpec((d, tf), lambda i, j: (0, j + nf)),
            pl.BlockSpec((tf, d), lambda i, j: (j, 0)),
            pl.BlockSpec((1, d), lambda i, j: (0, 0)),
        ],
        out_specs=pl.BlockSpec((tm, d), lambda i, j: (i, 0)),
        out_shape=jax.ShapeDtypeStruct((m, d), F32),
        scratch_shapes=[pltpu.VMEM((tm, d), BF16)],
        compiler_params=_cparams(("parallel", "arbitrary")),
        name="ffn",
    )(x, g.reshape(1, d), wi, wi, wo, gf.reshape(1, d))
    return outs


def _proj_kernel(x_ref, g_ref, w_ref, o_ref, xn_ref):
    @pl.when(pl.program_id(1) == 0)
    def _():
        xn_ref[...] = _rms(x_ref[...], g_ref[...]).astype(BF16)

    o_ref[...] = _dot(xn_ref[...], w_ref[...])


def _proj(x, g, w, tm, tn=256):
    m, d = x.shape
    n = w.shape[1]
    return pl.pallas_call(
        _proj_kernel,
        grid=(m // tm, n // tn),
        in_specs=[
            pl.BlockSpec((tm, d), lambda i, j: (i, 0)),
            pl.BlockSpec((1, d), lambda i, j: (0, 0)),
            pl.BlockSpec((d, tn), lambda i, j: (0, j)),
        ],
        out_specs=pl.BlockSpec((tm, tn), lambda i, j: (i, j)),
        out_shape=jax.ShapeDtypeStruct((m, n), F32),
        scratch_shapes=[pltpu.VMEM((tm, d), BF16)],
        compiler_params=_cparams(("parallel", "arbitrary")),
        name="proj",
    )(x, g.reshape(1, d), w)


def _merge_kernel(x_ref, oa_ref, ob_ref, oc_ref, od_ref, g0_ref, g1_ref, g2_ref, g3_ref, wb_ref, wo_ref, o_ref):
    merged = None
    for i, (o_r, g_r) in enumerate(((oa_ref, g0_ref), (ob_ref, g1_ref), (oc_ref, g2_ref), (od_ref, g3_ref))):
        gate = 0.5 * jnp.tanh(0.5 * g_r[...]) + 0.5
        t = gate * _dot(o_r[...].astype(BF16), wb_ref[i])
        merged = t if merged is None else merged + t
    o_ref[...] = x_ref[...] + _dot(merged.astype(BF16), wo_ref[...])


def _merge(x, branches, h, wb, wo, tm):
    m, d = x.shape
    gate_specs = [pl.BlockSpec((tm, d), functools.partial(lambda i, k: (i, k), k=COL_GATE // d + k)) for k in range(N_BRANCH)]
    return pl.pallas_call(
        _merge_kernel,
        grid=(m // tm,),
        in_specs=[pl.BlockSpec((tm, d), lambda i: (i, 0))]
        + [pl.BlockSpec((tm, BRANCH_W), lambda i: (i, 0))] * N_BRANCH
        + gate_specs
        + [pl.BlockSpec((N_BRANCH, BRANCH_W, d), lambda i: (0, 0, 0)), pl.BlockSpec((d, d), lambda i: (0, 0))],
        out_specs=pl.BlockSpec((tm, d), lambda i: (i, 0)),
        out_shape=jax.ShapeDtypeStruct((m, d), F32),
        compiler_params=_cparams(("parallel",)),
        name="merge",
    )(x, *branches, h, h, h, h, wb, wo)


def _forget_kernel(ff_ref, bias_ref, lf_ref, hi_ref, mid_ref, lo_ref, *, t):
    x = ff_ref[...] + bias_ref[...]
    lf = jnp.minimum(x, 0.0) - _softplus_neg_abs(x)
    lf_ref[...] = lf
    w = 128
    tri = (lax.broadcasted_iota(jnp.int32, (w, w), 0) <= lax.broadcasted_iota(jnp.int32, (w, w), 1)).astype(F32)
    carry = jnp.zeros((x.shape[0], 1), F32)
    for c in range(t // w):
        cols = slice(c * w, (c + 1) * w)
        cs = _dot(lf[:, cols], tri, HIGHEST) + carry
        carry = cs[:, w - 1:w]
        hi = cs.astype(BF16)
        r1 = cs - hi.astype(F32)
        mid = r1.astype(BF16)
        hi_ref[:, cols] = hi
        mid_ref[:, cols] = mid
        lo_ref[:, cols] = (r1 - mid.astype(F32)).astype(BF16)


def _forget(ff_t, bias_rows):
    r, t = ff_t.shape
    term = jax.ShapeDtypeStruct((r, t), BF16)
    return pl.pallas_call(
        functools.partial(_forget_kernel, t=t),
        out_shape=(jax.ShapeDtypeStruct((r, t), F32), term, term, term),
        compiler_params=pltpu.CompilerParams(vmem_limit_bytes=VMEM_LIMIT),
        name="forget",
    )(ff_t, bias_rows)


PACK = 128


TERM_COLS = 16


def _pack_kernel(fq_ref, fk_ref, fv_ref, sq_ref, sk_ref, sv_ref, terms_ref,
                 fqa_ref, fka_ref, fva_ref, sqa_ref, ska_ref, sva_ref):
    w = HEADS * PACK
    r = lax.broadcasted_iota(jnp.int32, (BLK, w), 0)
    c = lax.broadcasted_iota(jnp.int32, (BLK, w), 1)
    place = ((c // PACK == r // HEAD_DIM) & (c % PACK == r % HEAD_DIM)).astype(BF16)
    e = lax.broadcasted_iota(jnp.int32, (TERM_COLS, w), 0)
    ce = lax.broadcasted_iota(jnp.int32, (TERM_COLS, w), 1)
    term_lane = (e % HEADS) * PACK + HEAD_DIM + e // HEADS
    real = e < 3 * HEADS
    place_q = ((ce == term_lane) & real).astype(BF16)
    place_k = -((ce == term_lane + 3) & real).astype(BF16)
    lane = lax.broadcasted_iota(jnp.int32, (1, w), 1) % PACK
    ones_q = ((lane >= HEAD_DIM + 3) & (lane < HEAD_DIM + 6)).astype(F32)
    ones_k = ((lane >= HEAD_DIM) & (lane < HEAD_DIM + 3)).astype(F32)
    ones_v = (lane == HEAD_DIM).astype(F32)
    terms = terms_ref[...]

    def put(x):
        return _dot(x.astype(BF16), place)

    fqa_ref[...] = (put(fq_ref[...] * QK_SCALE) + _dot(terms, place_q) + ones_q).astype(BF16)
    fka_ref[...] = (put(fk_ref[...]) + _dot(terms, place_k) + ones_k).astype(BF16)
    fva_ref[...] = (put(fv_ref[...]) + ones_v).astype(BF16)
    sqa_ref[...] = put(sq_ref[...] * QK_SCALE).astype(BF16)
    ska_ref[...] = put(sk_ref[...]).astype(BF16)
    sva_ref[...] = put(sv_ref[...]).astype(BF16)


def _pack_operands(h, terms, tm):
    m = h.shape[0]
    w = HEADS * PACK
    colblk = lambda c0: pl.BlockSpec((tm, BLK), functools.partial(lambda i, c: (i, c), c=c0 // BLK))
    cols = [COL_FOX, COL_FOX + BLK, COL_FOX + 2 * BLK, COL_SB, COL_SB + BLK, COL_SB + 2 * BLK]
    return pl.pallas_call(
        _pack_kernel,
        grid=(m // tm,),
        in_specs=[colblk(c0) for c0 in cols] + [pl.BlockSpec((tm, TERM_COLS), lambda i: (i, 0))],
        out_specs=[pl.BlockSpec((tm, w), lambda i: (i, 0))] * 6,
        out_shape=[jax.ShapeDtypeStruct((m, w), BF16)] * 6,
        compiler_params=_cparams(("parallel",)),
        name="pack",
    )(h, h, h, h, h, h, terms)


def _causal_pairs(nq, keys_descending):
    qs, ks = [], []
    for qi in range(nq):
        order = range(qi, -1, -1) if keys_descending else range(qi + 1)
        for ki in order:
            qs.append(qi)
            ks.append(ki)
    return jnp.asarray(np.array(qs, np.int32)), jnp.asarray(np.array(ks, np.int32))


def _fox_kernel(qtab_ref, ktab_ref, q_ref, k_ref, v_ref, o_ref, m_scr, acc_scr, *, tq, seqs):
    qi = qtab_ref[pl.program_id(1)]
    ki = ktab_ref[pl.program_id(1)]
    lanes = [slice(h * PACK, (h + 1) * PACK) for h in range(HEADS)]
    pieces = [(n, h) for n in range(seqs) for h in range(HEADS)]
    ids = range(len(pieces))

    @pl.when(ki == 0)
    def _():
        m_scr[...] = jnp.full_like(m_scr, NEG_BIG)
        acc_scr[...] = jnp.zeros_like(acc_scr)

    def block(diagonal):
        s = [_dot_nt(q_ref[n, :, lanes[h]], k_ref[n, :, lanes[h]]) for n, h in pieces]
        if diagonal:
            mask = lax.broadcasted_iota(jnp.int32, (tq, tq), 1) <= lax.broadcasted_iota(jnp.int32, (tq, tq), 0)
            s = [jnp.where(mask, x, NEG_BIG) for x in s]
        m_old = [m_scr[i] for i in ids]
        m_new = [jnp.maximum(m_old[i], jnp.max(s[i], axis=-1, keepdims=True)) for i in ids]
        p = [jnp.exp(s[i] - m_new[i]).astype(BF16) for i in ids]
        pv = [_dot(p[i], v_ref[n, :, lanes[h]]) for i, (n, h) in enumerate(pieces)]
        for i in ids:
            acc_scr[i] = jnp.exp(m_old[i] - m_new[i]) * acc_scr[i] + pv[i]
            m_scr[i] = m_new[i]

    pl.when(ki < qi)(functools.partial(block, False))

    @pl.when(ki == qi)
    def _():
        block(True)
        for i, (n, h) in enumerate(pieces):
            acc = acc_scr[i]
            o_ref[n, :, h * HEAD_DIM:(h + 1) * HEAD_DIM] = acc[:, :HEAD_DIM] / acc[:, HEAD_DIM:HEAD_DIM + 1]


def _pair_call(kern, name, qa, ka, va, tq, keys_descending, scratch, seqs=1):
    b, t, w = qa.shape
    qtab, ktab = _causal_pairs(t // tq, keys_descending)
    grid_spec = pltpu.PrefetchScalarGridSpec(
        num_scalar_prefetch=2,
        grid=(b // seqs, qtab.shape[0]),
        in_specs=[
            pl.BlockSpec((seqs, tq, w), lambda bi, p, qt, kt: (bi, qt[p], 0)),
            pl.BlockSpec((seqs, tq, w), lambda bi, p, qt, kt: (bi, kt[p], 0)),
            pl.BlockSpec((seqs, tq, w), lambda bi, p, qt, kt: (bi, kt[p], 0)),
        ],
        out_specs=pl.BlockSpec((seqs, tq, BLK), lambda bi, p, qt, kt: (bi, qt[p], 0)),
        scratch_shapes=scratch,
    )
    return pl.pallas_call(
        kern,
        grid_spec=grid_spec,
        out_shape=jax.ShapeDtypeStruct((b, t, BLK), F32),
        compiler_params=_cparams(("parallel", "arbitrary")),
        name=name,
    )(qtab, ktab, qa, ka, va)


def _fox_prompt(qa, ka, va, tq=512):
    seqs = 1
    scratch = [pltpu.VMEM((seqs * HEADS, tq, 1), F32), pltpu.VMEM((seqs * HEADS, tq, PACK), F32)]
    kern = functools.partial(_fox_kernel, tq=tq, seqs=seqs)
    return _pair_call(kern, "fox_prompt", qa, ka, va, tq, False, scratch, seqs)


def _strict_upper(n):
    return lax.broadcasted_iota(jnp.int32, (n, n), 0) > lax.broadcasted_iota(jnp.int32, (n, n), 1)


def _suffix_excl(x, u_bf16):
    hi = x.astype(BF16)
    lo = (x - hi.astype(F32)).astype(BF16)
    return _dot(hi, u_bf16) + _dot(lo, u_bf16)


def _sb_kernel(qtab_ref, ktab_ref, q_ref, k_ref, v_ref, o_ref, r_scr, acc_scr, *, tq):
    qi = qtab_ref[pl.program_id(1)]
    ki = ktab_ref[pl.program_id(1)]

    @pl.when(ki == qi)
    def _():
        r_scr[...] = jnp.zeros_like(r_scr)
        acc_scr[...] = jnp.zeros_like(acc_scr)

    heads = range(HEADS)
    lanes = [slice(h * PACK, (h + 1) * PACK) for h in heads]

    def block(diagonal):
        u = _strict_upper(tq).astype(BF16)
        z = [_dot_nt(q_ref[0, :, lanes[h]], k_ref[0, :, lanes[h]]) for h in heads]
        sp = [_softplus_neg_abs(x) for x in z]
        ls = [jnp.minimum(z[h], 0.0) - sp[h] for h in heads]
        l1m = [-jnp.maximum(z[h], 0.0) - sp[h] for h in heads]
        if diagonal:
            mask = lax.broadcasted_iota(jnp.int32, (tq, tq), 1) < lax.broadcasted_iota(jnp.int32, (tq, tq), 0)
            l1m = [jnp.where(mask, x, 0.0) for x in l1m]
        excl = [_suffix_excl(x, u) for x in l1m]
        a = [jnp.exp(ls[h] + excl[h] + r_scr[h]) for h in heads]
        if diagonal:
            a = [jnp.where(mask, x, 0.0) for x in a]
        av = [_dot(a[h].astype(BF16), v_ref[0, :, lanes[h]]) for h in heads]
        for h in heads:
            acc_scr[h] += av[h]
            r_scr[h] += jnp.sum(l1m[h], axis=-1, keepdims=True)

    pl.when(ki == qi)(functools.partial(block, True))
    pl.when(ki < qi)(functools.partial(block, False))

    @pl.when(ki == 0)
    def _():
        for h, sl in enumerate(_head_slices()):
            o_ref[0, :, sl] = acc_scr[h][:, :HEAD_DIM]


def _sb_prompt(qa, ka, va, tq=256):
    scratch = [pltpu.VMEM((HEADS, tq, 1), F32), pltpu.VMEM((HEADS, tq, PACK), F32)]
    return _pair_call(functools.partial(_sb_kernel, tq=tq), "sb_prompt", qa, ka, va, tq, True, scratch)


GROUP = 8
MAX_SAMPLE_PAGES_PER_STEP = 32


def _page_scores(k_refs, qx_scr, s_scr):
    for j, k_ref in enumerate(k_refs):
        for h, sl in enumerate(_head_slices()):
            s_scr[j * GROUP + h:j * GROUP + h + 1, :] = jnp.sum(k_ref[0, sl, :] * qx_scr[sl, :], axis=0, keepdims=True)


def _page_carries(r, tot, pp):
    carries = []
    for j in range(pp):
        carries.append(r)
        r = r + tot[j * GROUP:(j + 1) * GROUP]
    return jnp.concatenate(carries, axis=0), r


def _weighted_values(w_scr, v_refs, acc_scr, scale):
    for h, sl in enumerate(_head_slices()):
        acc = acc_scr[sl, :] if scale is None else acc_scr[sl, :] * scale[h:h + 1, :]
        for j, v_ref in enumerate(v_refs):
            acc = acc + w_scr[j * GROUP + h:j * GROUP + h + 1, :] * v_ref[0, sl, :]
        acc_scr[sl, :] = acc


def _fox_sample_kernel(pt_ref, q_ref, kn_ref, vn_ref, lfn_ref, *refs, pp):
    k_refs, v_refs, lf_refs = refs[:pp], refs[pp:2 * pp], refs[2 * pp:3 * pp]
    o_ref, qx_scr, s_scr, lf_scr, p_scr, m_scr, l_scr, r_scr, acc_scr = refs[3 * pp:]
    step = pl.program_id(1)

    @pl.when(step == 0)
    def _():
        qcol = q_ref[0] * QK_SCALE
        qx_scr[...] = jnp.broadcast_to(qcol, qx_scr.shape)
        s_scr[...] = jnp.zeros_like(s_scr)
        lf_scr[...] = jnp.zeros_like(lf_scr)
        m_scr[...] = jnp.zeros_like(m_scr)
        own = qcol * kn_ref[0]
        for h, sl in enumerate(_head_slices()):
            m_scr[h:h + 1, :] = jnp.sum(own[sl], axis=0, keepdims=True)
        l_scr[...] = jnp.ones_like(l_scr)
        r_scr[...] = lfn_ref[0]
        lane = lax.broadcasted_iota(jnp.int32, acc_scr.shape, 1)
        acc_scr[...] = jnp.where(lane == 0, vn_ref[0], 0.0)

    for j, lf_ref in enumerate(lf_refs):
        lf_scr[j * GROUP:j * GROUP + HEADS, :] = lf_ref[0]
    _page_scores(k_refs, qx_scr, s_scr)
    lf_all = lf_scr[...]
    excl = _dot(lf_all, _strict_upper(PAGE_SIZE).astype(F32), HIGHEST)
    carries, r_out = _page_carries(r_scr[...], jnp.sum(lf_all, axis=-1, keepdims=True), pp)
    r_scr[...] = r_out
    s_all = s_scr[...] + excl + carries
    row_max = jnp.max(s_all, axis=-1, keepdims=True)
    m_old = m_scr[...]
    m_new = m_old
    for j in range(pp):
        m_new = jnp.maximum(m_new, row_max[j * GROUP:(j + 1) * GROUP])
    alpha = jnp.exp(m_old - m_new)
    p_all = jnp.exp(s_all - jnp.concatenate([m_new] * pp, axis=0))
    p_scr[...] = p_all
    row_sum = jnp.sum(p_all, axis=-1, keepdims=True)
    l_new = alpha * l_scr[...]
    for j in range(pp):
        l_new = l_new + row_sum[j * GROUP:(j + 1) * GROUP]
    l_scr[...] = l_new
    m_scr[...] = m_new
    _weighted_values(p_scr, v_refs, acc_scr, alpha)

    @pl.when(step == pl.num_programs(1) - 1)
    def _():
        l = l_scr[...]
        for h, sl in enumerate(_head_slices()):
            o_ref[0, sl, :] = jnp.sum(acc_scr[sl, :], axis=-1, keepdims=True) / l[h:h + 1, :]


def _sb_sample_kernel(pt_ref, q_ref, *refs, pp):
    k_refs, v_refs = refs[:pp], refs[pp:2 * pp]
    o_ref, qx_scr, z_scr, a_scr, r_scr, acc_scr = refs[2 * pp:]
    step = pl.program_id(1)

    @pl.when(step == 0)
    def _():
        qx_scr[...] = jnp.broadcast_to(q_ref[0] * QK_SCALE, qx_scr.shape)
        z_scr[...] = jnp.zeros_like(z_scr)
        r_scr[...] = jnp.zeros_like(r_scr)
        acc_scr[...] = jnp.zeros_like(acc_scr)

    _page_scores(k_refs, qx_scr, z_scr)
    z = z_scr[...]
    sp = _softplus_neg_abs(z)
    ls = jnp.minimum(z, 0.0) - sp
    l1m = -jnp.maximum(z, 0.0) - sp
    excl = _suffix_excl(l1m, _strict_upper(PAGE_SIZE).astype(BF16))
    carries, r_out = _page_carries(r_scr[...], jnp.sum(l1m, axis=-1, keepdims=True), pp)
    r_scr[...] = r_out
    a_scr[...] = jnp.exp(ls + excl + carries)
    _weighted_values(a_scr, v_refs, acc_scr, None)

    @pl.when(step == pl.num_programs(1) - 1)
    def _():
        o_ref[...] = jnp.sum(acc_scr[...], axis=-1, keepdims=True)[None]


def _pages_per_step(n_pages):
    pp = math.gcd(n_pages, MAX_SAMPLE_PAGES_PER_STEP)
    return pp


def _page_specs(page_table, base, pp, block):
    n_pages = page_table.shape[1]

    def spec(j):
        return pl.BlockSpec(block, lambda bi, st, pt: (base + pt[bi, n_pages - 1 - (st * pp + j)], 0, 0))

    return [spec(j) for j in range(pp)]


def _per_sample(bi, st, pt):
    return (bi, 0, 0)


def _fox_sample(q, k_new, v_new, lf_new, cache_kt, cache_vt, cache_lft, page_table, base):
    b, n_pages = page_table.shape
    pp = _pages_per_step(n_pages)
    col = lambda x: x[:, :, None]
    lfn = col(jnp.concatenate([lf_new, jnp.zeros((b, GROUP - HEADS), F32)], axis=1))
    kv_block = (1, BLK, PAGE_SIZE)
    grid_spec = pltpu.PrefetchScalarGridSpec(
        num_scalar_prefetch=1,
        grid=(b, n_pages // pp),
        in_specs=[pl.BlockSpec((1, BLK, 1), _per_sample)] * 3 + [pl.BlockSpec((1, GROUP, 1), _per_sample)]
        + _page_specs(page_table, base, pp, kv_block) + _page_specs(page_table, base, pp, kv_block)
        + _page_specs(page_table, base, pp, (1, HEADS, PAGE_SIZE)),
        out_specs=pl.BlockSpec((1, BLK, 1), _per_sample),
        scratch_shapes=[pltpu.VMEM((BLK, PAGE_SIZE), F32)] + [pltpu.VMEM((pp * GROUP, PAGE_SIZE), F32)] * 3
        + [pltpu.VMEM((GROUP, 1), F32)] * 3 + [pltpu.VMEM((BLK, PAGE_SIZE), F32)],
    )
    out = pl.pallas_call(
        functools.partial(_fox_sample_kernel, pp=pp),
        grid_spec=grid_spec,
        out_shape=jax.ShapeDtypeStruct((b, BLK, 1), F32),
        compiler_params=_cparams(("parallel", "arbitrary")),
        name="fox_sample",
    )(page_table, col(q), col(k_new), col(v_new), lfn, *([cache_kt] * pp), *([cache_vt] * pp), *([cache_lft] * pp))
    return out[:, :, 0]


def _sb_sample(q, cache_kt, cache_vt, page_table, base):
    b, n_pages = page_table.shape
    pp = _pages_per_step(n_pages)
    kv_block = (1, BLK, PAGE_SIZE)
    grid_spec = pltpu.PrefetchScalarGridSpec(
        num_scalar_prefetch=1,
        grid=(b, n_pages // pp),
        in_specs=[pl.BlockSpec((1, BLK, 1), _per_sample)]
        + _page_specs(page_table, base, pp, kv_block) + _page_specs(page_table, base, pp, kv_block),
        out_specs=pl.BlockSpec((1, BLK, 1), _per_sample),
        scratch_shapes=[pltpu.VMEM((BLK, PAGE_SIZE), F32)] + [pltpu.VMEM((pp * GROUP, PAGE_SIZE), F32)] * 2
        + [pltpu.VMEM((GROUP, 1), F32), pltpu.VMEM((BLK, PAGE_SIZE), F32)],
    )
    out = pl.pallas_call(
        functools.partial(_sb_sample_kernel, pp=pp),
        grid_spec=grid_spec,
        out_shape=jax.ShapeDtypeStruct((b, BLK, 1), F32),
        compiler_params=_cparams(("parallel", "arbitrary")),
        name="sb_sample",
    )(page_table, q[:, :, None], *([cache_kt] * pp), *([cache_vt] * pp))
    return out[:, :, 0]


def _rot_half(x):
    half = HEAD_DIM // 2
    lane = lax.broadcasted_iota(jnp.int32, x.shape, 1) % HEAD_DIM
    n = x.shape[1]
    return jnp.where(lane < half, -pltpu.roll(x, n - half, 1), pltpu.roll(x, half, 1))


def _ret_kernel(q_ref, k_ref, v_ref, g_ref, cos_ref, sin_ref, dmask_ref, xi_ref, zeta_ref, cd_ref, gnw_ref, s0_ref,
                o_ref, st_ref, s_scr, *, seqs):
    c = pl.program_id(1)

    @pl.when(c == 0)
    def _():
        s_scr[...] = s0_ref[...]

    cos = cos_ref[...]
    sin = sin_ref[...]
    gnw = gnw_ref[...]
    qr, kr, v, gate = [], [], [], []
    for n in range(seqs):
        q = q_ref[n]
        k = k_ref[n]
        g = g_ref[n]
        qr.append((q * cos + _rot_half(q) * sin).astype(BF16))
        kr.append((k * cos + _rot_half(k) * sin) * QK_SCALE)
        v.append(v_ref[n].astype(BF16))
        gate.append(g * jax.nn.sigmoid(g))
    pieces = [(n, h, sl) for n in range(seqs) for h, sl in enumerate(_head_slices())]
    ids = range(len(pieces))
    qh = [qr[n][:, sl] for n, h, sl in pieces]
    kh = [kr[n][:, sl] for n, h, sl in pieces]
    vh = [v[n][:, sl] for n, h, sl in pieces]
    s = [s_scr[n, h] for n, h, sl in pieces]
    att = [(_dot_nt(qh[i], kh[i].astype(BF16)) * dmask_ref[pieces[i][1]]).astype(BF16) for i in ids]
    cross = [_dot(qh[i], s[i].astype(BF16)) * xi_ref[pieces[i][1]] for i in ids]
    o = [_dot(att[i], vh[i]) + cross[i] for i in ids]
    kv = [_dot_tn((kh[i] * zeta_ref[pieces[i][1]]).astype(BF16), vh[i]) for i in ids]
    for i, (n, h, sl) in enumerate(pieces):
        s_scr[n, h] = cd_ref[h] * s[i] + kv[i]
    mu = [jnp.mean(x, axis=-1, keepdims=True) for x in o]
    d = [o[i] - mu[i] for i in ids]
    var = [jnp.mean(x * x, axis=-1, keepdims=True) for x in d]
    for i, (n, h, sl) in enumerate(pieces):
        o_ref[n, :, sl] = gate[n][:, sl] * (d[i] * lax.rsqrt(var[i] + GN_EPS) * gnw[:, sl])

    @pl.when(c == pl.num_programs(1) - 1)
    def _():
        st_ref[...] = s_scr[...]


def _ret_tables(chunk, n_valid):
    lg = np.log(1.0 - 2.0 ** (-5.0 - np.arange(HEADS, dtype=np.float64)))
    n = np.arange(chunk, dtype=np.float64)
    diff = n[:, None] - n[None, :]
    valid = (n < n_valid).astype(np.float64)
    dmask = np.where(diff[None] >= 0, np.exp(np.maximum(diff, 0.0)[None] * lg[:, None, None]), 0.0)
    dmask = dmask * valid[None, :, None] * valid[None, None, :]
    xi = np.exp((n[None, :] + 1.0) * lg[:, None]) * valid[None, :]
    zeta = np.exp((n_valid - 1.0 - n)[None, :] * lg[:, None]) * valid[None, :]
    cd = np.exp(n_valid * lg)
    f = lambda a: jnp.asarray(a.astype(np.float32))
    return f(dmask), f(xi[:, :, None]), f(zeta[:, :, None]), f(cd[:, None, None])


def _rope_tables(pos):
    half = HEAD_DIM // 2
    inv = ROPE_BASE ** (-jnp.arange(half, dtype=F32) / half)
    ang = pos.astype(F32)[:, None] * inv[None, :]
    cos = jnp.tile(jnp.cos(ang), (1, 2 * HEADS))
    sin = jnp.tile(jnp.sin(ang), (1, 2 * HEADS))
    return cos, sin


def _retention(h3, col0, cos, sin, tables, gnw, s0, chunk):
    b, t, _ = h3.shape
    nc = t // chunk
    cb = col0 // BLK
    dmask, xi, zeta, cd = tables
    nq = math.gcd(b, SCAN_SEQS_PER_STEP)

    def tok(k):
        return pl.BlockSpec((nq, chunk, BLK), functools.partial(lambda bi, c, k: (bi, c, k), k=cb + k))

    const3 = lambda bi, c: (0, 0, 0)
    return pl.pallas_call(
        functools.partial(_ret_kernel, seqs=nq),
        grid=(b // nq, nc),
        in_specs=[tok(0), tok(1), tok(2), tok(3),
                  pl.BlockSpec((chunk, BLK), lambda bi, c: (c, 0)),
                  pl.BlockSpec((chunk, BLK), lambda bi, c: (c, 0)),
                  pl.BlockSpec((HEADS, chunk, chunk), const3),
                  pl.BlockSpec((HEADS, chunk, 1), const3),
                  pl.BlockSpec((HEADS, chunk, 1), const3),
                  pl.BlockSpec((HEADS, 1, 1), const3),
                  pl.BlockSpec((1, BLK), lambda bi, c: (0, 0)),
                  pl.BlockSpec((nq, HEADS, HEAD_DIM, HEAD_DIM), lambda bi, c: (bi, 0, 0, 0))],
        out_specs=[pl.BlockSpec((nq, chunk, BLK), lambda bi, c: (bi, c, 0)),
                   pl.BlockSpec((nq, HEADS, HEAD_DIM, HEAD_DIM), lambda bi, c: (bi, 0, 0, 0))],
        out_shape=[jax.ShapeDtypeStruct((b, t, BLK), F32),
                   jax.ShapeDtypeStruct((b, HEADS, HEAD_DIM, HEAD_DIM), F32)],
        scratch_shapes=[pltpu.VMEM((nq, HEADS, HEAD_DIM, HEAD_DIM), F32)],
        compiler_params=_cparams(("parallel", "arbitrary")),
        name="retention",
    )(h3, h3, h3, h3, cos, sin, dmask, xi, zeta, cd, gnw.reshape(1, BLK), s0)


def _head_sum_matrix():
    i = lax.broadcasted_iota(jnp.int32, (BLK, BLK), 0) // HEAD_DIM
    j = lax.broadcasted_iota(jnp.int32, (BLK, BLK), 1) // HEAD_DIM
    return (i == j).astype(F32)


def _rwkv_prep_kernel(h0_ref, h1_ref, h2_ref, h3_ref, c0_ref, c1_ref, c2_ref, c3_ref, shift_ref,
                      mu_ref, w0_ref, a0_ref, kk_w_ref, ka_ref, rk_ref, wup_ref, aup_ref, gup_ref,
                      r_ref, k_ref, v_ref, logd_ref, kk_ref, b_ref, g_ref, bonus_ref, *, tm, seq_len, n_valid):
    hr = jnp.concatenate([h0_ref[...], h1_ref[...], h2_ref[...], h3_ref[...]], axis=1)
    before = jnp.concatenate([c0_ref[...], c1_ref[...], c2_ref[...], c3_ref[...]], axis=1)[7:8]
    row = lax.broadcasted_iota(jnp.int32, (tm, 1), 0)
    pos = (pl.program_id(0) * tm + row) & (seq_len - 1)
    prev = jnp.where(row == 0, before, pltpu.roll(hr, 1, 0))
    prev = jnp.where(pos == 0, shift_ref[0], prev)
    valid = (pos < n_valid).astype(F32)
    xm = hr + (prev - hr) * mu_ref[...]
    r = xm[:, 0:BLK]
    k = xm[:, BLK:2 * BLK]
    v = xm[:, 2 * BLK:3 * BLK]
    o = 3 * BLK
    wd = xm[:, o:o + DECAY_LORA]
    ad = xm[:, o + DECAY_LORA:o + DECAY_LORA + AAA_LORA]
    gd = xm[:, o + DECAY_LORA + AAA_LORA:]
    wpre = w0_ref[...] + _dot(jnp.tanh(wd).astype(BF16), wup_ref[...])
    w = jnp.minimum(wpre, 0.0) - _softplus_neg_abs(wpre) - 0.5
    a = jax.nn.sigmoid(a0_ref[...] + _dot(ad.astype(BF16), aup_ref[...]))
    g = _dot(jax.nn.sigmoid(gd).astype(BF16), gup_ref[...])
    hs = _head_sum_matrix()
    kk = k * kk_w_ref[...]
    kk = kk * lax.rsqrt(_dot(kk * kk, hs, HIGHEST) + 1e-12)
    kmod = k * (1.0 + (a - 1.0) * ka_ref[...])
    bonus = _dot(r * kmod * rk_ref[...], hs, HIGHEST) * v
    r_ref[...] = r * valid
    k_ref[...] = kmod * valid
    v_ref[...] = v * valid
    logd_ref[...] = -jnp.exp(w) * valid
    kk_ref[...] = kk * valid
    b_ref[...] = kk * a * valid
    g_ref[...] = g
    bonus_ref[...] = bonus


def _rwkv_prep(h, shift0, seq_len, n_valid, mu, w0, a0, k_k, k_a, r_k, w_up, a_up, g_up, tm):
    m = h.shape[0]
    assert seq_len & (seq_len - 1) == 0 and seq_len % tm == 0 and tm % 8 == 0
    cb = COL_RWKV // BLK
    row = lambda n: pl.BlockSpec((1, n), lambda i: (0, 0))
    full = lambda a: pl.BlockSpec(a.shape, lambda i: (0, 0))
    tokb = pl.BlockSpec((tm, BLK), lambda i: (i, 0))
    cur = [pl.BlockSpec((tm, BLK), functools.partial(lambda i, c: (i, c), c=cb + c)) for c in range(4)]
    before = [pl.BlockSpec((8, BLK), functools.partial(lambda i, c: (jnp.maximum(i * (tm // 8) - 1, 0), c), c=cb + c))
              for c in range(4)]
    return pl.pallas_call(
        functools.partial(_rwkv_prep_kernel, tm=tm, seq_len=seq_len, n_valid=n_valid),
        grid=(m // tm,),
        in_specs=cur + before + [pl.BlockSpec((1, 1, RWKV_IN), lambda i: (i * tm // seq_len, 0, 0)),
                                 row(RWKV_IN), row(BLK), row(BLK), row(BLK), row(BLK), row(BLK),
                                 full(w_up), full(a_up), full(g_up)],
        out_specs=[tokb] * 8,
        out_shape=[jax.ShapeDtypeStruct((m, BLK), F32)] * 8,
        compiler_params=_cparams(("parallel",)),
        name="rwkv_prep",
    )(h, h, h, h, h, h, h, h, shift0[:, None, :], mu.reshape(1, -1), w0.reshape(1, -1), a0.reshape(1, -1),
      k_k.reshape(1, -1), k_a.reshape(1, -1), r_k.reshape(1, -1), w_up, a_up, g_up)


def _rwkv_chunk_kernel(r_ref, k_ref, v_ref, logd_ref, kk_ref, b_ref, r2_ref, y0_ref, gt_ref, ht_ref, *, chunk, per_step):
    c = chunk
    n = HEADS * c
    ri = lax.broadcasted_iota(jnp.int32, (c, c), 0)
    ci = lax.broadcasted_iota(jnp.int32, (c, c), 1)
    cum = (ci <= ri).astype(F32)
    rr = lax.broadcasted_iota(jnp.int32, (n, n), 0)
    cc = lax.broadcasted_iota(jnp.int32, (n, n), 1)
    assert c & (c - 1) == 0
    strict = (cc & (c - 1)) < (rr & (c - 1))
    incl = (cc & (c - 1)) <= (rr & (c - 1))
    eye = (rr == cc).astype(F32)
    eye_w = (lax.broadcasted_iota(jnp.int32, (BLK, BLK), 0) == lax.broadcasted_iota(jnp.int32, (BLK, BLK), 1)).astype(F32)
    lane_head = lax.broadcasted_iota(jnp.int32, (c, BLK), 1) // HEAD_DIM

    def stack(x):
        return jnp.concatenate([jnp.where(lane_head == h, x, 0.0) for h in range(HEADS)], axis=0)

    def unstack(xw):
        out = xw[0:c]
        for h in range(1, HEADS):
            out = out + xw[h * c:(h + 1) * c]
        return out

    chunks = range(per_step)
    rows = [slice(s * c, (s + 1) * c) for s in chunks]
    lk, lr, rb, rk, vw, be_w, ke_w, w_c = [], [], [], [], [], [], [], []
    for s in chunks:
        logd = logd_ref[0, rows[s], :]
        logw = _dot(cum, logd, HIGHEST)
        logw_c = logw[c - 1:c, :]
        e_neg = jnp.exp(-logw)
        e_end = jnp.exp(logw_c - logw)
        b = b_ref[0, rows[s], :]
        k = k_ref[0, rows[s], :]
        lk.append(stack(kk_ref[0, rows[s], :] * jnp.exp(logw - logd)))
        lr.append(stack(r_ref[0, rows[s], :] * jnp.exp(logw)))
        rb.append(stack(b * e_neg))
        rk.append(stack(k * e_neg))
        be_w.append(stack(b * e_end))
        ke_w.append(stack(k * e_end))
        vw.append(stack(v_ref[0, rows[s], :]))
        w_c.append(jnp.exp(logw_c))

    lhs = [jnp.concatenate([lk[s], lr[s]], axis=0) for s in chunks]
    ab_mb = [_mm(lhs[s], rb[s], _NT) for s in chunks]
    ak_mk = [_mm(lhs[s], rk[s], _NT) for s in chunks]
    m_b = [jnp.where(incl, ab_mb[s][n:], 0.0) for s in chunks]
    am_k = [jnp.concatenate([jnp.where(strict, ak_mk[s][:n], 0.0), jnp.where(incl, ak_mk[s][n:], 0.0)], axis=0)
            for s in chunks]
    npow = [jnp.where(strict, -ab_mb[s][:n], 0.0) for s in chunks]
    tinv = [eye + npow[s] for s in chunks]
    for _ in range(int(math.log2(c)) - 1):
        npow = [_mm(npow[s], npow[s], _NN) for s in chunks]
        tinv = [tinv[s] + _mm(tinv[s], npow[s], _NN) for s in chunks]
    akv_mkv = [_mm(am_k[s], vw[s], _NN) for s in chunks]
    pq = [_mm(tinv[s], jnp.concatenate([lk[s], akv_mkv[s][:n]], axis=1), _NN) for s in chunks]
    mb_pq = [_mm(m_b[s], pq[s], _NN) for s in chunks]
    for s in chunks:
        r2_ref[0, rows[s], :] = unstack(lr[s] - mb_pq[s][:, :BLK])
        y0_ref[0, rows[s], :] = unstack(akv_mkv[s][n:] - mb_pq[s][:, BLK:])
    for s in chunks:
        p_w, q_w = pq[s][:, :BLK], pq[s][:, BLK:]
        gt_ref[0, s] = eye_w * w_c[s] - _mm(be_w[s], p_w, _TN)
        ht_ref[0, s] = _mm(ke_w[s], vw[s], _TN) - _mm(be_w[s], q_w, _TN)


def _chunks_per_step(nc):
    return math.gcd(nc, RWKV_CHUNKS_PER_STEP)


def _rwkv_chunks(r, k, v, logd, kk, b, chunk):
    bsz, t, _ = r.shape
    nc = t // chunk
    ps = _chunks_per_step(nc)
    tok = pl.BlockSpec((1, ps * chunk, BLK), lambda bi, c: (bi, c, 0))
    mat = pl.BlockSpec((1, ps, BLK, BLK), lambda bi, c: (bi, c, 0, 0))
    mat_shape = jax.ShapeDtypeStruct((bsz, nc, BLK, BLK), F32)
    return pl.pallas_call(
        functools.partial(_rwkv_chunk_kernel, chunk=chunk, per_step=ps),
        grid=(bsz, nc // ps),
        in_specs=[tok] * 6,
        out_specs=[tok, tok, mat, mat],
        out_shape=[jax.ShapeDtypeStruct((bsz, t, BLK), F32)] * 2 + [mat_shape] * 2,
        compiler_params=_cparams(("parallel", "parallel")),
        name="rwkv_chunks",
    )(r, k, v, logd, kk, b)


def _rwkv_scan_kernel(r2_ref, y0_ref, gt_ref, ht_ref, gate_ref, bonus_ref, lnw_ref, lnb_ref, s0_ref,
                      o_ref, st_ref, s_scr, *, chunk, per_step, seqs):
    step = pl.program_id(1)

    @pl.when(step == 0)
    def _():
        s_scr[...] = jnp.zeros_like(s_scr)
        for q in range(seqs):
            for h, sl in enumerate(_head_slices()):
                s_scr[q, sl, sl] = s0_ref[q, h]

    lnw = lnw_ref[...]
    lnb = lnb_ref[...]
    sts = [[s_scr[q] for q in range(seqs)]]
    for i in range(per_step):
        sts.append([_mm(gt_ref[q, i], sts[i][q], _NN) + ht_ref[q, i] for q in range(seqs)])
    for q in range(seqs):
        s_scr[q] = sts[per_step][q]
    rows = [slice(i * chunk, (i + 1) * chunk) for i in range(per_step)]
    pairs = [(q, i) for q in range(seqs) for i in range(per_step)]
    ys = [_mm(r2_ref[q, rows[i], :], sts[i][q], _NN) + y0_ref[q, rows[i], :] for q, i in pairs]
    pieces = [(n, q, i, sl) for n, (q, i) in enumerate(pairs) for sl in _head_slices()]
    yh = [ys[n][:, sl] for n, q, i, sl in pieces]
    mu = [jnp.mean(x, axis=-1, keepdims=True) for x in yh]
    d = [yh[m] - mu[m] for m in range(len(pieces))]
    var = [jnp.mean(x * x, axis=-1, keepdims=True) for x in d]
    for m, (n, q, i, sl) in enumerate(pieces):
        yn = d[m] * lax.rsqrt(var[m] + RWKV_GN_EPS) * lnw[:, sl] + lnb[:, sl]
        o_ref[q, rows[i], sl] = (yn + bonus_ref[q, rows[i], sl]) * gate_ref[q, rows[i], sl]

    @pl.when(step == pl.num_programs(1) - 1)
    def _():
        for q in range(seqs):
            for h, sl in enumerate(_head_slices()):
                st_ref[q, h] = sts[per_step][q][sl, sl]


SCAN_SEQS_PER_STEP = 4


def _rwkv_scan(r2, y0, gt, ht, gate, bonus, ln_w, ln_b, st0, chunk):
    bsz, t, _ = r2.shape
    nc = t // chunk
    ps = _chunks_per_step(nc)
    nq = math.gcd(bsz, SCAN_SEQS_PER_STEP)
    tok = pl.BlockSpec((nq, ps * chunk, BLK), lambda bi, c: (bi, c, 0))
    mat = pl.BlockSpec((nq, ps, BLK, BLK), lambda bi, c: (bi, c, 0, 0))
    state = pl.BlockSpec((nq, HEADS, HEAD_DIM, HEAD_DIM), lambda bi, c: (bi, 0, 0, 0))
    row = pl.BlockSpec((1, BLK), lambda bi, c: (0, 0))
    return pl.pallas_call(
        functools.partial(_rwkv_scan_kernel, chunk=chunk, per_step=ps, seqs=nq),
        grid=(bsz // nq, nc // ps),
        in_specs=[tok, tok, mat, mat, tok, tok, row, row, state],
        out_specs=[tok, state],
        out_shape=[jax.ShapeDtypeStruct((bsz, t, BLK), F32),
                   jax.ShapeDtypeStruct((bsz, HEADS, HEAD_DIM, HEAD_DIM), F32)],
        scratch_shapes=[pltpu.VMEM((nq, BLK, BLK), F32)],
        compiler_params=_cparams(("parallel", "arbitrary")),
        name="rwkv_scan",
    )(r2, y0, gt, ht, gate, bonus, ln_w.reshape(1, BLK), ln_b.reshape(1, BLK), st0)


def _rwkv(h3, shift0, s0, n_valid, p, chunk, tm):
    bsz, t, cols = h3.shape
    outs = _rwkv_prep(h3.reshape(bsz * t, cols), shift0, t, n_valid,
                      p["mu"], p["w0"], p["a0"], p["k_k"], p["k_a"], p["r_k"], p["w_up"], p["a_up"], p["g_up"], tm)
    r, k, v, logd, kk, b, gate, bonus = [o.reshape(bsz, t, BLK) for o in outs]
    r2, y0, gt, ht = _rwkv_chunks(r, k, v, logd, kk, b, chunk)
    out, st = _rwkv_scan(r2, y0, gt, ht, gate, bonus, p["ln_w"], p["ln_b"], jnp.swapaxes(s0, 2, 3), chunk)
    return out, jnp.swapaxes(st, 2, 3)


def _reorder_w_in(w):
    fox_end = 3 * BRANCH_W
    ff_end = fox_end + HEADS
    gate_start = w.shape[-1] - N_BRANCH * D_MODEL
    pad = jnp.zeros(w.shape[:-1] + (BRANCH_W - HEADS,), w.dtype)
    out = jnp.concatenate([w[..., gate_start:], w[..., :fox_end], w[..., ff_end:gate_start],
                           w[..., fox_end:ff_end], pad], axis=-1)
    assert out.shape[-1] == PROJ_COLS
    return out.astype(BF16)


def _heads4(x):
    return x.reshape(x.shape[:-1] + (HEADS, HEAD_DIM))


def kernel(x_prompt, x_sample, cache_fox_k, cache_fox_v, cache_fox_logf, cache_sb_k, cache_sb_v, state_ret, state_rwkv, state_rwkv_shift, page_table, w_in, fox_f_bias, ret_gn_w, rwkv_mu, rwkv_w0, rwkv_w_up, rwkv_a0, rwkv_a_up, rwkv_g_up, rwkv_k_k, rwkv_k_a, rwkv_r_k, rwkv_ln_w, rwkv_ln_b, w_branch, w_out, norm_ffn1, ffn1_w_in, ffn1_w_out, norm_mix, norm_ffn2, ffn2_w_in, ffn2_w_out, norm_final):
    depth = w_in.shape[0]
    bp, t, d = x_prompt.shape
    bs = x_sample.shape[0]
    n_pool = cache_fox_k.shape[1]
    past_len = page_table.shape[1] * PAGE_SIZE

    w_in_r = _reorder_w_in(w_in)
    f1i, f1o, f2i, f2o = (a.astype(BF16) for a in (ffn1_w_in, ffn1_w_out, ffn2_w_in, ffn2_w_out))
    wbr = w_branch.astype(BF16)
    wo = w_out.astype(BF16)
    w_up, a_up, g_up = rwkv_w_up.astype(BF16), rwkv_a_up.astype(BF16), rwkv_g_up.astype(BF16)

    pool = lambda c: jnp.transpose(c, (0, 1, 3, 4, 2)).reshape(depth * n_pool, BLK, PAGE_SIZE)
    ck_fox, cv_fox, ck_sb, cv_sb = pool(cache_fox_k), pool(cache_fox_v), pool(cache_sb_k), pool(cache_sb_v)
    lf_t = jnp.swapaxes(cache_fox_logf, 2, 3).reshape(depth * n_pool, HEADS, PAGE_SIZE)

    cos_p, sin_p = _rope_tables(jnp.arange(t))
    s_pad = 8
    cos_s, sin_s = _rope_tables(past_len + jnp.arange(s_pad))
    ret_tab_p = _ret_tables(RET_CHUNK, RET_CHUNK)
    ret_tab_s = _ret_tables(s_pad, 1)
    zeros_state = jnp.zeros((bp, HEADS, HEAD_DIM, HEAD_DIM), F32)

    xp = x_prompt.reshape(bp * t, d)
    xs = x_sample.reshape(bs, d)
    new_p = [[] for _ in range(8)]
    new_s = [[] for _ in range(8)]
    tm_p = 1024
    tm_wide = min(2 * tm_p, bp * t)

    for l in range(depth):
        last = l == depth - 1
        rw = dict(mu=rwkv_mu[l], w0=rwkv_w0[l], a0=rwkv_a0[l], k_k=rwkv_k_k[l], k_a=rwkv_k_a[l], r_k=rwkv_r_k[l],
                  w_up=w_up[l], a_up=a_up[l], g_up=g_up[l], ln_w=rwkv_ln_w[l], ln_b=rwkv_ln_b[l])

        xp = _ffn(xp, norm_ffn1[l], f1i[l], f1o[l], norm_final, False, tm_wide)
        h = _proj(xp, norm_mix[l], w_in_r[l], tm_wide)
        h3 = h.reshape(bp, t, PROJ_COLS)
        ff_t = jnp.swapaxes(h3[:, :, COL_FORGET:COL_FORGET + HEADS], 1, 2).reshape(bp * HEADS, t)
        lf_rows, *cum_terms = _forget(ff_t, jnp.tile(fox_f_bias[l], bp)[:, None])
        by_token = lambda rows: jnp.swapaxes(rows.reshape(bp, HEADS, t), 1, 2)
        lf = by_token(lf_rows)
        terms = jnp.concatenate([by_token(c) for c in cum_terms]
                                + [jnp.zeros((bp, t, TERM_COLS - 3 * HEADS), BF16)], axis=-1)
        packed = _pack_operands(h, terms.reshape(bp * t, TERM_COLS), tm_p)
        fqa, fka, fva, sqa, ska, sva = [a.reshape(bp, t, HEADS * PACK) for a in packed]
        o_a = _fox_prompt(fqa, fka, fva)
        o_b, ret_p = _retention(h3, COL_RET, cos_p, sin_p, ret_tab_p, ret_gn_w[l], zeros_state, RET_CHUNK)
        o_c, rw_p = _rwkv(h3, jnp.zeros((bp, RWKV_IN), F32), zeros_state, t, rw, RWKV_CHUNK, tm_p)
        o_d = _sb_prompt(sqa, ska, sva)
        flat = lambda o: o.reshape(bp * t, BLK)
        xp = _merge(xp, [flat(o_a), flat(o_b), flat(o_c), flat(o_d)], h, wbr[l], wo[l], 512)
        xp = _ffn(xp, norm_ffn2[l], f2i[l], f2o[l], norm_final, last, tm_wide)
        for i, a in enumerate((_heads4(h3[:, :, COL_FOX + BLK:COL_FOX + 2 * BLK]),
                               _heads4(h3[:, :, COL_FOX + 2 * BLK:COL_FOX + 3 * BLK]), lf,
                               _heads4(h3[:, :, COL_SB + BLK:COL_SB + 2 * BLK]),
                               _heads4(h3[:, :, COL_SB + 2 * BLK:COL_SB + 3 * BLK]), ret_p, rw_p,
                               h3[:, -1, COL_RWKV:COL_RWKV + RWKV_IN])):
            new_p[i].append(a)

        xs = _ffn(xs, norm_ffn1[l], f1i[l], f1o[l], norm_final, False, bs)
        hs = _proj(xs, norm_mix[l], w_in_r[l], bs)
        col = lambda c0, n=BLK: hs[:, c0:c0 + n]
        ff_s = jnp.swapaxes(col(COL_FORGET, HEADS), 0, 1)
        ff_s = jnp.concatenate([ff_s, jnp.zeros((8 - HEADS, bs), F32)], axis=0)
        bias_s = jnp.concatenate([fox_f_bias[l], jnp.zeros((8 - HEADS,), F32)])[:, None]
        lf_s_rows = _forget(jnp.pad(ff_s, ((0, 0), (0, 128 - bs))), bias_s)[0]
        lf_s = jnp.swapaxes(lf_s_rows[:HEADS, :bs], 0, 1)
        o_a = _fox_sample(col(COL_FOX), col(COL_FOX + BLK), col(COL_FOX + 2 * BLK), lf_s,
                          ck_fox, cv_fox, lf_t, page_table, l * n_pool)
        hs_pad = jnp.pad(hs[:, None, :], ((0, 0), (0, s_pad - 1), (0, 0)))
        o_b, ret_s = _retention(hs_pad, COL_RET, cos_s, sin_s, ret_tab_s, ret_gn_w[l], state_ret[l], s_pad)
        o_c, rw_s = _rwkv(hs_pad, state_rwkv_shift[l], state_rwkv[l], 1, rw, s_pad, s_pad)
        o_d = _sb_sample(col(COL_SB), ck_sb, cv_sb, page_table, l * n_pool)
        xs = _merge(xs, [o_a, o_b[:, 0], o_c[:, 0], o_d], hs, wbr[l], wo[l], bs)
        xs = _ffn(xs, norm_ffn2[l], f2i[l], f2o[l], norm_final, last, bs)
        for i, a in enumerate((_heads4(col(COL_FOX + BLK))[:, None], _heads4(col(COL_FOX + 2 * BLK))[:, None],
                               lf_s[:, None], _heads4(col(COL_SB + BLK))[:, None],
                               _heads4(col(COL_SB + 2 * BLK))[:, None], ret_s, rw_s, col(COL_RWKV, RWKV_IN))):
            new_s[i].append(a)

    sp = [jnp.stack(v) for v in new_p]
    ss = [jnp.stack(v) for v in new_s]
    return (xp.reshape(bp, t, d), xs.reshape(bs, 1, d), sp[0], sp[1], sp[2], sp[3], sp[4], sp[5], sp[6], sp[7],
            ss[0], ss[1], ss[2], ss[3], ss[4], ss[5], ss[6], ss[7])
```

```python
import functools
import math

import jax
import jax.numpy as jnp
import numpy as np
from jax import lax
from jax.experimental import pallas as pl
from jax.experimental.pallas import tpu as pltpu

F32 = jnp.float32
BF16 = jnp.bfloat16
HIGHEST = lax.Precision.HIGHEST

D_MODEL = 1024
HEADS = 4
HEAD_DIM = 64
BRANCH_W = HEADS * HEAD_DIM
N_BRANCH = 4
DECAY_LORA = 64
AAA_LORA = 64
GATE_LORA = 128
RWKV_IN = 3 * BRANCH_W + DECAY_LORA + AAA_LORA + GATE_LORA
PAGE_SIZE = 128
RET_CHUNK = 128
ROPE_BASE = 10000.0
RMS_EPS = 1e-6
GN_EPS = 1e-5
RWKV_GN_EPS = 64e-5
QK_SCALE = HEAD_DIM ** -0.5
NEG_BIG = -1e30

COL_GATE = 0
COL_FOX = N_BRANCH * D_MODEL
COL_RET = COL_FOX + 3 * BRANCH_W
COL_RWKV = COL_RET + 4 * BRANCH_W
COL_SB = COL_RWKV + RWKV_IN
COL_FORGET = COL_SB + 3 * BRANCH_W
PROJ_TILE = 512
PROJ_COLS = -(-(COL_FORGET + HEADS) // PROJ_TILE) * PROJ_TILE
BLK = BRANCH_W

VMEM_LIMIT = 56 * 1024 * 1024
RWKV_CHUNK = 64
RWKV_CHUNKS_PER_STEP = 4


def _cparams(sem):
    return pltpu.CompilerParams(dimension_semantics=sem, vmem_limit_bytes=VMEM_LIMIT)


def _dot(a, b, precision=None):
    return jnp.dot(a, b, preferred_element_type=F32, precision=precision)


def _dot_nt(a, b, precision=None):
    return lax.dot_general(a, b, (((1,), (1,)), ((), ())), preferred_element_type=F32, precision=precision)


def _dot_tn(a, b, precision=None):
    return lax.dot_general(a, b, (((0,), (0,)), ((), ())), preferred_element_type=F32, precision=precision)


_NN = (((1,), (0,)), ((), ()))
_NT = (((1,), (1,)), ((), ()))
_TN = (((0,), (0,)), ((), ()))


def _mm(a, b, dims):
    return lax.dot_general(a.astype(BF16), b.astype(BF16), dims, preferred_element_type=F32)


def _rms(x, g):
    return x * lax.rsqrt(jnp.mean(x * x, axis=-1, keepdims=True) + RMS_EPS) * g


def _softplus_neg_abs(z):
    return jnp.log(1.0 + jnp.exp(-jnp.abs(z)))


def _head_slices():
    return [slice(h * HEAD_DIM, (h + 1) * HEAD_DIM) for h in range(HEADS)]


def _ffn_kernel(x_ref, g_ref, wa_ref, wb_ref, wo_ref, gf_ref, o_ref, xn_ref, *, final_norm):
    j = pl.program_id(1)

    @pl.when(j == 0)
    def _():
        xn_ref[...] = _rms(x_ref[...], g_ref[...]).astype(BF16)
        o_ref[...] = jnp.zeros_like(o_ref)

    xn = xn_ref[...]
    a = _dot(xn, wa_ref[...])
    b = _dot(xn, wb_ref[...])
    hmid = (a * (0.5 * jnp.tanh(0.5 * a) + 0.5)) * b
    o_ref[...] += _dot(hmid.astype(BF16), wo_ref[...])

    @pl.when(j == pl.num_programs(1) - 1)
    def _():
        y = x_ref[...] + 0.5 * o_ref[...]
        if final_norm:
            o_ref[...] = _rms(y, gf_ref[...])
        else:
            o_ref[...] = y


def _ffn(x, g, wi, wo, gf, final_norm, tm, tf=256):
    m, d = x.shape
    f = wo.shape[0]
    nf = f // tf
    kern = functools.partial(_ffn_kernel, final_norm=final_norm)
    outs = pl.pallas_call(
        kern,
        grid=(m // tm, nf),
        in_specs=[
            pl.BlockSpec((tm, d), lambda i, j: (i, 0)),
            pl.BlockSpec((1, d), lambda i, j: (0, 0)),
            pl.BlockSpec((d, tf), lambda i, j: (0, j)),
            pl.BlockSpec((d, tf), lambda i, j: (0, j + nf)),
            pl.BlockSpec((tf, d), lambda i, j: (j, 0)),
            pl.BlockSpec((1, d), lambda i, j: (0, 0)),
        ],
        out_specs=pl.BlockSpec((tm, d), lambda i, j: (i, 0)),
        out_shape=jax.ShapeDtypeStruct((m, d), F32),
        scratch_shapes=[pltpu.VMEM((tm, d), BF16)],
        compiler_params=_cparams(("parallel", "arbitrary")),
        name="ffn",
    )(x, g.reshape(1, d), wi, wi, wo, gf.reshape(1, d))
    return outs


def _proj_kernel(x_ref, g_ref, w_ref, o_ref, xn_ref):
    @pl.when(pl.program_id(1) == 0)
    def _():
        xn_ref[...] = _rms(x_ref[...], g_ref[...]).astype(BF16)

    o_ref[...] = _dot(xn_ref[...], w_ref[...])


def _proj(x, g, w, tm, tn=PROJ_TILE):
    m, d = x.shape
    n = w.shape[1]
    return pl.pallas_call(
        _proj_kernel,
        grid=(m // tm, n // tn),
        in_specs=[
            pl.BlockSpec((tm, d), lambda i, j: (i, 0)),
            pl.BlockSpec((1, d), lambda i, j: (0, 0)),
            pl.BlockSpec((d, tn), lambda i, j: (0, j)),
        ],
        out_specs=pl.BlockSpec((tm, tn), lambda i, j: (i, j)),
        out_shape=jax.ShapeDtypeStruct((m, n), F32),
        scratch_shapes=[pltpu.VMEM((tm, d), BF16)],
        compiler_params=_cparams(("parallel", "arbitrary")),
        name="proj",
    )(x, g.reshape(1, d), w)


def _merge_kernel(x_ref, oa_ref, ob_ref, oc_ref, od_ref, g0_ref, g1_ref, g2_ref, g3_ref, wb_ref, wo_ref, o_ref):
    merged = None
    for i, (o_r, g_r) in enumerate(((oa_ref, g0_ref), (ob_ref, g1_ref), (oc_ref, g2_ref), (od_ref, g3_ref))):
        gate = 0.5 * jnp.tanh(0.5 * g_r[...]) + 0.5
        t = gate * _dot(o_r[...].astype(BF16), wb_ref[i])
        merged = t if merged is None else merged + t
    o_ref[...] = x_ref[...] + _dot(merged.astype(BF16), wo_ref[...])


def _merge(x, branches, h, wb, wo, tm):
    m, d = x.shape
    gate_specs = [pl.BlockSpec((tm, d), functools.partial(lambda i, k: (i, k), k=COL_GATE // d + k)) for k in range(N_BRANCH)]
    return pl.pallas_call(
        _merge_kernel,
        grid=(m // tm,),
        in_specs=[pl.BlockSpec((tm, d), lambda i: (i, 0))]
        + [pl.BlockSpec((tm, BRANCH_W), lambda i: (i, 0))] * N_BRANCH
        + gate_specs
        + [pl.BlockSpec((N_BRANCH, BRANCH_W, d), lambda i: (0, 0, 0)), pl.BlockSpec((d, d), lambda i: (0, 0))],
        out_specs=pl.BlockSpec((tm, d), lambda i: (i, 0)),
        out_shape=jax.ShapeDtypeStruct((m, d), F32),
        compiler_params=_cparams(("parallel",)),
        name="merge",
    )(x, *branches, h, h, h, h, wb, wo)


def _forget_kernel(ff_ref, bias_ref, lf_ref, hi_ref, mid_ref, lo_ref, *, t):
    x = ff_ref[...] + bias_ref[...]
    lf = jnp.minimum(x, 0.0) - _softplus_neg_abs(x)
    lf_ref[...] = lf
    w = 128
    tri = (lax.broadcasted_iota(jnp.int32, (w, w), 0) <= lax.broadcasted_iota(jnp.int32, (w, w), 1)).astype(F32)
    carry = jnp.zeros((x.shape[0], 1), F32)
    for c in range(t // w):
        cols = slice(c * w, (c + 1) * w)
        cs = _dot(lf[:, cols], tri, HIGHEST) + carry
        carry = cs[:, w - 1:w]
        hi = cs.astype(BF16)
        r1 = cs - hi.astype(F32)
        mid = r1.astype(BF16)
        hi_ref[:, cols] = hi
        mid_ref[:, cols] = mid
        lo_ref[:, cols] = (r1 - mid.astype(F32)).astype(BF16)


def _forget(ff_t, bias_rows):
    r, t = ff_t.shape
    term = jax.ShapeDtypeStruct((r, t), BF16)
    return pl.pallas_call(
        functools.partial(_forget_kernel, t=t),
        out_shape=(jax.ShapeDtypeStruct((r, t), F32), term, term, term),
        compiler_params=pltpu.CompilerParams(vmem_limit_bytes=VMEM_LIMIT),
        name="forget",
    )(ff_t, bias_rows)


PACK = 128


TERM_COLS = 16


def _pack_kernel(fq_ref, fk_ref, fv_ref, sq_ref, sk_ref, sv_ref, terms_ref,
                 fqa_ref, fka_ref, fva_ref, sqa_ref, ska_ref, sva_ref):
    w = HEADS * PACK
    r = lax.broadcasted_iota(jnp.int32, (BLK, w), 0)
    c = lax.broadcasted_iota(jnp.int32, (BLK, w), 1)
    place = ((c // PACK == r // HEAD_DIM) & (c % PACK == r % HEAD_DIM)).astype(BF16)
    e = lax.broadcasted_iota(jnp.int32, (TERM_COLS, w), 0)
    ce = lax.broadcasted_iota(jnp.int32, (TERM_COLS, w), 1)
    term_lane = (e % HEADS) * PACK + HEAD_DIM + e // HEADS
    real = e < 3 * HEADS
    place_q = ((ce == term_lane) & real).astype(BF16)
    place_k = -((ce == term_lane + 3) & real).astype(BF16)
    lane = lax.broadcasted_iota(jnp.int32, (1, w), 1) % PACK
    ones_q = ((lane >= HEAD_DIM + 3) & (lane < HEAD_DIM + 6)).astype(F32)
    ones_k = ((lane >= HEAD_DIM) & (lane < HEAD_DIM + 3)).astype(F32)
    ones_v = (lane == HEAD_DIM).astype(F32)
    terms = terms_ref[...]

    def put(x):
        return _dot(x.astype(BF16), place)

    fqa_ref[...] = (put(fq_ref[...] * QK_SCALE) + _dot(terms, place_q) + ones_q).astype(BF16)
    fka_ref[...] = (put(fk_ref[...]) + _dot(terms, place_k) + ones_k).astype(BF16)
    fva_ref[...] = (put(fv_ref[...]) + ones_v).astype(BF16)
    sqa_ref[...] = put(sq_ref[...] * QK_SCALE).astype(BF16)
    ska_ref[...] = put(sk_ref[...]).astype(BF16)
    sva_ref[...] = put(sv_ref[...]).astype(BF16)


def _pack_operands(h, terms, tm):
    m = h.shape[0]
    w = HEADS * PACK
    colblk = lambda c0: pl.BlockSpec((tm, BLK), functools.partial(lambda i, c: (i, c), c=c0 // BLK))
    cols = [COL_FOX, COL_FOX + BLK, COL_FOX + 2 * BLK, COL_SB, COL_SB + BLK, COL_SB + 2 * BLK]
    return pl.pallas_call(
        _pack_kernel,
        grid=(m // tm,),
        in_specs=[colblk(c0) for c0 in cols] + [pl.BlockSpec((tm, TERM_COLS), lambda i: (i, 0))],
        out_specs=[pl.BlockSpec((tm, w), lambda i: (i, 0))] * 6,
        out_shape=[jax.ShapeDtypeStruct((m, w), BF16)] * 6,
        compiler_params=_cparams(("parallel",)),
        name="pack",
    )(h, h, h, h, h, h, terms)


def _causal_pairs(nq, keys_descending):
    qs, ks = [], []
    for qi in range(nq):
        order = range(qi, -1, -1) if keys_descending else range(qi + 1)
        for ki in order:
            qs.append(qi)
            ks.append(ki)
    return jnp.asarray(np.array(qs, np.int32)), jnp.asarray(np.array(ks, np.int32))


def _fox_kernel(qtab_ref, ktab_ref, q_ref, k_ref, v_ref, o_ref, m_scr, acc_scr, *, tq, seqs):
    qi = qtab_ref[pl.program_id(1)]
    ki = ktab_ref[pl.program_id(1)]
    lanes = [slice(h * PACK, (h + 1) * PACK) for h in range(HEADS)]
    pieces = [(n, h) for n in range(seqs) for h in range(HEADS)]
    ids = range(len(pieces))

    @pl.when(ki == 0)
    def _():
        m_scr[...] = jnp.full_like(m_scr, NEG_BIG)
        acc_scr[...] = jnp.zeros_like(acc_scr)

    def block(diagonal):
        s = [_dot_nt(q_ref[n, :, lanes[h]], k_ref[n, :, lanes[h]]) for n, h in pieces]
        if diagonal:
            mask = lax.broadcasted_iota(jnp.int32, (tq, tq), 1) <= lax.broadcasted_iota(jnp.int32, (tq, tq), 0)
            s = [jnp.where(mask, x, NEG_BIG) for x in s]
        m_old = [m_scr[i] for i in ids]
        m_new = [jnp.maximum(m_old[i], jnp.max(s[i], axis=-1, keepdims=True)) for i in ids]
        p = [jnp.exp(s[i] - m_new[i]).astype(BF16) for i in ids]
        pv = [_dot(p[i], v_ref[n, :, lanes[h]]) for i, (n, h) in enumerate(pieces)]
        for i in ids:
            acc_scr[i] = jnp.exp(m_old[i] - m_new[i]) * acc_scr[i] + pv[i]
            m_scr[i] = m_new[i]

    pl.when(ki < qi)(functools.partial(block, False))

    @pl.when(ki == qi)
    def _():
        block(True)
        for i, (n, h) in enumerate(pieces):
            acc = acc_scr[i]
            o_ref[n, :, h * HEAD_DIM:(h + 1) * HEAD_DIM] = acc[:, :HEAD_DIM] / acc[:, HEAD_DIM:HEAD_DIM + 1]


def _pair_call(kern, name, qa, ka, va, tq, keys_descending, scratch, seqs=1):
    b, t, w = qa.shape
    qtab, ktab = _causal_pairs(t // tq, keys_descending)
    grid_spec = pltpu.PrefetchScalarGridSpec(
        num_scalar_prefetch=2,
        grid=(b // seqs, qtab.shape[0]),
        in_specs=[
            pl.BlockSpec((seqs, tq, w), lambda bi, p, qt, kt: (bi, qt[p], 0)),
            pl.BlockSpec((seqs, tq, w), lambda bi, p, qt, kt: (bi, kt[p], 0)),
            pl.BlockSpec((seqs, tq, w), lambda bi, p, qt, kt: (bi, kt[p], 0)),
        ],
        out_specs=pl.BlockSpec((seqs, tq, BLK), lambda bi, p, qt, kt: (bi, qt[p], 0)),
        scratch_shapes=scratch,
    )
    return pl.pallas_call(
        kern,
        grid_spec=grid_spec,
        out_shape=jax.ShapeDtypeStruct((b, t, BLK), F32),
        compiler_params=_cparams(("parallel", "arbitrary")),
        name=name,
    )(qtab, ktab, qa, ka, va)


def _fox_prompt(qa, ka, va, tq=512):
    seqs = 1
    scratch = [pltpu.VMEM((seqs * HEADS, tq, 1), F32), pltpu.VMEM((seqs * HEADS, tq, PACK), F32)]
    kern = functools.partial(_fox_kernel, tq=tq, seqs=seqs)
    return _pair_call(kern, "fox_prompt", qa, ka, va, tq, False, scratch, seqs)


def _strict_upper(n):
    return lax.broadcasted_iota(jnp.int32, (n, n), 0) > lax.broadcasted_iota(jnp.int32, (n, n), 1)


def _suffix_excl(x, u_bf16):
    hi = x.astype(BF16)
    lo = (x - hi.astype(F32)).astype(BF16)
    return _dot(hi, u_bf16) + _dot(lo, u_bf16)


def _sb_kernel(qtab_ref, ktab_ref, q_ref, k_ref, v_ref, o_ref, r_scr, acc_scr, *, tq):
    qi = qtab_ref[pl.program_id(1)]
    ki = ktab_ref[pl.program_id(1)]

    @pl.when(ki == qi)
    def _():
        r_scr[...] = jnp.zeros_like(r_scr)
        acc_scr[...] = jnp.zeros_like(acc_scr)

    heads = range(HEADS)
    lanes = [slice(h * PACK, (h + 1) * PACK) for h in heads]

    def block(diagonal):
        u = _strict_upper(tq).astype(BF16)
        z = [_dot_nt(q_ref[0, :, lanes[h]], k_ref[0, :, lanes[h]]) for h in heads]
        sp = [_softplus_neg_abs(x) for x in z]
        ls = [jnp.minimum(z[h], 0.0) - sp[h] for h in heads]
        l1m = [-jnp.maximum(z[h], 0.0) - sp[h] for h in heads]
        if diagonal:
            mask = lax.broadcasted_iota(jnp.int32, (tq, tq), 1) < lax.broadcasted_iota(jnp.int32, (tq, tq), 0)
            l1m = [jnp.where(mask, x, 0.0) for x in l1m]
        excl = [_suffix_excl(x, u) for x in l1m]
        a = [jnp.exp(ls[h] + excl[h] + r_scr[h]) for h in heads]
        if diagonal:
            a = [jnp.where(mask, x, 0.0) for x in a]
        av = [_dot(a[h].astype(BF16), v_ref[0, :, lanes[h]]) for h in heads]
        for h in heads:
            acc_scr[h] += av[h]
            r_scr[h] += jnp.sum(l1m[h], axis=-1, keepdims=True)

    pl.when(ki == qi)(functools.partial(block, True))
    pl.when(ki < qi)(functools.partial(block, False))

    @pl.when(ki == 0)
    def _():
        for h, sl in enumerate(_head_slices()):
            o_ref[0, :, sl] = acc_scr[h][:, :HEAD_DIM]


def _sb_prompt(qa, ka, va, tq=256):
    scratch = [pltpu.VMEM((HEADS, tq, 1), F32), pltpu.VMEM((HEADS, tq, PACK), F32)]
    return _pair_call(functools.partial(_sb_kernel, tq=tq), "sb_prompt", qa, ka, va, tq, True, scratch)


GROUP = 8
MAX_SAMPLE_PAGES_PER_STEP = 32


def _page_scores(k_refs, qx_scr, s_scr):
    for j, k_ref in enumerate(k_refs):
        for h, sl in enumerate(_head_slices()):
            s_scr[j * GROUP + h:j * GROUP + h + 1, :] = jnp.sum(k_ref[0, sl, :] * qx_scr[sl, :], axis=0, keepdims=True)


def _page_carries(r, tot, pp):
    carries = []
    for j in range(pp):
        carries.append(r)
        r = r + tot[j * GROUP:(j + 1) * GROUP]
    return jnp.concatenate(carries, axis=0), r


def _weighted_values(w_scr, v_refs, acc_scr, scale):
    for h, sl in enumerate(_head_slices()):
        acc = acc_scr[sl, :] if scale is None else acc_scr[sl, :] * scale[h:h + 1, :]
        for j, v_ref in enumerate(v_refs):
            acc = acc + w_scr[j * GROUP + h:j * GROUP + h + 1, :] * v_ref[0, sl, :]
        acc_scr[sl, :] = acc


def _fox_sample_kernel(pt_ref, q_ref, kn_ref, vn_ref, lfn_ref, *refs, pp):
    k_refs, v_refs, lf_refs = refs[:pp], refs[pp:2 * pp], refs[2 * pp:3 * pp]
    o_ref, qx_scr, s_scr, lf_scr, p_scr, m_scr, l_scr, r_scr, acc_scr = refs[3 * pp:]
    step = pl.program_id(1)

    @pl.when(step == 0)
    def _():
        qcol = q_ref[0] * QK_SCALE
        qx_scr[...] = jnp.broadcast_to(qcol, qx_scr.shape)
        s_scr[...] = jnp.zeros_like(s_scr)
        lf_scr[...] = jnp.zeros_like(lf_scr)
        m_scr[...] = jnp.zeros_like(m_scr)
        own = qcol * kn_ref[0]
        for h, sl in enumerate(_head_slices()):
            m_scr[h:h + 1, :] = jnp.sum(own[sl], axis=0, keepdims=True)
        l_scr[...] = jnp.ones_like(l_scr)
        r_scr[...] = lfn_ref[0]
        lane = lax.broadcasted_iota(jnp.int32, acc_scr.shape, 1)
        acc_scr[...] = jnp.where(lane == 0, vn_ref[0], 0.0)

    for j, lf_ref in enumerate(lf_refs):
        lf_scr[j * GROUP:j * GROUP + HEADS, :] = lf_ref[0]
    _page_scores(k_refs, qx_scr, s_scr)
    lf_all = lf_scr[...]
    excl = _dot(lf_all, _strict_upper(PAGE_SIZE).astype(F32), HIGHEST)
    carries, r_out = _page_carries(r_scr[...], jnp.sum(lf_all, axis=-1, keepdims=True), pp)
    r_scr[...] = r_out
    s_all = s_scr[...] + excl + carries
    row_max = jnp.max(s_all, axis=-1, keepdims=True)
    m_old = m_scr[...]
    m_new = m_old
    for j in range(pp):
        m_new = jnp.maximum(m_new, row_max[j * GROUP:(j + 1) * GROUP])
    alpha = jnp.exp(m_old - m_new)
    p_all = jnp.exp(s_all - jnp.concatenate([m_new] * pp, axis=0))
    p_scr[...] = p_all
    row_sum = jnp.sum(p_all, axis=-1, keepdims=True)
    l_new = alpha * l_scr[...]
    for j in range(pp):
        l_new = l_new + row_sum[j * GROUP:(j + 1) * GROUP]
    l_scr[...] = l_new
    m_scr[...] = m_new
    _weighted_values(p_scr, v_refs, acc_scr, alpha)

    @pl.when(step == pl.num_programs(1) - 1)
    def _():
        l = l_scr[...]
        for h, sl in enumerate(_head_slices()):
            o_ref[0, sl, :] = jnp.sum(acc_scr[sl, :], axis=-1, keepdims=True) / l[h:h + 1, :]


def _sb_sample_kernel(pt_ref, q_ref, *refs, pp):
    k_refs, v_refs = refs[:pp], refs[pp:2 * pp]
    o_ref, qx_scr, z_scr, a_scr, r_scr, acc_scr = refs[2 * pp:]
    step = pl.program_id(1)

    @pl.when(step == 0)
    def _():
        qx_scr[...] = jnp.broadcast_to(q_ref[0] * QK_SCALE, qx_scr.shape)
        z_scr[...] = jnp.zeros_like(z_scr)
        r_scr[...] = jnp.zeros_like(r_scr)
        acc_scr[...] = jnp.zeros_like(acc_scr)

    _page_scores(k_refs, qx_scr, z_scr)
    z = z_scr[...]
    sp = _softplus_neg_abs(z)
    ls = jnp.minimum(z, 0.0) - sp
    l1m = -jnp.maximum(z, 0.0) - sp
    excl = _suffix_excl(l1m, _strict_upper(PAGE_SIZE).astype(BF16))
    carries, r_out = _page_carries(r_scr[...], jnp.sum(l1m, axis=-1, keepdims=True), pp)
    r_scr[...] = r_out
    a_scr[...] = jnp.exp(ls + excl + carries)
    _weighted_values(a_scr, v_refs, acc_scr, None)

    @pl.when(step == pl.num_programs(1) - 1)
    def _():
        o_ref[...] = jnp.sum(acc_scr[...], axis=-1, keepdims=True)[None]


def _pages_per_step(n_pages):
    pp = math.gcd(n_pages, MAX_SAMPLE_PAGES_PER_STEP)
    return pp


def _page_specs(page_table, base, pp, block):
    n_pages = page_table.shape[1]

    def spec(j):
        return pl.BlockSpec(block, lambda bi, st, pt: (base + pt[bi, n_pages - 1 - (st * pp + j)], 0, 0))

    return [spec(j) for j in range(pp)]


def _per_sample(bi, st, pt):
    return (bi, 0, 0)


def _fox_sample(q, k_new, v_new, lf_new, cache_kt, cache_vt, cache_lft, page_table, base):
    b, n_pages = page_table.shape
    pp = _pages_per_step(n_pages)
    col = lambda x: x[:, :, None]
    lfn = col(jnp.concatenate([lf_new, jnp.zeros((b, GROUP - HEADS), F32)], axis=1))
    kv_block = (1, BLK, PAGE_SIZE)
    grid_spec = pltpu.PrefetchScalarGridSpec(
        num_scalar_prefetch=1,
        grid=(b, n_pages // pp),
        in_specs=[pl.BlockSpec((1, BLK, 1), _per_sample)] * 3 + [pl.BlockSpec((1, GROUP, 1), _per_sample)]
        + _page_specs(page_table, base, pp, kv_block) + _page_specs(page_table, base, pp, kv_block)
        + _page_specs(page_table, base, pp, (1, HEADS, PAGE_SIZE)),
        out_specs=pl.BlockSpec((1, BLK, 1), _per_sample),
        scratch_shapes=[pltpu.VMEM((BLK, PAGE_SIZE), F32)] + [pltpu.VMEM((pp * GROUP, PAGE_SIZE), F32)] * 3
        + [pltpu.VMEM((GROUP, 1), F32)] * 3 + [pltpu.VMEM((BLK, PAGE_SIZE), F32)],
    )
    out = pl.pallas_call(
        functools.partial(_fox_sample_kernel, pp=pp),
        grid_spec=grid_spec,
        out_shape=jax.ShapeDtypeStruct((b, BLK, 1), F32),
        compiler_params=_cparams(("parallel", "arbitrary")),
        name="fox_sample",
    )(page_table, col(q), col(k_new), col(v_new), lfn, *([cache_kt] * pp), *([cache_vt] * pp), *([cache_lft] * pp))
    return out[:, :, 0]


def _sb_sample(q, cache_kt, cache_vt, page_table, base):
    b, n_pages = page_table.shape
    pp = _pages_per_step(n_pages)
    kv_block = (1, BLK, PAGE_SIZE)
    grid_spec = pltpu.PrefetchScalarGridSpec(
        num_scalar_prefetch=1,
        grid=(b, n_pages // pp),
        in_specs=[pl.BlockSpec((1, BLK, 1), _per_sample)]
        + _page_specs(page_table, base, pp, kv_block) + _page_specs(page_table, base, pp, kv_block),
        out_specs=pl.BlockSpec((1, BLK, 1), _per_sample),
        scratch_shapes=[pltpu.VMEM((BLK, PAGE_SIZE), F32)] + [pltpu.VMEM((pp * GROUP, PAGE_SIZE), F32)] * 2
        + [pltpu.VMEM((GROUP, 1), F32), pltpu.VMEM((BLK, PAGE_SIZE), F32)],
    )
    out = pl.pallas_call(
        functools.partial(_sb_sample_kernel, pp=pp),
        grid_spec=grid_spec,
        out_shape=jax.ShapeDtypeStruct((b, BLK, 1), F32),
        compiler_params=_cparams(("parallel", "arbitrary")),
        name="sb_sample",
    )(page_table, q[:, :, None], *([cache_kt] * pp), *([cache_vt] * pp))
    return out[:, :, 0]


def _rot_half(x):
    half = HEAD_DIM // 2
    lane = lax.broadcasted_iota(jnp.int32, x.shape, 1) % HEAD_DIM
    n = x.shape[1]
    return jnp.where(lane < half, -pltpu.roll(x, n - half, 1), pltpu.roll(x, half, 1))


def _ret_kernel(q_ref, k_ref, v_ref, g_ref, cos_ref, sin_ref, dmask_ref, xi_ref, zeta_ref, cd_ref, gnw_ref, s0_ref,
                o_ref, st_ref, s_scr, *, seqs):
    c = pl.program_id(1)

    @pl.when(c == 0)
    def _():
        s_scr[...] = s0_ref[...]

    cos = cos_ref[...]
    sin = sin_ref[...]
    gnw = gnw_ref[...]
    qr, kr, v, gate = [], [], [], []
    for n in range(seqs):
        q = q_ref[n]
        k = k_ref[n]
        g = g_ref[n]
        qr.append((q * cos + _rot_half(q) * sin).astype(BF16))
        kr.append((k * cos + _rot_half(k) * sin) * QK_SCALE)
        v.append(v_ref[n].astype(BF16))
        gate.append(g * jax.nn.sigmoid(g))
    pieces = [(n, h, sl) for n in range(seqs) for h, sl in enumerate(_head_slices())]
    ids = range(len(pieces))
    qh = [qr[n][:, sl] for n, h, sl in pieces]
    kh = [kr[n][:, sl] for n, h, sl in pieces]
    vh = [v[n][:, sl] for n, h, sl in pieces]
    s = [s_scr[n, h] for n, h, sl in pieces]
    att = [(_dot_nt(qh[i], kh[i].astype(BF16)) * dmask_ref[pieces[i][1]]).astype(BF16) for i in ids]
    cross = [_dot(qh[i], s[i].astype(BF16)) * xi_ref[pieces[i][1]] for i in ids]
    o = [_dot(att[i], vh[i]) + cross[i] for i in ids]
    kv = [_dot_tn((kh[i] * zeta_ref[pieces[i][1]]).astype(BF16), vh[i]) for i in ids]
    for i, (n, h, sl) in enumerate(pieces):
        s_scr[n, h] = cd_ref[h] * s[i] + kv[i]
    mu = [jnp.mean(x, axis=-1, keepdims=True) for x in o]
    d = [o[i] - mu[i] for i in ids]
    var = [jnp.mean(x * x, axis=-1, keepdims=True) for x in d]
    for i, (n, h, sl) in enumerate(pieces):
        o_ref[n, :, sl] = gate[n][:, sl] * (d[i] * lax.rsqrt(var[i] + GN_EPS) * gnw[:, sl])

    @pl.when(c == pl.num_programs(1) - 1)
    def _():
        st_ref[...] = s_scr[...]


def _ret_tables(chunk, n_valid):
    lg = np.log(1.0 - 2.0 ** (-5.0 - np.arange(HEADS, dtype=np.float64)))
    n = np.arange(chunk, dtype=np.float64)
    diff = n[:, None] - n[None, :]
    valid = (n < n_valid).astype(np.float64)
    dmask = np.where(diff[None] >= 0, np.exp(np.maximum(diff, 0.0)[None] * lg[:, None, None]), 0.0)
    dmask = dmask * valid[None, :, None] * valid[None, None, :]
    xi = np.exp((n[None, :] + 1.0) * lg[:, None]) * valid[None, :]
    zeta = np.exp((n_valid - 1.0 - n)[None, :] * lg[:, None]) * valid[None, :]
    cd = np.exp(n_valid * lg)
    f = lambda a: jnp.asarray(a.astype(np.float32))
    return f(dmask), f(xi[:, :, None]), f(zeta[:, :, None]), f(cd[:, None, None])


def _rope_tables(pos):
    half = HEAD_DIM // 2
    inv = ROPE_BASE ** (-jnp.arange(half, dtype=F32) / half)
    ang = pos.astype(F32)[:, None] * inv[None, :]
    cos = jnp.tile(jnp.cos(ang), (1, 2 * HEADS))
    sin = jnp.tile(jnp.sin(ang), (1, 2 * HEADS))
    return cos, sin


def _retention(h3, col0, cos, sin, tables, gnw, s0, chunk):
    b, t, _ = h3.shape
    nc = t // chunk
    cb = col0 // BLK
    dmask, xi, zeta, cd = tables
    nq = math.gcd(b, SCAN_SEQS_PER_STEP)

    def tok(k):
        return pl.BlockSpec((nq, chunk, BLK), functools.partial(lambda bi, c, k: (bi, c, k), k=cb + k))

    const3 = lambda bi, c: (0, 0, 0)
    return pl.pallas_call(
        functools.partial(_ret_kernel, seqs=nq),
        grid=(b // nq, nc),
        in_specs=[tok(0), tok(1), tok(2), tok(3),
                  pl.BlockSpec((chunk, BLK), lambda bi, c: (c, 0)),
                  pl.BlockSpec((chunk, BLK), lambda bi, c: (c, 0)),
                  pl.BlockSpec((HEADS, chunk, chunk), const3),
                  pl.BlockSpec((HEADS, chunk, 1), const3),
                  pl.BlockSpec((HEADS, chunk, 1), const3),
                  pl.BlockSpec((HEADS, 1, 1), const3),
                  pl.BlockSpec((1, BLK), lambda bi, c: (0, 0)),
                  pl.BlockSpec((nq, HEADS, HEAD_DIM, HEAD_DIM), lambda bi, c: (bi, 0, 0, 0))],
        out_specs=[pl.BlockSpec((nq, chunk, BLK), lambda bi, c: (bi, c, 0)),
                   pl.BlockSpec((nq, HEADS, HEAD_DIM, HEAD_DIM), lambda bi, c: (bi, 0, 0, 0))],
        out_shape=[jax.ShapeDtypeStruct((b, t, BLK), F32),
                   jax.ShapeDtypeStruct((b, HEADS, HEAD_DIM, HEAD_DIM), F32)],
        scratch_shapes=[pltpu.VMEM((nq, HEADS, HEAD_DIM, HEAD_DIM), F32)],
        compiler_params=_cparams(("parallel", "arbitrary")),
        name="retention",
    )(h3, h3, h3, h3, cos, sin, dmask, xi, zeta, cd, gnw.reshape(1, BLK), s0)


def _head_sum_matrix():
    i = lax.broadcasted_iota(jnp.int32, (BLK, BLK), 0) // HEAD_DIM
    j = lax.broadcasted_iota(jnp.int32, (BLK, BLK), 1) // HEAD_DIM
    return (i == j).astype(F32)


def _rwkv_prep_kernel(h0_ref, h1_ref, h2_ref, h3_ref, c0_ref, c1_ref, c2_ref, c3_ref, shift_ref,
                      mu_ref, w0_ref, a0_ref, kk_w_ref, ka_ref, rk_ref, wup_ref, aup_ref, gup_ref,
                      r_ref, k_ref, v_ref, logd_ref, kk_ref, b_ref, g_ref, bonus_ref, *, tm, seq_len, n_valid):
    hr = jnp.concatenate([h0_ref[...], h1_ref[...], h2_ref[...], h3_ref[...]], axis=1)
    before = jnp.concatenate([c0_ref[...], c1_ref[...], c2_ref[...], c3_ref[...]], axis=1)[7:8]
    row = lax.broadcasted_iota(jnp.int32, (tm, 1), 0)
    pos = (pl.program_id(0) * tm + row) & (seq_len - 1)
    prev = jnp.where(row == 0, before, pltpu.roll(hr, 1, 0))
    prev = jnp.where(pos == 0, shift_ref[0], prev)
    valid = (pos < n_valid).astype(F32)
    xm = hr + (prev - hr) * mu_ref[...]
    r = xm[:, 0:BLK]
    k = xm[:, BLK:2 * BLK]
    v = xm[:, 2 * BLK:3 * BLK]
    o = 3 * BLK
    wd = xm[:, o:o + DECAY_LORA]
    ad = xm[:, o + DECAY_LORA:o + DECAY_LORA + AAA_LORA]
    gd = xm[:, o + DECAY_LORA + AAA_LORA:]
    wpre = w0_ref[...] + _dot(jnp.tanh(wd).astype(BF16), wup_ref[...])
    w = jnp.minimum(wpre, 0.0) - _softplus_neg_abs(wpre) - 0.5
    a = jax.nn.sigmoid(a0_ref[...] + _dot(ad.astype(BF16), aup_ref[...]))
    g = _dot(jax.nn.sigmoid(gd).astype(BF16), gup_ref[...])
    hs = _head_sum_matrix()
    kk = k * kk_w_ref[...]
    kk = kk * lax.rsqrt(_dot(kk * kk, hs, HIGHEST) + 1e-12)
    kmod = k * (1.0 + (a - 1.0) * ka_ref[...])
    bonus = _dot(r * kmod * rk_ref[...], hs, HIGHEST) * v
    r_ref[...] = r * valid
    k_ref[...] = kmod * valid
    v_ref[...] = v * valid
    logd_ref[...] = -jnp.exp(w) * valid
    kk_ref[...] = kk * valid
    b_ref[...] = kk * a * valid
    g_ref[...] = g
    bonus_ref[...] = bonus


def _rwkv_prep(h, shift0, seq_len, n_valid, mu, w0, a0, k_k, k_a, r_k, w_up, a_up, g_up, tm):
    m = h.shape[0]
    assert seq_len & (seq_len - 1) == 0 and seq_len % tm == 0 and tm % 8 == 0
    cb = COL_RWKV // BLK
    row = lambda n: pl.BlockSpec((1, n), lambda i: (0, 0))
    full = lambda a: pl.BlockSpec(a.shape, lambda i: (0, 0))
    tokb = pl.BlockSpec((tm, BLK), lambda i: (i, 0))
    cur = [pl.BlockSpec((tm, BLK), functools.partial(lambda i, c: (i, c), c=cb + c)) for c in range(4)]
    before = [pl.BlockSpec((8, BLK), functools.partial(lambda i, c: (jnp.maximum(i * (tm // 8) - 1, 0), c), c=cb + c))
              for c in range(4)]
    return pl.pallas_call(
        functools.partial(_rwkv_prep_kernel, tm=tm, seq_len=seq_len, n_valid=n_valid),
        grid=(m // tm,),
        in_specs=cur + before + [pl.BlockSpec((1, 1, RWKV_IN), lambda i: (i * tm // seq_len, 0, 0)),
                                 row(RWKV_IN), row(BLK), row(BLK), row(BLK), row(BLK), row(BLK),
                                 full(w_up), full(a_up), full(g_up)],
        out_specs=[tokb] * 8,
        out_shape=[jax.ShapeDtypeStruct((m, BLK), F32)] * 8,
        compiler_params=_cparams(("parallel",)),
        name="rwkv_prep",
    )(h, h, h, h, h, h, h, h, shift0[:, None, :], mu.reshape(1, -1), w0.reshape(1, -1), a0.reshape(1, -1),
      k_k.reshape(1, -1), k_a.reshape(1, -1), r_k.reshape(1, -1), w_up, a_up, g_up)


def _rwkv_chunk_kernel(r_ref, k_ref, v_ref, logd_ref, kk_ref, b_ref, r2_ref, y0_ref, gt_ref, ht_ref, *, chunk, per_step):
    c = chunk
    n = HEADS * c
    ri = lax.broadcasted_iota(jnp.int32, (c, c), 0)
    ci = lax.broadcasted_iota(jnp.int32, (c, c), 1)
    cum = (ci <= ri).astype(F32)
    rr = lax.broadcasted_iota(jnp.int32, (n, n), 0)
    cc = lax.broadcasted_iota(jnp.int32, (n, n), 1)
    assert c & (c - 1) == 0
    strict = (cc & (c - 1)) < (rr & (c - 1))
    incl = (cc & (c - 1)) <= (rr & (c - 1))
    eye = (rr == cc).astype(F32)
    eye_w = (lax.broadcasted_iota(jnp.int32, (BLK, BLK), 0) == lax.broadcasted_iota(jnp.int32, (BLK, BLK), 1)).astype(F32)
    lane_head = lax.broadcasted_iota(jnp.int32, (c, BLK), 1) // HEAD_DIM

    def stack(x):
        return jnp.concatenate([jnp.where(lane_head == h, x, 0.0) for h in range(HEADS)], axis=0)

    def unstack(xw):
        out = xw[0:c]
        for h in range(1, HEADS):
            out = out + xw[h * c:(h + 1) * c]
        return out

    chunks = range(per_step)
    rows = [slice(s * c, (s + 1) * c) for s in chunks]
    lk, lr, rb, rk, vw, be_w, ke_w, w_c = [], [], [], [], [], [], [], []
    for s in chunks:
        logd = logd_ref[0, rows[s], :]
        logw = _dot(cum, logd, HIGHEST)
        logw_c = logw[c - 1:c, :]
        e_neg = jnp.exp(-logw)
        e_end = jnp.exp(logw_c - logw)
        b = b_ref[0, rows[s], :]
        k = k_ref[0, rows[s], :]
        lk.append(stack(kk_ref[0, rows[s], :] * jnp.exp(logw - logd)))
        lr.append(stack(r_ref[0, rows[s], :] * jnp.exp(logw)))
        rb.append(stack(b * e_neg))
        rk.append(stack(k * e_neg))
        be_w.append(stack(b * e_end))
        ke_w.append(stack(k * e_end))
        vw.append(stack(v_ref[0, rows[s], :]))
        w_c.append(jnp.exp(logw_c))

    lhs = [jnp.concatenate([lk[s], lr[s]], axis=0) for s in chunks]
    ab_mb = [_mm(lhs[s], rb[s], _NT) for s in chunks]
    ak_mk = [_mm(lhs[s], rk[s], _NT) for s in chunks]
    m_b = [jnp.where(incl, ab_mb[s][n:], 0.0) for s in chunks]
    am_k = [jnp.concatenate([jnp.where(strict, ak_mk[s][:n], 0.0), jnp.where(incl, ak_mk[s][n:], 0.0)], axis=0)
            for s in chunks]
    npow = [jnp.where(strict, -ab_mb[s][:n], 0.0) for s in chunks]
    tinv = [eye + npow[s] for s in chunks]
    for _ in range(int(math.log2(c)) - 1):
        npow = [_mm(npow[s], npow[s], _NN) for s in chunks]
        tinv = [tinv[s] + _mm(tinv[s], npow[s], _NN) for s in chunks]
    akv_mkv = [_mm(am_k[s], vw[s], _NN) for s in chunks]
    pq = [_mm(tinv[s], jnp.concatenate([lk[s], akv_mkv[s][:n]], axis=1), _NN) for s in chunks]
    mb_pq = [_mm(m_b[s], pq[s], _NN) for s in chunks]
    for s in chunks:
        r2_ref[0, rows[s], :] = unstack(lr[s] - mb_pq[s][:, :BLK])
        y0_ref[0, rows[s], :] = unstack(akv_mkv[s][n:] - mb_pq[s][:, BLK:])
    for s in chunks:
        p_w, q_w = pq[s][:, :BLK], pq[s][:, BLK:]
        gt_ref[0, s] = eye_w * w_c[s] - _mm(be_w[s], p_w, _TN)
        ht_ref[0, s] = _mm(ke_w[s], vw[s], _TN) - _mm(be_w[s], q_w, _TN)


def _chunks_per_step(nc):
    return math.gcd(nc, RWKV_CHUNKS_PER_STEP)


def _rwkv_chunks(r, k, v, logd, kk, b, chunk):
    bsz, t, _ = r.shape
    nc = t // chunk
    ps = _chunks_per_step(nc)
    tok = pl.BlockSpec((1, ps * chunk, BLK), lambda bi, c: (bi, c, 0))
    mat = pl.BlockSpec((1, ps, BLK, BLK), lambda bi, c: (bi, c, 0, 0))
    mat_shape = jax.ShapeDtypeStruct((bsz, nc, BLK, BLK), F32)
    return pl.pallas_call(
        functools.partial(_rwkv_chunk_kernel, chunk=chunk, per_step=ps),
        grid=(bsz, nc // ps),
        in_specs=[tok] * 6,
        out_specs=[tok, tok, mat, mat],
        out_shape=[jax.ShapeDtypeStruct((bsz, t, BLK), F32)] * 2 + [mat_shape] * 2,
        compiler_params=_cparams(("parallel", "parallel")),
        name="rwkv_chunks",
    )(r, k, v, logd, kk, b)


def _rwkv_scan_kernel(r2_ref, y0_ref, gt_ref, ht_ref, gate_ref, bonus_ref, lnw_ref, lnb_ref, s0_ref,
                      o_ref, st_ref, s_scr, *, chunk, per_step, seqs):
    step = pl.program_id(1)

    @pl.when(step == 0)
    def _():
        s_scr[...] = jnp.zeros_like(s_scr)
        for q in range(seqs):
            for h, sl in enumerate(_head_slices()):
                s_scr[q, sl, sl] = s0_ref[q, h]

    lnw = lnw_ref[...]
    lnb = lnb_ref[...]
    sts = [[s_scr[q] for q in range(seqs)]]
    for i in range(per_step):
        sts.append([_mm(gt_ref[q, i], sts[i][q], _NN) + ht_ref[q, i] for q in range(seqs)])
    for q in range(seqs):
        s_scr[q] = sts[per_step][q]
    rows = [slice(i * chunk, (i + 1) * chunk) for i in range(per_step)]
    pairs = [(q, i) for q in range(seqs) for i in range(per_step)]
    ys = [_mm(r2_ref[q, rows[i], :], sts[i][q], _NN) + y0_ref[q, rows[i], :] for q, i in pairs]
    pieces = [(n, q, i, sl) for n, (q, i) in enumerate(pairs) for sl in _head_slices()]
    yh = [ys[n][:, sl] for n, q, i, sl in pieces]
    mu = [jnp.mean(x, axis=-1, keepdims=True) for x in yh]
    d = [yh[m] - mu[m] for m in range(len(pieces))]
    var = [jnp.mean(x * x, axis=-1, keepdims=True) for x in d]
    for m, (n, q, i, sl) in enumerate(pieces):
        yn = d[m] * lax.rsqrt(var[m] + RWKV_GN_EPS) * lnw[:, sl] + lnb[:, sl]
        o_ref[q, rows[i], sl] = (yn + bonus_ref[q, rows[i], sl]) * gate_ref[q, rows[i], sl]

    @pl.when(step == pl.num_programs(1) - 1)
    def _():
        for q in range(seqs):
            for h, sl in enumerate(_head_slices()):
                st_ref[q, h] = sts[per_step][q][sl, sl]


SCAN_SEQS_PER_STEP = 4


def _rwkv_scan(r2, y0, gt, ht, gate, bonus, ln_w, ln_b, st0, chunk):
    bsz, t, _ = r2.shape
    nc = t // chunk
    ps = _chunks_per_step(nc)
    nq = math.gcd(bsz, SCAN_SEQS_PER_STEP)
    tok = pl.BlockSpec((nq, ps * chunk, BLK), lambda bi, c: (bi, c, 0))
    mat = pl.BlockSpec((nq, ps, BLK, BLK), lambda bi, c: (bi, c, 0, 0))
    state = pl.BlockSpec((nq, HEADS, HEAD_DIM, HEAD_DIM), lambda bi, c: (bi, 0, 0, 0))
    row = pl.BlockSpec((1, BLK), lambda bi, c: (0, 0))
    return pl.pallas_call(
        functools.partial(_rwkv_scan_kernel, chunk=chunk, per_step=ps, seqs=nq),
        grid=(bsz // nq, nc // ps),
        in_specs=[tok, tok, mat, mat, tok, tok, row, row, state],
        out_specs=[tok, state],
        out_shape=[jax.ShapeDtypeStruct((bsz, t, BLK), F32),
                   jax.ShapeDtypeStruct((bsz, HEADS, HEAD_DIM, HEAD_DIM), F32)],
        scratch_shapes=[pltpu.VMEM((nq, BLK, BLK), F32)],
        compiler_params=_cparams(("parallel", "arbitrary")),
        name="rwkv_scan",
    )(r2, y0, gt, ht, gate, bonus, ln_w.reshape(1, BLK), ln_b.reshape(1, BLK), st0)


def _rwkv(h3, shift0, s0, n_valid, p, chunk, tm):
    bsz, t, cols = h3.shape
    outs = _rwkv_prep(h3.reshape(bsz * t, cols), shift0, t, n_valid,
                      p["mu"], p["w0"], p["a0"], p["k_k"], p["k_a"], p["r_k"], p["w_up"], p["a_up"], p["g_up"], tm)
    r, k, v, logd, kk, b, gate, bonus = [o.reshape(bsz, t, BLK) for o in outs]
    r2, y0, gt, ht = _rwkv_chunks(r, k, v, logd, kk, b, chunk)
    out, st = _rwkv_scan(r2, y0, gt, ht, gate, bonus, p["ln_w"], p["ln_b"], jnp.swapaxes(s0, 2, 3), chunk)
    return out, jnp.swapaxes(st, 2, 3)


def _reorder_w_in(w):
    fox_end = 3 * BRANCH_W
    ff_end = fox_end + HEADS
    gate_start = w.shape[-1] - N_BRANCH * D_MODEL
    pad = jnp.zeros(w.shape[:-1] + (PROJ_COLS - COL_FORGET - HEADS,), w.dtype)
    out = jnp.concatenate([w[..., gate_start:], w[..., :fox_end], w[..., ff_end:gate_start],
                           w[..., fox_end:ff_end], pad], axis=-1)
    assert out.shape[-1] == PROJ_COLS
    return out.astype(BF16)


def _heads4(x):
    return x.reshape(x.shape[:-1] + (HEADS, HEAD_DIM))


def kernel(x_prompt, x_sample, cache_fox_k, cache_fox_v, cache_fox_logf, cache_sb_k, cache_sb_v, state_ret, state_rwkv, state_rwkv_shift, page_table, w_in, fox_f_bias, ret_gn_w, rwkv_mu, rwkv_w0, rwkv_w_up, rwkv_a0, rwkv_a_up, rwkv_g_up, rwkv_k_k, rwkv_k_a, rwkv_r_k, rwkv_ln_w, rwkv_ln_b, w_branch, w_out, norm_ffn1, ffn1_w_in, ffn1_w_out, norm_mix, norm_ffn2, ffn2_w_in, ffn2_w_out, norm_final):
    depth = w_in.shape[0]
    bp, t, d = x_prompt.shape
    bs = x_sample.shape[0]
    n_pool = cache_fox_k.shape[1]
    past_len = page_table.shape[1] * PAGE_SIZE

    w_in_r = _reorder_w_in(w_in)
    f1i, f1o, f2i, f2o = (a.astype(BF16) for a in (ffn1_w_in, ffn1_w_out, ffn2_w_in, ffn2_w_out))
    wbr = w_branch.astype(BF16)
    wo = w_out.astype(BF16)
    w_up, a_up, g_up = rwkv_w_up.astype(BF16), rwkv_a_up.astype(BF16), rwkv_g_up.astype(BF16)

    pool = lambda c: jnp.transpose(c, (0, 1, 3, 4, 2)).reshape(depth * n_pool, BLK, PAGE_SIZE)
    ck_fox, cv_fox, ck_sb, cv_sb = pool(cache_fox_k), pool(cache_fox_v), pool(cache_sb_k), pool(cache_sb_v)
    lf_t = jnp.swapaxes(cache_fox_logf, 2, 3).reshape(depth * n_pool, HEADS, PAGE_SIZE)

    cos_p, sin_p = _rope_tables(jnp.arange(t))
    s_pad = 8
    cos_s, sin_s = _rope_tables(past_len + jnp.arange(s_pad))
    ret_tab_p = _ret_tables(RET_CHUNK, RET_CHUNK)
    ret_tab_s = _ret_tables(s_pad, 1)
    zeros_state = jnp.zeros((bp, HEADS, HEAD_DIM, HEAD_DIM), F32)

    xp = x_prompt.reshape(bp * t, d)
    xs = x_sample.reshape(bs, d)
    new_p = [[] for _ in range(8)]
    new_s = [[] for _ in range(8)]
    tm_p = 1024
    tm_wide = min(2 * tm_p, bp * t)

    for l in range(depth):
        last = l == depth - 1
        rw = dict(mu=rwkv_mu[l], w0=rwkv_w0[l], a0=rwkv_a0[l], k_k=rwkv_k_k[l], k_a=rwkv_k_a[l], r_k=rwkv_r_k[l],
                  w_up=w_up[l], a_up=a_up[l], g_up=g_up[l], ln_w=rwkv_ln_w[l], ln_b=rwkv_ln_b[l])

        xp = _ffn(xp, norm_ffn1[l], f1i[l], f1o[l], norm_final, False, tm_wide)
        h = _proj(xp, norm_mix[l], w_in_r[l], tm_wide)
        h3 = h.reshape(bp, t, PROJ_COLS)
        ff_t = jnp.swapaxes(h3[:, :, COL_FORGET:COL_FORGET + HEADS], 1, 2).reshape(bp * HEADS, t)
        lf_rows, *cum_terms = _forget(ff_t, jnp.tile(fox_f_bias[l], bp)[:, None])
        by_token = lambda rows: jnp.swapaxes(rows.reshape(bp, HEADS, t), 1, 2)
        lf = by_token(lf_rows)
        terms = jnp.concatenate([by_token(c) for c in cum_terms]
                                + [jnp.zeros((bp, t, TERM_COLS - 3 * HEADS), BF16)], axis=-1)
        packed = _pack_operands(h, terms.reshape(bp * t, TERM_COLS), tm_p)
        fqa, fka, fva, sqa, ska, sva = [a.reshape(bp, t, HEADS * PACK) for a in packed]
        o_a = _fox_prompt(fqa, fka, fva)
        o_b, ret_p = _retention(h3, COL_RET, cos_p, sin_p, ret_tab_p, ret_gn_w[l], zeros_state, RET_CHUNK)
        o_c, rw_p = _rwkv(h3, jnp.zeros((bp, RWKV_IN), F32), zeros_state, t, rw, RWKV_CHUNK, tm_p)
        o_d = _sb_prompt(sqa, ska, sva)
        flat = lambda o: o.reshape(bp * t, BLK)
        xp = _merge(xp, [flat(o_a), flat(o_b), flat(o_c), flat(o_d)], h, wbr[l], wo[l], 512)
        xp = _ffn(xp, norm_ffn2[l], f2i[l], f2o[l], norm_final, last, tm_wide)
        for i, a in enumerate((_heads4(h3[:, :, COL_FOX + BLK:COL_FOX + 2 * BLK]),
                               _heads4(h3[:, :, COL_FOX + 2 * BLK:COL_FOX + 3 * BLK]), lf,
                               _heads4(h3[:, :, COL_SB + BLK:COL_SB + 2 * BLK]),
                               _heads4(h3[:, :, COL_SB + 2 * BLK:COL_SB + 3 * BLK]), ret_p, rw_p,
                               h3[:, -1, COL_RWKV:COL_RWKV + RWKV_IN])):
            new_p[i].append(a)

        xs = _ffn(xs, norm_ffn1[l], f1i[l], f1o[l], norm_final, False, bs)
        hs = _proj(xs, norm_mix[l], w_in_r[l], bs)
        col = lambda c0, n=BLK: hs[:, c0:c0 + n]
        ff_s = jnp.swapaxes(col(COL_FORGET, HEADS), 0, 1)
        ff_s = jnp.concatenate([ff_s, jnp.zeros((8 - HEADS, bs), F32)], axis=0)
        bias_s = jnp.concatenate([fox_f_bias[l], jnp.zeros((8 - HEADS,), F32)])[:, None]
        lf_s_rows = _forget(jnp.pad(ff_s, ((0, 0), (0, 128 - bs))), bias_s)[0]
        lf_s = jnp.swapaxes(lf_s_rows[:HEADS, :bs], 0, 1)
        o_a = _fox_sample(col(COL_FOX), col(COL_FOX + BLK), col(COL_FOX + 2 * BLK), lf_s,
                          ck_fox, cv_fox, lf_t, page_table, l * n_pool)
        hs_pad = jnp.pad(hs[:, None, :], ((0, 0), (0, s_pad - 1), (0, 0)))
        o_b, ret_s = _retention(hs_pad, COL_RET, cos_s, sin_s, ret_tab_s, ret_gn_w[l], state_ret[l], s_pad)
        o_c, rw_s = _rwkv(hs_pad, state_rwkv_shift[l], state_rwkv[l], 1, rw, s_pad, s_pad)
        o_d = _sb_sample(col(COL_SB), ck_sb, cv_sb, page_table, l * n_pool)
        xs = _merge(xs, [o_a, o_b[:, 0], o_c[:, 0], o_d], hs, wbr[l], wo[l], bs)
        xs = _ffn(xs, norm_ffn2[l], f2i[l], f2o[l], norm_final, last, bs)
        for i, a in enumerate((_heads4(col(COL_FOX + BLK))[:, None], _heads4(col(COL_FOX + 2 * BLK))[:, None],
                               lf_s[:, None], _heads4(col(COL_SB + BLK))[:, None],
                               _heads4(col(COL_SB + 2 * BLK))[:, None], ret_s, rw_s, col(COL_RWKV, RWKV_IN))):
            new_s[i].append(a)

    sp = [jnp.stack(v) for v in new_p]
    ss = [jnp.stack(v) for v in new_s]
    return (xp.reshape(bp, t, d), xs.reshape(bs, 1, d), sp[0], sp[1], sp[2], sp[3], sp[4], sp[5], sp[6], sp[7],
            ss[0], ss[1], ss[2], ss[3], ss[4], ss[5], ss[6], ss[7])
```

```python
import functools
import math

import jax
import jax.numpy as jnp
import numpy as np
from jax import lax
from jax.experimental import pallas as pl
from jax.experimental.pallas import tpu as pltpu

F32 = jnp.float32
BF16 = jnp.bfloat16
HIGHEST = lax.Precision.HIGHEST

D_MODEL = 1024
HEADS = 4
HEAD_DIM = 64
BRANCH_W = HEADS * HEAD_DIM
N_BRANCH = 4
DECAY_LORA = 64
AAA_LORA = 64
GATE_LORA = 128
RWKV_IN = 3 * BRANCH_W + DECAY_LORA + AAA_LORA + GATE_LORA
PAGE_SIZE = 128
RET_CHUNK = 128
ROPE_BASE = 10000.0
RMS_EPS = 1e-6
GN_EPS = 1e-5
RWKV_GN_EPS = 64e-5
QK_SCALE = HEAD_DIM ** -0.5
NEG_BIG = -1e30

COL_GATE = 0
COL_FOX = N_BRANCH * D_MODEL
COL_RET = COL_FOX + 3 * BRANCH_W
COL_RWKV = COL_RET + 4 * BRANCH_W
COL_SB = COL_RWKV + RWKV_IN
COL_FORGET = COL_SB + 3 * BRANCH_W
PROJ_TILE = 512
PROJ_COLS = -(-(COL_FORGET + HEADS) // PROJ_TILE) * PROJ_TILE
BLK = BRANCH_W

VMEM_LIMIT = 56 * 1024 * 1024
RWKV_CHUNK = 64
RWKV_CHUNKS_PER_STEP = 4


def _cparams(sem):
    return pltpu.CompilerParams(dimension_semantics=sem, vmem_limit_bytes=VMEM_LIMIT)


def _dot(a, b, precision=None):
    return jnp.dot(a, b, preferred_element_type=F32, precision=precision)


def _dot_nt(a, b, precision=None):
    return lax.dot_general(a, b, (((1,), (1,)), ((), ())), preferred_element_type=F32, precision=precision)


def _dot_tn(a, b, precision=None):
    return lax.dot_general(a, b, (((0,), (0,)), ((), ())), preferred_element_type=F32, precision=precision)


_NN = (((1,), (0,)), ((), ()))
_NT = (((1,), (1,)), ((), ()))
_TN = (((0,), (0,)), ((), ()))


def _mm(a, b, dims):
    return lax.dot_general(a.astype(BF16), b.astype(BF16), dims, preferred_element_type=F32)


def _rms(x, g):
    return x * lax.rsqrt(jnp.mean(x * x, axis=-1, keepdims=True) + RMS_EPS) * g


def _softplus_neg_abs(z):
    return jnp.log(1.0 + jnp.exp(-jnp.abs(z)))


def _head_slices():
    return [slice(h * HEAD_DIM, (h + 1) * HEAD_DIM) for h in range(HEADS)]


def _ffn_kernel(x_ref, g_ref, wa_ref, wb_ref, wo_ref, gf_ref, o_ref, xn_ref, *, final_norm):
    j = pl.program_id(1)

    @pl.when(j == 0)
    def _():
        xn_ref[...] = _rms(x_ref[...], g_ref[...]).astype(BF16)
        o_ref[...] = jnp.zeros_like(o_ref)

    xn = xn_ref[...]
    a = _dot(xn, wa_ref[...])
    b = _dot(xn, wb_ref[...])
    hmid = (a * (0.5 * jnp.tanh(0.5 * a) + 0.5)) * b
    o_ref[...] += _dot(hmid.astype(BF16), wo_ref[...])

    @pl.when(j == pl.num_programs(1) - 1)
    def _():
        y = x_ref[...] + 0.5 * o_ref[...]
        if final_norm:
            o_ref[...] = _rms(y, gf_ref[...])
        else:
            o_ref[...] = y


def _ffn(x, g, wi, wo, gf, final_norm, tm, tf=256):
    m, d = x.shape
    f = wo.shape[0]
    nf = f // tf
    kern = functools.partial(_ffn_kernel, final_norm=final_norm)
    outs = pl.pallas_call(
        kern,
        grid=(m // tm, nf),
        in_specs=[
            pl.BlockSpec((tm, d), lambda i, j: (i, 0)),
            pl.BlockSpec((1, d), lambda i, j: (0, 0)),
            pl.BlockSpec((d, tf), lambda i, j: (0, j)),
            pl.BlockSpec((d, tf), lambda i, j: (0, j + nf)),
            pl.BlockSpec((tf, d), lambda i, j: (j, 0)),
            pl.BlockSpec((1, d), lambda i, j: (0, 0)),
        ],
        out_specs=pl.BlockSpec((tm, d), lambda i, j: (i, 0)),
        out_shape=jax.ShapeDtypeStruct((m, d), F32),
        scratch_shapes=[pltpu.VMEM((tm, d), BF16)],
        compiler_params=_cparams(("parallel", "arbitrary")),
        name="ffn",
    )(x, g.reshape(1, d), wi, wi, wo, gf.reshape(1, d))
    return outs


def _proj_kernel(x_ref, g_ref, w_ref, o_ref, xn_ref):
    @pl.when(pl.program_id(1) == 0)
    def _():
        xn_ref[...] = _rms(x_ref[...], g_ref[...]).astype(BF16)

    o_ref[...] = _dot(xn_ref[...], w_ref[...])


def _proj(x, g, w, tm, tn=PROJ_TILE):
    m, d = x.shape
    n = w.shape[1]
    return pl.pallas_call(
        _proj_kernel,
        grid=(m // tm, n // tn),
        in_specs=[
            pl.BlockSpec((tm, d), lambda i, j: (i, 0)),
            pl.BlockSpec((1, d), lambda i, j: (0, 0)),
            pl.BlockSpec((d, tn), lambda i, j: (0, j)),
        ],
        out_specs=pl.BlockSpec((tm, tn), lambda i, j: (i, j)),
        out_shape=jax.ShapeDtypeStruct((m, n), F32),
        scratch_shapes=[pltpu.VMEM((tm, d), BF16)],
        compiler_params=_cparams(("parallel", "arbitrary")),
        name="proj",
    )(x, g.reshape(1, d), w)


def _merge_kernel(x_ref, oa_ref, ob_ref, oc_ref, od_ref, g0_ref, g1_ref, g2_ref, g3_ref, wb_ref, wo_ref, o_ref):
    merged = None
    for i, (o_r, g_r) in enumerate(((oa_ref, g0_ref), (ob_ref, g1_ref), (oc_ref, g2_ref), (od_ref, g3_ref))):
        gate = 0.5 * jnp.tanh(0.5 * g_r[...]) + 0.5
        t = gate * _dot(o_r[...].astype(BF16), wb_ref[i])
        merged = t if merged is None else merged + t
    o_ref[...] = x_ref[...] + _dot(merged.astype(BF16), wo_ref[...])


def _merge(x, branches, h, wb, wo, tm):
    m, d = x.shape
    gate_specs = [pl.BlockSpec((tm, d), functools.partial(lambda i, k: (i, k), k=COL_GATE // d + k)) for k in range(N_BRANCH)]
    return pl.pallas_call(
        _merge_kernel,
        grid=(m // tm,),
        in_specs=[pl.BlockSpec((tm, d), lambda i: (i, 0))]
        + [pl.BlockSpec((tm, BRANCH_W), lambda i: (i, 0))] * N_BRANCH
        + gate_specs
        + [pl.BlockSpec((N_BRANCH, BRANCH_W, d), lambda i: (0, 0, 0)), pl.BlockSpec((d, d), lambda i: (0, 0))],
        out_specs=pl.BlockSpec((tm, d), lambda i: (i, 0)),
        out_shape=jax.ShapeDtypeStruct((m, d), F32),
        compiler_params=_cparams(("parallel",)),
        name="merge",
    )(x, *branches, h, h, h, h, wb, wo)


def _forget_kernel(ff_ref, bias_ref, lf_ref, hi_ref, mid_ref, lo_ref, *, t):
    x = ff_ref[...] + bias_ref[...]
    lf = jnp.minimum(x, 0.0) - _softplus_neg_abs(x)
    lf_ref[...] = lf
    w = 128
    tri = (lax.broadcasted_iota(jnp.int32, (w, w), 0) <= lax.broadcasted_iota(jnp.int32, (w, w), 1)).astype(F32)
    carry = jnp.zeros((x.shape[0], 1), F32)
    for c in range(t // w):
        cols = slice(c * w, (c + 1) * w)
        cs = _dot(lf[:, cols], tri, HIGHEST) + carry
        carry = cs[:, w - 1:w]
        hi = cs.astype(BF16)
        r1 = cs - hi.astype(F32)
        mid = r1.astype(BF16)
        hi_ref[:, cols] = hi
        mid_ref[:, cols] = mid
        lo_ref[:, cols] = (r1 - mid.astype(F32)).astype(BF16)


def _forget(ff_t, bias_rows):
    r, t = ff_t.shape
    term = jax.ShapeDtypeStruct((r, t), BF16)
    return pl.pallas_call(
        functools.partial(_forget_kernel, t=t),
        out_shape=(jax.ShapeDtypeStruct((r, t), F32), term, term, term),
        compiler_params=pltpu.CompilerParams(vmem_limit_bytes=VMEM_LIMIT),
        name="forget",
    )(ff_t, bias_rows)


PACK = 128


TERM_COLS = 16


def _pack_kernel(fq_ref, fk_ref, fv_ref, sq_ref, sk_ref, sv_ref, terms_ref,
                 fqa_ref, fka_ref, fva_ref, sqa_ref, ska_ref, sva_ref):
    w = HEADS * PACK
    r = lax.broadcasted_iota(jnp.int32, (BLK, w), 0)
    c = lax.broadcasted_iota(jnp.int32, (BLK, w), 1)
    place = ((c // PACK == r // HEAD_DIM) & (c % PACK == r % HEAD_DIM)).astype(BF16)
    e = lax.broadcasted_iota(jnp.int32, (TERM_COLS, w), 0)
    ce = lax.broadcasted_iota(jnp.int32, (TERM_COLS, w), 1)
    term_lane = (e % HEADS) * PACK + HEAD_DIM + e // HEADS
    real = e < 3 * HEADS
    place_q = ((ce == term_lane) & real).astype(BF16)
    place_k = -((ce == term_lane + 3) & real).astype(BF16)
    lane = lax.broadcasted_iota(jnp.int32, (1, w), 1) % PACK
    ones_q = ((lane >= HEAD_DIM + 3) & (lane < HEAD_DIM + 6)).astype(F32)
    ones_k = ((lane >= HEAD_DIM) & (lane < HEAD_DIM + 3)).astype(F32)
    ones_v = (lane == HEAD_DIM).astype(F32)
    terms = terms_ref[...]

    def put(x):
        return _dot(x.astype(BF16), place)

    fqa_ref[...] = (put(fq_ref[...] * QK_SCALE) + _dot(terms, place_q) + ones_q).astype(BF16)
    fka_ref[...] = (put(fk_ref[...]) + _dot(terms, place_k) + ones_k).astype(BF16)
    fva_ref[...] = (put(fv_ref[...]) + ones_v).astype(BF16)
    sqa_ref[...] = put(sq_ref[...] * QK_SCALE).astype(BF16)
    ska_ref[...] = put(sk_ref[...]).astype(BF16)
    sva_ref[...] = put(sv_ref[...]).astype(BF16)


def _pack_operands(h, terms, tm):
    m = h.shape[0]
    w = HEADS * PACK
    colblk = lambda c0: pl.BlockSpec((tm, BLK), functools.partial(lambda i, c: (i, c), c=c0 // BLK))
    cols = [COL_FOX, COL_FOX + BLK, COL_FOX + 2 * BLK, COL_SB, COL_SB + BLK, COL_SB + 2 * BLK]
    return pl.pallas_call(
        _pack_kernel,
        grid=(m // tm,),
        in_specs=[colblk(c0) for c0 in cols] + [pl.BlockSpec((tm, TERM_COLS), lambda i: (i, 0))],
        out_specs=[pl.BlockSpec((tm, w), lambda i: (i, 0))] * 6,
        out_shape=[jax.ShapeDtypeStruct((m, w), BF16)] * 6,
        compiler_params=_cparams(("parallel",)),
        name="pack",
    )(h, h, h, h, h, h, terms)


def _causal_pairs(nq, keys_descending):
    qs, ks = [], []
    for qi in range(nq):
        order = range(qi, -1, -1) if keys_descending else range(qi + 1)
        for ki in order:
            qs.append(qi)
            ks.append(ki)
    return jnp.asarray(np.array(qs, np.int32)), jnp.asarray(np.array(ks, np.int32))


def _fox_kernel(qtab_ref, ktab_ref, q_ref, k_ref, v_ref, o_ref, m_scr, acc_scr, *, tq, seqs):
    qi = qtab_ref[pl.program_id(1)]
    ki = ktab_ref[pl.program_id(1)]
    lanes = [slice(h * PACK, (h + 1) * PACK) for h in range(HEADS)]
    pieces = [(n, h) for n in range(seqs) for h in range(HEADS)]
    ids = range(len(pieces))

    @pl.when(ki == 0)
    def _():
        m_scr[...] = jnp.full_like(m_scr, NEG_BIG)
        acc_scr[...] = jnp.zeros_like(acc_scr)

    def block(diagonal):
        s = [_dot_nt(q_ref[n, :, lanes[h]], k_ref[n, :, lanes[h]]) for n, h in pieces]
        if diagonal:
            mask = lax.broadcasted_iota(jnp.int32, (tq, tq), 1) <= lax.broadcasted_iota(jnp.int32, (tq, tq), 0)
            s = [jnp.where(mask, x, NEG_BIG) for x in s]
        m_old = [m_scr[i] for i in ids]
        m_new = [jnp.maximum(m_old[i], jnp.max(s[i], axis=-1, keepdims=True)) for i in ids]
        p = [jnp.exp(s[i] - m_new[i]).astype(BF16) for i in ids]
        pv = [_dot(p[i], v_ref[n, :, lanes[h]]) for i, (n, h) in enumerate(pieces)]
        for i in ids:
            acc_scr[i] = jnp.exp(m_old[i] - m_new[i]) * acc_scr[i] + pv[i]
            m_scr[i] = m_new[i]

    pl.when(ki < qi)(functools.partial(block, False))

    @pl.when(ki == qi)
    def _():
        block(True)
        for i, (n, h) in enumerate(pieces):
            acc = acc_scr[i]
            o_ref[n, :, h * HEAD_DIM:(h + 1) * HEAD_DIM] = acc[:, :HEAD_DIM] / acc[:, HEAD_DIM:HEAD_DIM + 1]


def _pair_call(kern, name, qa, ka, va, tq, keys_descending, scratch, seqs=1):
    b, t, w = qa.shape
    qtab, ktab = _causal_pairs(t // tq, keys_descending)
    grid_spec = pltpu.PrefetchScalarGridSpec(
        num_scalar_prefetch=2,
        grid=(b // seqs, qtab.shape[0]),
        in_specs=[
            pl.BlockSpec((seqs, tq, w), lambda bi, p, qt, kt: (bi, qt[p], 0)),
            pl.BlockSpec((seqs, tq, w), lambda bi, p, qt, kt: (bi, kt[p], 0)),
            pl.BlockSpec((seqs, tq, w), lambda bi, p, qt, kt: (bi, kt[p], 0)),
        ],
        out_specs=pl.BlockSpec((seqs, tq, BLK), lambda bi, p, qt, kt: (bi, qt[p], 0)),
        scratch_shapes=scratch,
    )
    return pl.pallas_call(
        kern,
        grid_spec=grid_spec,
        out_shape=jax.ShapeDtypeStruct((b, t, BLK), F32),
        compiler_params=_cparams(("parallel", "arbitrary")),
        name=name,
    )(qtab, ktab, qa, ka, va)


def _fox_prompt(qa, ka, va, tq=512):
    seqs = 1
    scratch = [pltpu.VMEM((seqs * HEADS, tq, 1), F32), pltpu.VMEM((seqs * HEADS, tq, PACK), F32)]
    kern = functools.partial(_fox_kernel, tq=tq, seqs=seqs)
    return _pair_call(kern, "fox_prompt", qa, ka, va, tq, False, scratch, seqs)


def _strict_upper(n):
    return lax.broadcasted_iota(jnp.int32, (n, n), 0) > lax.broadcasted_iota(jnp.int32, (n, n), 1)


def _suffix_excl(x, u_bf16):
    hi = x.astype(BF16)
    lo = (x - hi.astype(F32)).astype(BF16)
    return _dot(hi, u_bf16) + _dot(lo, u_bf16)


def _sb_kernel(qtab_ref, ktab_ref, q_ref, k_ref, v_ref, o_ref, r_scr, acc_scr, *, tq):
    qi = qtab_ref[pl.program_id(1)]
    ki = ktab_ref[pl.program_id(1)]

    @pl.when(ki == qi)
    def _():
        r_scr[...] = jnp.zeros_like(r_scr)
        acc_scr[...] = jnp.zeros_like(acc_scr)

    heads = range(HEADS)
    lanes = [slice(h * PACK, (h + 1) * PACK) for h in heads]

    def block(diagonal):
        u = _strict_upper(tq).astype(BF16)
        z = [_dot_nt(q_ref[0, :, lanes[h]], k_ref[0, :, lanes[h]]) for h in heads]
        sp = [_softplus_neg_abs(x) for x in z]
        ls = [jnp.minimum(z[h], 0.0) - sp[h] for h in heads]
        l1m = [-jnp.maximum(z[h], 0.0) - sp[h] for h in heads]
        if diagonal:
            mask = lax.broadcasted_iota(jnp.int32, (tq, tq), 1) < lax.broadcasted_iota(jnp.int32, (tq, tq), 0)
            l1m = [jnp.where(mask, x, 0.0) for x in l1m]
        excl = [_suffix_excl(x, u) for x in l1m]
        a = [jnp.exp(ls[h] + excl[h] + r_scr[h]) for h in heads]
        if diagonal:
            a = [jnp.where(mask, x, 0.0) for x in a]
        av = [_dot(a[h].astype(BF16), v_ref[0, :, lanes[h]]) for h in heads]
        for h in heads:
            acc_scr[h] += av[h]
            r_scr[h] += jnp.sum(l1m[h], axis=-1, keepdims=True)

    pl.when(ki == qi)(functools.partial(block, True))
    pl.when(ki < qi)(functools.partial(block, False))

    @pl.when(ki == 0)
    def _():
        for h, sl in enumerate(_head_slices()):
            o_ref[0, :, sl] = acc_scr[h][:, :HEAD_DIM]


def _sb_prompt(qa, ka, va, tq=256):
    scratch = [pltpu.VMEM((HEADS, tq, 1), F32), pltpu.VMEM((HEADS, tq, PACK), F32)]
    return _pair_call(functools.partial(_sb_kernel, tq=tq), "sb_prompt", qa, ka, va, tq, True, scratch)


GROUP = 8
MAX_SAMPLE_PAGES_PER_STEP = 64


def _page_scores(k_refs, qx_scr, s_scr):
    for j, k_ref in enumerate(k_refs):
        for h, sl in enumerate(_head_slices()):
            s_scr[j * GROUP + h:j * GROUP + h + 1, :] = jnp.sum(k_ref[0, sl, :] * qx_scr[sl, :], axis=0, keepdims=True)


def _page_carries(r, tot, pp):
    carries = []
    for j in range(pp):
        carries.append(r)
        r = r + tot[j * GROUP:(j + 1) * GROUP]
    return jnp.concatenate(carries, axis=0), r


def _weighted_values(w_scr, v_refs, acc_scr, scale):
    for h, sl in enumerate(_head_slices()):
        acc = acc_scr[sl, :] if scale is None else acc_scr[sl, :] * scale[h:h + 1, :]
        for j, v_ref in enumerate(v_refs):
            acc = acc + w_scr[j * GROUP + h:j * GROUP + h + 1, :] * v_ref[0, sl, :]
        acc_scr[sl, :] = acc


def _fox_sample_kernel(pt_ref, q_ref, kn_ref, vn_ref, lfn_ref, *refs, pp):
    k_refs, v_refs, lf_refs = refs[:pp], refs[pp:2 * pp], refs[2 * pp:3 * pp]
    o_ref, qx_scr, s_scr, lf_scr, p_scr, m_scr, l_scr, r_scr, acc_scr = refs[3 * pp:]
    step = pl.program_id(1)

    @pl.when(step == 0)
    def _():
        qcol = q_ref[0] * QK_SCALE
        qx_scr[...] = jnp.broadcast_to(qcol, qx_scr.shape)
        s_scr[...] = jnp.zeros_like(s_scr)
        lf_scr[...] = jnp.zeros_like(lf_scr)
        m_scr[...] = jnp.zeros_like(m_scr)
        own = qcol * kn_ref[0]
        for h, sl in enumerate(_head_slices()):
            m_scr[h:h + 1, :] = jnp.sum(own[sl], axis=0, keepdims=True)
        l_scr[...] = jnp.ones_like(l_scr)
        r_scr[...] = lfn_ref[0]
        lane = lax.broadcasted_iota(jnp.int32, acc_scr.shape, 1)
        acc_scr[...] = jnp.where(lane == 0, vn_ref[0], 0.0)

    for j, lf_ref in enumerate(lf_refs):
        lf_scr[j * GROUP:j * GROUP + HEADS, :] = lf_ref[0]
    _page_scores(k_refs, qx_scr, s_scr)
    lf_all = lf_scr[...]
    excl = _dot(lf_all, _strict_upper(PAGE_SIZE).astype(F32), HIGHEST)
    carries, r_out = _page_carries(r_scr[...], jnp.sum(lf_all, axis=-1, keepdims=True), pp)
    r_scr[...] = r_out
    s_all = s_scr[...] + excl + carries
    row_max = jnp.max(s_all, axis=-1, keepdims=True)
    m_old = m_scr[...]
    m_new = m_old
    for j in range(pp):
        m_new = jnp.maximum(m_new, row_max[j * GROUP:(j + 1) * GROUP])
    alpha = jnp.exp(m_old - m_new)
    p_all = jnp.exp(s_all - jnp.concatenate([m_new] * pp, axis=0))
    p_scr[...] = p_all
    row_sum = jnp.sum(p_all, axis=-1, keepdims=True)
    l_new = alpha * l_scr[...]
    for j in range(pp):
        l_new = l_new + row_sum[j * GROUP:(j + 1) * GROUP]
    l_scr[...] = l_new
    m_scr[...] = m_new
    _weighted_values(p_scr, v_refs, acc_scr, alpha)

    @pl.when(step == pl.num_programs(1) - 1)
    def _():
        l = l_scr[...]
        for h, sl in enumerate(_head_slices()):
            o_ref[0, sl, :] = jnp.sum(acc_scr[sl, :], axis=-1, keepdims=True) / l[h:h + 1, :]


def _sb_sample_kernel(pt_ref, q_ref, *refs, pp):
    k_refs, v_refs = refs[:pp], refs[pp:2 * pp]
    o_ref, qx_scr, z_scr, a_scr, r_scr, acc_scr = refs[2 * pp:]
    step = pl.program_id(1)

    @pl.when(step == 0)
    def _():
        qx_scr[...] = jnp.broadcast_to(q_ref[0] * QK_SCALE, qx_scr.shape)
        z_scr[...] = jnp.zeros_like(z_scr)
        r_scr[...] = jnp.zeros_like(r_scr)
        acc_scr[...] = jnp.zeros_like(acc_scr)

    _page_scores(k_refs, qx_scr, z_scr)
    z = z_scr[...]
    sp = _softplus_neg_abs(z)
    ls = jnp.minimum(z, 0.0) - sp
    l1m = -jnp.maximum(z, 0.0) - sp
    excl = _suffix_excl(l1m, _strict_upper(PAGE_SIZE).astype(BF16))
    carries, r_out = _page_carries(r_scr[...], jnp.sum(l1m, axis=-1, keepdims=True), pp)
    r_scr[...] = r_out
    a_scr[...] = jnp.exp(ls + excl + carries)
    _weighted_values(a_scr, v_refs, acc_scr, None)

    @pl.when(step == pl.num_programs(1) - 1)
    def _():
        o_ref[...] = jnp.sum(acc_scr[...], axis=-1, keepdims=True)[None]


def _pages_per_step(n_pages):
    pp = math.gcd(n_pages, MAX_SAMPLE_PAGES_PER_STEP)
    return pp


def _page_specs(page_table, base, pp, block):
    n_pages = page_table.shape[1]

    def spec(j):
        return pl.BlockSpec(block, lambda bi, st, pt: (base + pt[bi, n_pages - 1 - (st * pp + j)], 0, 0))

    return [spec(j) for j in range(pp)]


def _per_sample(bi, st, pt):
    return (bi, 0, 0)


def _fox_sample(q, k_new, v_new, lf_new, cache_kt, cache_vt, cache_lft, page_table, base):
    b, n_pages = page_table.shape
    pp = _pages_per_step(n_pages)
    col = lambda x: x[:, :, None]
    lfn = col(jnp.concatenate([lf_new, jnp.zeros((b, GROUP - HEADS), F32)], axis=1))
    kv_block = (1, BLK, PAGE_SIZE)
    grid_spec = pltpu.PrefetchScalarGridSpec(
        num_scalar_prefetch=1,
        grid=(b, n_pages // pp),
        in_specs=[pl.BlockSpec((1, BLK, 1), _per_sample)] * 3 + [pl.BlockSpec((1, GROUP, 1), _per_sample)]
        + _page_specs(page_table, base, pp, kv_block) + _page_specs(page_table, base, pp, kv_block)
        + _page_specs(page_table, base, pp, (1, HEADS, PAGE_SIZE)),
        out_specs=pl.BlockSpec((1, BLK, 1), _per_sample),
        scratch_shapes=[pltpu.VMEM((BLK, PAGE_SIZE), F32)] + [pltpu.VMEM((pp * GROUP, PAGE_SIZE), F32)] * 3
        + [pltpu.VMEM((GROUP, 1), F32)] * 3 + [pltpu.VMEM((BLK, PAGE_SIZE), F32)],
    )
    out = pl.pallas_call(
        functools.partial(_fox_sample_kernel, pp=pp),
        grid_spec=grid_spec,
        out_shape=jax.ShapeDtypeStruct((b, BLK, 1), F32),
        compiler_params=_cparams(("parallel", "arbitrary")),
        name="fox_sample",
    )(page_table, col(q), col(k_new), col(v_new), lfn, *([cache_kt] * pp), *([cache_vt] * pp), *([cache_lft] * pp))
    return out[:, :, 0]


def _sb_sample(q, cache_kt, cache_vt, page_table, base):
    b, n_pages = page_table.shape
    pp = _pages_per_step(n_pages)
    kv_block = (1, BLK, PAGE_SIZE)
    grid_spec = pltpu.PrefetchScalarGridSpec(
        num_scalar_prefetch=1,
        grid=(b, n_pages // pp),
        in_specs=[pl.BlockSpec((1, BLK, 1), _per_sample)]
        + _page_specs(page_table, base, pp, kv_block) + _page_specs(page_table, base, pp, kv_block),
        out_specs=pl.BlockSpec((1, BLK, 1), _per_sample),
        scratch_shapes=[pltpu.VMEM((BLK, PAGE_SIZE), F32)] + [pltpu.VMEM((pp * GROUP, PAGE_SIZE), F32)] * 2
        + [pltpu.VMEM((GROUP, 1), F32), pltpu.VMEM((BLK, PAGE_SIZE), F32)],
    )
    out = pl.pallas_call(
        functools.partial(_sb_sample_kernel, pp=pp),
        grid_spec=grid_spec,
        out_shape=jax.ShapeDtypeStruct((b, BLK, 1), F32),
        compiler_params=_cparams(("parallel", "arbitrary")),
        name="sb_sample",
    )(page_table, q[:, :, None], *([cache_kt] * pp), *([cache_vt] * pp))
    return out[:, :, 0]


def _rot_half(x):
    half = HEAD_DIM // 2
    lane = lax.broadcasted_iota(jnp.int32, x.shape, 1) % HEAD_DIM
    n = x.shape[1]
    return jnp.where(lane < half, -pltpu.roll(x, n - half, 1), pltpu.roll(x, half, 1))


def _ret_kernel(q_ref, k_ref, v_ref, g_ref, cos_ref, sin_ref, dmask_ref, xi_ref, zeta_ref, cd_ref, gnw_ref, s0_ref,
                o_ref, st_ref, s_scr, *, seqs):
    c = pl.program_id(1)

    @pl.when(c == 0)
    def _():
        s_scr[...] = s0_ref[...]

    cos = cos_ref[...]
    sin = sin_ref[...]
    gnw = gnw_ref[...]
    qr, kr, v, gate = [], [], [], []
    for n in range(seqs):
        q = q_ref[n]
        k = k_ref[n]
        g = g_ref[n]
        qr.append((q * cos + _rot_half(q) * sin).astype(BF16))
        kr.append((k * cos + _rot_half(k) * sin) * QK_SCALE)
        v.append(v_ref[n].astype(BF16))
        gate.append(g * jax.nn.sigmoid(g))
    pieces = [(n, h, sl) for n in range(seqs) for h, sl in enumerate(_head_slices())]
    ids = range(len(pieces))
    qh = [qr[n][:, sl] for n, h, sl in pieces]
    kh = [kr[n][:, sl] for n, h, sl in pieces]
    vh = [v[n][:, sl] for n, h, sl in pieces]
    s = [s_scr[n, h] for n, h, sl in pieces]
    att = [(_dot_nt(qh[i], kh[i].astype(BF16)) * dmask_ref[pieces[i][1]]).astype(BF16) for i in ids]
    cross = [_dot(qh[i], s[i].astype(BF16)) * xi_ref[pieces[i][1]] for i in ids]
    o = [_dot(att[i], vh[i]) + cross[i] for i in ids]
    kv = [_dot_tn((kh[i] * zeta_ref[pieces[i][1]]).astype(BF16), vh[i]) for i in ids]
    for i, (n, h, sl) in enumerate(pieces):
        s_scr[n, h] = cd_ref[h] * s[i] + kv[i]
    mu = [jnp.mean(x, axis=-1, keepdims=True) for x in o]
    d = [o[i] - mu[i] for i in ids]
    var = [jnp.mean(x * x, axis=-1, keepdims=True) for x in d]
    for i, (n, h, sl) in enumerate(pieces):
        o_ref[n, :, sl] = gate[n][:, sl] * (d[i] * lax.rsqrt(var[i] + GN_EPS) * gnw[:, sl])

    @pl.when(c == pl.num_programs(1) - 1)
    def _():
        st_ref[...] = s_scr[...]


def _ret_tables(chunk, n_valid):
    lg = np.log(1.0 - 2.0 ** (-5.0 - np.arange(HEADS, dtype=np.float64)))
    n = np.arange(chunk, dtype=np.float64)
    diff = n[:, None] - n[None, :]
    valid = (n < n_valid).astype(np.float64)
    dmask = np.where(diff[None] >= 0, np.exp(np.maximum(diff, 0.0)[None] * lg[:, None, None]), 0.0)
    dmask = dmask * valid[None, :, None] * valid[None, None, :]
    xi = np.exp((n[None, :] + 1.0) * lg[:, None]) * valid[None, :]
    zeta = np.exp((n_valid - 1.0 - n)[None, :] * lg[:, None]) * valid[None, :]
    cd = np.exp(n_valid * lg)
    f = lambda a: jnp.asarray(a.astype(np.float32))
    return f(dmask), f(xi[:, :, None]), f(zeta[:, :, None]), f(cd[:, None, None])


def _rope_tables(pos):
    half = HEAD_DIM // 2
    inv = ROPE_BASE ** (-jnp.arange(half, dtype=F32) / half)
    ang = pos.astype(F32)[:, None] * inv[None, :]
    cos = jnp.tile(jnp.cos(ang), (1, 2 * HEADS))
    sin = jnp.tile(jnp.sin(ang), (1, 2 * HEADS))
    return cos, sin


def _retention(h3, col0, cos, sin, tables, gnw, s0, chunk):
    b, t, _ = h3.shape
    nc = t // chunk
    cb = col0 // BLK
    dmask, xi, zeta, cd = tables
    nq = math.gcd(b, SCAN_SEQS_PER_STEP)

    def tok(k):
        return pl.BlockSpec((nq, chunk, BLK), functools.partial(lambda bi, c, k: (bi, c, k), k=cb + k))

    const3 = lambda bi, c: (0, 0, 0)
    return pl.pallas_call(
        functools.partial(_ret_kernel, seqs=nq),
        grid=(b // nq, nc),
        in_specs=[tok(0), tok(1), tok(2), tok(3),
                  pl.BlockSpec((chunk, BLK), lambda bi, c: (c, 0)),
                  pl.BlockSpec((chunk, BLK), lambda bi, c: (c, 0)),
                  pl.BlockSpec((HEADS, chunk, chunk), const3),
                  pl.BlockSpec((HEADS, chunk, 1), const3),
                  pl.BlockSpec((HEADS, chunk, 1), const3),
                  pl.BlockSpec((HEADS, 1, 1), const3),
                  pl.BlockSpec((1, BLK), lambda bi, c: (0, 0)),
                  pl.BlockSpec((nq, HEADS, HEAD_DIM, HEAD_DIM), lambda bi, c: (bi, 0, 0, 0))],
        out_specs=[pl.BlockSpec((nq, chunk, BLK), lambda bi, c: (bi, c, 0)),
                   pl.BlockSpec((nq, HEADS, HEAD_DIM, HEAD_DIM), lambda bi, c: (bi, 0, 0, 0))],
        out_shape=[jax.ShapeDtypeStruct((b, t, BLK), F32),
                   jax.ShapeDtypeStruct((b, HEADS, HEAD_DIM, HEAD_DIM), F32)],
        scratch_shapes=[pltpu.VMEM((nq, HEADS, HEAD_DIM, HEAD_DIM), F32)],
        compiler_params=_cparams(("parallel", "arbitrary")),
        name="retention",
    )(h3, h3, h3, h3, cos, sin, dmask, xi, zeta, cd, gnw.reshape(1, BLK), s0)


def _head_sum_matrix():
    i = lax.broadcasted_iota(jnp.int32, (BLK, BLK), 0) // HEAD_DIM
    j = lax.broadcasted_iota(jnp.int32, (BLK, BLK), 1) // HEAD_DIM
    return (i == j).astype(F32)


def _rwkv_prep_kernel(h0_ref, h1_ref, h2_ref, h3_ref, c0_ref, c1_ref, c2_ref, c3_ref, shift_ref,
                      mu_ref, w0_ref, a0_ref, kk_w_ref, ka_ref, rk_ref, wup_ref, aup_ref, gup_ref,
                      r_ref, k_ref, v_ref, logd_ref, kk_ref, b_ref, g_ref, bonus_ref, *, tm, seq_len, n_valid):
    hr = jnp.concatenate([h0_ref[...], h1_ref[...], h2_ref[...], h3_ref[...]], axis=1)
    before = jnp.concatenate([c0_ref[...], c1_ref[...], c2_ref[...], c3_ref[...]], axis=1)[7:8]
    row = lax.broadcasted_iota(jnp.int32, (tm, 1), 0)
    pos = (pl.program_id(0) * tm + row) & (seq_len - 1)
    prev = jnp.where(row == 0, before, pltpu.roll(hr, 1, 0))
    prev = jnp.where(pos == 0, shift_ref[0], prev)
    valid = (pos < n_valid).astype(F32)
    xm = hr + (prev - hr) * mu_ref[...]
    r = xm[:, 0:BLK]
    k = xm[:, BLK:2 * BLK]
    v = xm[:, 2 * BLK:3 * BLK]
    o = 3 * BLK
    wd = xm[:, o:o + DECAY_LORA]
    ad = xm[:, o + DECAY_LORA:o + DECAY_LORA + AAA_LORA]
    gd = xm[:, o + DECAY_LORA + AAA_LORA:]
    wpre = w0_ref[...] + _dot(jnp.tanh(wd).astype(BF16), wup_ref[...])
    w = jnp.minimum(wpre, 0.0) - _softplus_neg_abs(wpre) - 0.5
    a = jax.nn.sigmoid(a0_ref[...] + _dot(ad.astype(BF16), aup_ref[...]))
    g = _dot(jax.nn.sigmoid(gd).astype(BF16), gup_ref[...])
    hs = _head_sum_matrix()
    kk = k * kk_w_ref[...]
    kk = kk * lax.rsqrt(_dot(kk * kk, hs, HIGHEST) + 1e-12)
    kmod = k * (1.0 + (a - 1.0) * ka_ref[...])
    bonus = _dot(r * kmod * rk_ref[...], hs, HIGHEST) * v
    r_ref[...] = r * valid
    k_ref[...] = kmod * valid
    v_ref[...] = v * valid
    logd_ref[...] = -jnp.exp(w) * valid
    kk_ref[...] = kk * valid
    b_ref[...] = kk * a * valid
    g_ref[...] = g
    bonus_ref[...] = bonus


def _rwkv_prep(h, shift0, seq_len, n_valid, mu, w0, a0, k_k, k_a, r_k, w_up, a_up, g_up, tm):
    m = h.shape[0]
    assert seq_len & (seq_len - 1) == 0 and seq_len % tm == 0 and tm % 8 == 0
    cb = COL_RWKV // BLK
    row = lambda n: pl.BlockSpec((1, n), lambda i: (0, 0))
    full = lambda a: pl.BlockSpec(a.shape, lambda i: (0, 0))
    tokb = pl.BlockSpec((tm, BLK), lambda i: (i, 0))
    cur = [pl.BlockSpec((tm, BLK), functools.partial(lambda i, c: (i, c), c=cb + c)) for c in range(4)]
    before = [pl.BlockSpec((8, BLK), functools.partial(lambda i, c: (jnp.maximum(i * (tm // 8) - 1, 0), c), c=cb + c))
              for c in range(4)]
    return pl.pallas_call(
        functools.partial(_rwkv_prep_kernel, tm=tm, seq_len=seq_len, n_valid=n_valid),
        grid=(m // tm,),
        in_specs=cur + before + [pl.BlockSpec((1, 1, RWKV_IN), lambda i: (i * tm // seq_len, 0, 0)),
                                 row(RWKV_IN), row(BLK), row(BLK), row(BLK), row(BLK), row(BLK),
                                 full(w_up), full(a_up), full(g_up)],
        out_specs=[tokb] * 8,
        out_shape=[jax.ShapeDtypeStruct((m, BLK), F32)] * 8,
        compiler_params=_cparams(("parallel",)),
        name="rwkv_prep",
    )(h, h, h, h, h, h, h, h, shift0[:, None, :], mu.reshape(1, -1), w0.reshape(1, -1), a0.reshape(1, -1),
      k_k.reshape(1, -1), k_a.reshape(1, -1), r_k.reshape(1, -1), w_up, a_up, g_up)


def _rwkv_chunk_kernel(r_ref, k_ref, v_ref, logd_ref, kk_ref, b_ref, r2_ref, y0_ref, gt_ref, ht_ref, *, chunk, per_step):
    c = chunk
    n = HEADS * c
    ri = lax.broadcasted_iota(jnp.int32, (c, c), 0)
    ci = lax.broadcasted_iota(jnp.int32, (c, c), 1)
    cum = (ci <= ri).astype(F32)
    rr = lax.broadcasted_iota(jnp.int32, (n, n), 0)
    cc = lax.broadcasted_iota(jnp.int32, (n, n), 1)
    assert c & (c - 1) == 0
    strict = (cc & (c - 1)) < (rr & (c - 1))
    incl = (cc & (c - 1)) <= (rr & (c - 1))
    eye = (rr == cc).astype(F32)
    eye_w = (lax.broadcasted_iota(jnp.int32, (BLK, BLK), 0) == lax.broadcasted_iota(jnp.int32, (BLK, BLK), 1)).astype(F32)
    lane_head = lax.broadcasted_iota(jnp.int32, (c, BLK), 1) // HEAD_DIM

    def stack(x):
        return jnp.concatenate([jnp.where(lane_head == h, x, 0.0) for h in range(HEADS)], axis=0)

    def unstack(xw):
        out = xw[0:c]
        for h in range(1, HEADS):
            out = out + xw[h * c:(h + 1) * c]
        return out

    chunks = range(per_step)
    rows = [slice(s * c, (s + 1) * c) for s in chunks]
    lk, lr, rb, rk, vw, be_w, ke_w, w_c = [], [], [], [], [], [], [], []
    for s in chunks:
        logd = logd_ref[0, rows[s], :]
        logw = _dot(cum, logd, HIGHEST)
        logw_c = logw[c - 1:c, :]
        e_neg = jnp.exp(-logw)
        e_end = jnp.exp(logw_c - logw)
        b = b_ref[0, rows[s], :]
        k = k_ref[0, rows[s], :]
        lk.append(stack(kk_ref[0, rows[s], :] * jnp.exp(logw - logd)))
        lr.append(stack(r_ref[0, rows[s], :] * jnp.exp(logw)))
        rb.append(stack(b * e_neg))
        rk.append(stack(k * e_neg))
        be_w.append(stack(b * e_end))
        ke_w.append(stack(k * e_end))
        vw.append(stack(v_ref[0, rows[s], :]))
        w_c.append(jnp.exp(logw_c))

    lhs = [jnp.concatenate([lk[s], lr[s]], axis=0) for s in chunks]
    ab_mb = [_mm(lhs[s], rb[s], _NT) for s in chunks]
    ak_mk = [_mm(lhs[s], rk[s], _NT) for s in chunks]
    m_b = [jnp.where(incl, ab_mb[s][n:], 0.0) for s in chunks]
    am_k = [jnp.concatenate([jnp.where(strict, ak_mk[s][:n], 0.0), jnp.where(incl, ak_mk[s][n:], 0.0)], axis=0)
            for s in chunks]
    npow = [jnp.where(strict, -ab_mb[s][:n], 0.0) for s in chunks]
    tinv = [eye + npow[s] for s in chunks]
    for _ in range(int(math.log2(c)) - 1):
        npow = [_mm(npow[s], npow[s], _NN) for s in chunks]
        tinv = [tinv[s] + _mm(tinv[s], npow[s], _NN) for s in chunks]
    akv_mkv = [_mm(am_k[s], vw[s], _NN) for s in chunks]
    pq = [_mm(tinv[s], jnp.concatenate([lk[s], akv_mkv[s][:n]], axis=1), _NN) for s in chunks]
    mb_pq = [_mm(m_b[s], pq[s], _NN) for s in chunks]
    for s in chunks:
        r2_ref[0, rows[s], :] = unstack(lr[s] - mb_pq[s][:, :BLK])
        y0_ref[0, rows[s], :] = unstack(akv_mkv[s][n:] - mb_pq[s][:, BLK:])
    for s in chunks:
        p_w, q_w = pq[s][:, :BLK], pq[s][:, BLK:]
        gt_ref[0, s] = eye_w * w_c[s] - _mm(be_w[s], p_w, _TN)
        ht_ref[0, s] = _mm(ke_w[s], vw[s], _TN) - _mm(be_w[s], q_w, _TN)


def _chunks_per_step(nc):
    return math.gcd(nc, RWKV_CHUNKS_PER_STEP)


def _rwkv_chunks(r, k, v, logd, kk, b, chunk):
    bsz, t, _ = r.shape
    nc = t // chunk
    ps = _chunks_per_step(nc)
    tok = pl.BlockSpec((1, ps * chunk, BLK), lambda bi, c: (bi, c, 0))
    mat = pl.BlockSpec((1, ps, BLK, BLK), lambda bi, c: (bi, c, 0, 0))
    mat_shape = jax.ShapeDtypeStruct((bsz, nc, BLK, BLK), F32)
    return pl.pallas_call(
        functools.partial(_rwkv_chunk_kernel, chunk=chunk, per_step=ps),
        grid=(bsz, nc // ps),
        in_specs=[tok] * 6,
        out_specs=[tok, tok, mat, mat],
        out_shape=[jax.ShapeDtypeStruct((bsz, t, BLK), F32)] * 2 + [mat_shape] * 2,
        compiler_params=_cparams(("parallel", "parallel")),
        name="rwkv_chunks",
    )(r, k, v, logd, kk, b)


def _rwkv_scan_kernel(r2_ref, y0_ref, gt_ref, ht_ref, gate_ref, bonus_ref, lnw_ref, lnb_ref, s0_ref,
                      o_ref, st_ref, s_scr, *, chunk, per_step, seqs):
    step = pl.program_id(1)

    @pl.when(step == 0)
    def _():
        s_scr[...] = jnp.zeros_like(s_scr)
        for q in range(seqs):
            for h, sl in enumerate(_head_slices()):
                s_scr[q, sl, sl] = s0_ref[q, h]

    lnw = lnw_ref[...]
    lnb = lnb_ref[...]
    sts = [[s_scr[q] for q in range(seqs)]]
    for i in range(per_step):
        sts.append([_mm(gt_ref[q, i], sts[i][q], _NN) + ht_ref[q, i] for q in range(seqs)])
    for q in range(seqs):
        s_scr[q] = sts[per_step][q]
    rows = [slice(i * chunk, (i + 1) * chunk) for i in range(per_step)]
    pairs = [(q, i) for q in range(seqs) for i in range(per_step)]
    ys = [_mm(r2_ref[q, rows[i], :], sts[i][q], _NN) + y0_ref[q, rows[i], :] for q, i in pairs]
    pieces = [(n, q, i, sl) for n, (q, i) in enumerate(pairs) for sl in _head_slices()]
    yh = [ys[n][:, sl] for n, q, i, sl in pieces]
    mu = [jnp.mean(x, axis=-1, keepdims=True) for x in yh]
    d = [yh[m] - mu[m] for m in range(len(pieces))]
    var = [jnp.mean(x * x, axis=-1, keepdims=True) for x in d]
    for m, (n, q, i, sl) in enumerate(pieces):
        yn = d[m] * lax.rsqrt(var[m] + RWKV_GN_EPS) * lnw[:, sl] + lnb[:, sl]
        o_ref[q, rows[i], sl] = (yn + bonus_ref[q, rows[i], sl]) * gate_ref[q, rows[i], sl]

    @pl.when(step == pl.num_programs(1) - 1)
    def _():
        for q in range(seqs):
            for h, sl in enumerate(_head_slices()):
                st_ref[q, h] = sts[per_step][q][sl, sl]


SCAN_SEQS_PER_STEP = 4


def _rwkv_scan(r2, y0, gt, ht, gate, bonus, ln_w, ln_b, st0, chunk):
    bsz, t, _ = r2.shape
    nc = t // chunk
    ps = _chunks_per_step(nc)
    nq = math.gcd(bsz, SCAN_SEQS_PER_STEP)
    tok = pl.BlockSpec((nq, ps * chunk, BLK), lambda bi, c: (bi, c, 0))
    mat = pl.BlockSpec((nq, ps, BLK, BLK), lambda bi, c: (bi, c, 0, 0))
    state = pl.BlockSpec((nq, HEADS, HEAD_DIM, HEAD_DIM), lambda bi, c: (bi, 0, 0, 0))
    row = pl.BlockSpec((1, BLK), lambda bi, c: (0, 0))
    return pl.pallas_call(
        functools.partial(_rwkv_scan_kernel, chunk=chunk, per_step=ps, seqs=nq),
        grid=(bsz // nq, nc // ps),
        in_specs=[tok, tok, mat, mat, tok, tok, row, row, state],
        out_specs=[tok, state],
        out_shape=[jax.ShapeDtypeStruct((bsz, t, BLK), F32),
                   jax.ShapeDtypeStruct((bsz, HEADS, HEAD_DIM, HEAD_DIM), F32)],
        scratch_shapes=[pltpu.VMEM((nq, BLK, BLK), F32)],
        compiler_params=_cparams(("parallel", "arbitrary")),
        name="rwkv_scan",
    )(r2, y0, gt, ht, gate, bonus, ln_w.reshape(1, BLK), ln_b.reshape(1, BLK), st0)


def _rwkv(h3, shift0, s0, n_valid, p, chunk, tm):
    bsz, t, cols = h3.shape
    outs = _rwkv_prep(h3.reshape(bsz * t, cols), shift0, t, n_valid,
                      p["mu"], p["w0"], p["a0"], p["k_k"], p["k_a"], p["r_k"], p["w_up"], p["a_up"], p["g_up"], tm)
    r, k, v, logd, kk, b, gate, bonus = [o.reshape(bsz, t, BLK) for o in outs]
    r2, y0, gt, ht = _rwkv_chunks(r, k, v, logd, kk, b, chunk)
    out, st = _rwkv_scan(r2, y0, gt, ht, gate, bonus, p["ln_w"], p["ln_b"], jnp.swapaxes(s0, 2, 3), chunk)
    return out, jnp.swapaxes(st, 2, 3)


def _reorder_w_in(w):
    fox_end = 3 * BRANCH_W
    ff_end = fox_end + HEADS
    gate_start = w.shape[-1] - N_BRANCH * D_MODEL
    pad = jnp.zeros(w.shape[:-1] + (PROJ_COLS - COL_FORGET - HEADS,), w.dtype)
    out = jnp.concatenate([w[..., gate_start:], w[..., :fox_end], w[..., ff_end:gate_start],
                           w[..., fox_end:ff_end], pad], axis=-1)
    assert out.shape[-1] == PROJ_COLS
    return out.astype(BF16)


def _heads4(x):
    return x.reshape(x.shape[:-1] + (HEADS, HEAD_DIM))


def kernel(x_prompt, x_sample, cache_fox_k, cache_fox_v, cache_fox_logf, cache_sb_k, cache_sb_v, state_ret, state_rwkv, state_rwkv_shift, page_table, w_in, fox_f_bias, ret_gn_w, rwkv_mu, rwkv_w0, rwkv_w_up, rwkv_a0, rwkv_a_up, rwkv_g_up, rwkv_k_k, rwkv_k_a, rwkv_r_k, rwkv_ln_w, rwkv_ln_b, w_branch, w_out, norm_ffn1, ffn1_w_in, ffn1_w_out, norm_mix, norm_ffn2, ffn2_w_in, ffn2_w_out, norm_final):
    depth = w_in.shape[0]
    bp, t, d = x_prompt.shape
    bs = x_sample.shape[0]
    n_pool = cache_fox_k.shape[1]
    past_len = page_table.shape[1] * PAGE_SIZE

    w_in_r = _reorder_w_in(w_in)
    f1i, f1o, f2i, f2o = (a.astype(BF16) for a in (ffn1_w_in, ffn1_w_out, ffn2_w_in, ffn2_w_out))
    wbr = w_branch.astype(BF16)
    wo = w_out.astype(BF16)
    w_up, a_up, g_up = rwkv_w_up.astype(BF16), rwkv_a_up.astype(BF16), rwkv_g_up.astype(BF16)

    pool = lambda c: jnp.transpose(c, (0, 1, 3, 4, 2)).reshape(depth * n_pool, BLK, PAGE_SIZE)
    ck_fox, cv_fox, ck_sb, cv_sb = pool(cache_fox_k), pool(cache_fox_v), pool(cache_sb_k), pool(cache_sb_v)
    lf_t = jnp.swapaxes(cache_fox_logf, 2, 3).reshape(depth * n_pool, HEADS, PAGE_SIZE)

    cos_p, sin_p = _rope_tables(jnp.arange(t))
    s_pad = 8
    cos_s, sin_s = _rope_tables(past_len + jnp.arange(s_pad))
    ret_tab_p = _ret_tables(RET_CHUNK, RET_CHUNK)
    ret_tab_s = _ret_tables(s_pad, 1)
    zeros_state = jnp.zeros((bp, HEADS, HEAD_DIM, HEAD_DIM), F32)

    xp = x_prompt.reshape(bp * t, d)
    xs = x_sample.reshape(bs, d)
    new_p = [[] for _ in range(8)]
    new_s = [[] for _ in range(8)]
    tm_p = 1024
    tm_wide = min(2 * tm_p, bp * t)

    for l in range(depth):
        last = l == depth - 1
        rw = dict(mu=rwkv_mu[l], w0=rwkv_w0[l], a0=rwkv_a0[l], k_k=rwkv_k_k[l], k_a=rwkv_k_a[l], r_k=rwkv_r_k[l],
                  w_up=w_up[l], a_up=a_up[l], g_up=g_up[l], ln_w=rwkv_ln_w[l], ln_b=rwkv_ln_b[l])

        xp = _ffn(xp, norm_ffn1[l], f1i[l], f1o[l], norm_final, False, tm_wide)
        h = _proj(xp, norm_mix[l], w_in_r[l], tm_wide)
        h3 = h.reshape(bp, t, PROJ_COLS)
        ff_t = jnp.swapaxes(h3[:, :, COL_FORGET:COL_FORGET + HEADS], 1, 2).reshape(bp * HEADS, t)
        lf_rows, *cum_terms = _forget(ff_t, jnp.tile(fox_f_bias[l], bp)[:, None])
        by_token = lambda rows: jnp.swapaxes(rows.reshape(bp, HEADS, t), 1, 2)
        lf = by_token(lf_rows)
        terms = jnp.concatenate([by_token(c) for c in cum_terms]
                                + [jnp.zeros((bp, t, TERM_COLS - 3 * HEADS), BF16)], axis=-1)
        packed = _pack_operands(h, terms.reshape(bp * t, TERM_COLS), tm_p)
        fqa, fka, fva, sqa, ska, sva = [a.reshape(bp, t, HEADS * PACK) for a in packed]
        o_a = _fox_prompt(fqa, fka, fva)
        o_b, ret_p = _retention(h3, COL_RET, cos_p, sin_p, ret_tab_p, ret_gn_w[l], zeros_state, RET_CHUNK)
        o_c, rw_p = _rwkv(h3, jnp.zeros((bp, RWKV_IN), F32), zeros_state, t, rw, RWKV_CHUNK, tm_p)
        o_d = _sb_prompt(sqa, ska, sva)
        flat = lambda o: o.reshape(bp * t, BLK)
        xp = _merge(xp, [flat(o_a), flat(o_b), flat(o_c), flat(o_d)], h, wbr[l], wo[l], 512)
        xp = _ffn(xp, norm_ffn2[l], f2i[l], f2o[l], norm_final, last, tm_wide)
        for i, a in enumerate((_heads4(h3[:, :, COL_FOX + BLK:COL_FOX + 2 * BLK]),
                               _heads4(h3[:, :, COL_FOX + 2 * BLK:COL_FOX + 3 * BLK]), lf,
                               _heads4(h3[:, :, COL_SB + BLK:COL_SB + 2 * BLK]),
                               _heads4(h3[:, :, COL_SB + 2 * BLK:COL_SB + 3 * BLK]), ret_p, rw_p,
                               h3[:, -1, COL_RWKV:COL_RWKV + RWKV_IN])):
            new_p[i].append(a)

        xs = _ffn(xs, norm_ffn1[l], f1i[l], f1o[l], norm_final, False, bs)
        hs = _proj(xs, norm_mix[l], w_in_r[l], bs)
        col = lambda c0, n=BLK: hs[:, c0:c0 + n]
        ff_s = jnp.swapaxes(col(COL_FORGET, HEADS), 0, 1)
        ff_s = jnp.concatenate([ff_s, jnp.zeros((8 - HEADS, bs), F32)], axis=0)
        bias_s = jnp.concatenate([fox_f_bias[l], jnp.zeros((8 - HEADS,), F32)])[:, None]
        lf_s_rows = _forget(jnp.pad(ff_s, ((0, 0), (0, 128 - bs))), bias_s)[0]
        lf_s = jnp.swapaxes(lf_s_rows[:HEADS, :bs], 0, 1)
        o_a = _fox_sample(col(COL_FOX), col(COL_FOX + BLK), col(COL_FOX + 2 * BLK), lf_s,
                          ck_fox, cv_fox, lf_t, page_table, l * n_pool)
        hs_pad = jnp.pad(hs[:, None, :], ((0, 0), (0, s_pad - 1), (0, 0)))
        o_b, ret_s = _retention(hs_pad, COL_RET, cos_s, sin_s, ret_tab_s, ret_gn_w[l], state_ret[l], s_pad)
        o_c, rw_s = _rwkv(hs_pad, state_rwkv_shift[l], state_rwkv[l], 1, rw, s_pad, s_pad)
        o_d = _sb_sample(col(COL_SB), ck_sb, cv_sb, page_table, l * n_pool)
        xs = _merge(xs, [o_a, o_b[:, 0], o_c[:, 0], o_d], hs, wbr[l], wo[l], bs)
        xs = _ffn(xs, norm_ffn2[l], f2i[l], f2o[l], norm_final, last, bs)
        for i, a in enumerate((_heads4(col(COL_FOX + BLK))[:, None], _heads4(col(COL_FOX + 2 * BLK))[:, None],
                               lf_s[:, None], _heads4(col(COL_SB + BLK))[:, None],
                               _heads4(col(COL_SB + 2 * BLK))[:, None], ret_s, rw_s, col(COL_RWKV, RWKV_IN))):
            new_s[i].append(a)

    sp = [jnp.stack(v) for v in new_p]
    ss = [jnp.stack(v) for v in new_s]
    return (xp.reshape(bp, t, d), xs.reshape(bs, 1, d), sp[0], sp[1], sp[2], sp[3], sp[4], sp[5], sp[6], sp[7],
            ss[0], ss[1], ss[2], ss[3], ss[4], ss[5], ss[6], ss[7])
```
